```python
import math
import jax, jax.numpy as jnp
from jax import lax
import numpy as np

D_MODEL = 1024
BATCH = 32
SEQ = 2048
DEPTH = 2

CTX_LEN = 256
GRID_W = 64
EPS = 1e-6

N_HEADS = 8
HEAD_DIM = 64
V_DIM = 2 * HEAD_DIM
Q_WIDTH = N_HEADS * 2 * HEAD_DIM
K_WIDTH = N_HEADS * 2 * HEAD_DIM
V_WIDTH = N_HEADS * V_DIM
ATTN_WIDTH = V_WIDTH
Q_BLOCK = 128
ROPE_THETA = 10000.0
AXIS_DIM = HEAD_DIM // 2
ROPE_PAIRS = AXIS_DIM // 2

POOL_WINDOWS = (2, 4, 8, 16)
POOL_GROUPS = len(POOL_WINDOWS)
POOL_WIDTH = 512
POOL_GROUP_DIM = POOL_WIDTH // POOL_GROUPS

N_BRANCHES = 2

OFF_POOL = 0
OFF_Q = OFF_POOL + POOL_WIDTH
OFF_K = OFF_Q + Q_WIDTH
OFF_V = OFF_K + K_WIDTH
OFF_G = OFF_V + V_WIDTH
IN_WIDTH = OFF_G + N_BRANCHES * D_MODEL

D_FF = 2752
N_EXPERTS = 8
TOP_K = 2
D_FF_EXPERT = 1408
N_DENSE = (DEPTH + 1) // 2
N_MOE = DEPTH // 2

kernel_name = "hybrid_pool_diffattn_moe_dit"


def rms_norm(x, g):
    xf = x.astype(jnp.float32)
    y = xf * lax.rsqrt(jnp.mean(xf * xf, axis=-1, keepdims=True) + EPS)
    return (y * g.astype(jnp.float32)).astype(x.dtype)


def modulate(h, shift, scale):
    return h * (1 + scale) + shift


def lambda_init(layer):
    return 0.8 - 0.6 * math.exp(-0.3 * layer)


def axial_rope_tables(L, dtype):
    rows = L // GRID_W
    row = jnp.repeat(jnp.arange(rows, dtype=jnp.float32), GRID_W)
    col = jnp.tile(jnp.arange(GRID_W, dtype=jnp.float32), rows)
    inv = ROPE_THETA ** (-jnp.arange(ROPE_PAIRS, dtype=jnp.float32) * 2.0 / AXIS_DIM)
    ang = jnp.stack([row[:, None] * inv, col[:, None] * inv], axis=1)
    return jnp.cos(ang).astype(dtype), jnp.sin(ang).astype(dtype)


def apply_axial_rope(u, cos, sin):
    shp = u.shape
    ur = u.reshape(shp[:-1] + (2, 2, ROPE_PAIRS))
    a, b = ur[..., 0, :], ur[..., 1, :]
    cs, sn = cos[:, None, None], sin[:, None, None]
    out = jnp.stack([a * cs - b * sn, a * sn + b * cs], axis=-2)
    return out.reshape(shp)


def centred_pool_minus_self(u):
    L = u.shape[1]
    uf = u.astype(jnp.float32)
    cs = jnp.concatenate([jnp.zeros_like(uf[:, :1]), jnp.cumsum(uf, axis=1)], axis=1)
    t = np.arange(L)
    means = []
    for g, w in enumerate(POOL_WINDOWS):
        lo = np.clip(t - w // 2, 0, L)
        hi = np.clip(t + w // 2, 0, L)
        cnt = (hi - lo).astype(np.float32)[None, :, None]
        csg = cs[..., g * POOL_GROUP_DIM:(g + 1) * POOL_GROUP_DIM]
        means.append((csg[:, hi] - csg[:, lo]) / cnt)
    return (jnp.concatenate(means, axis=-1) - uf).astype(u.dtype)


def pool_branch(u, pool_w_l, pool_scale_l):
    B, L, _ = u.shape
    m = centred_pool_minus_self(u).reshape(B, L, POOL_GROUPS, POOL_GROUP_DIM)
    y = jnp.einsum('blgc,gcd->blgd', m, pool_w_l).reshape(B, L, POOL_WIDTH)
    return y * pool_scale_l


def split_heads_q(zq):
    B, L, _ = zq.shape
    return zq.reshape(B, L, N_HEADS, 2, HEAD_DIM)


def split_heads_v(zv):
    B, L, _ = zv.shape
    return zv.reshape(B, L, N_HEADS, V_DIM)


def diff_attend(q, k, v, lam):
    s = jnp.einsum('bqhmd,bkhmd->bhmqk', q, k).astype(jnp.float32)
    p = jax.nn.softmax(s, axis=-1)
    a = p[:, :, 0] - lam * p[:, :, 1]
    return jnp.einsum('bhqk,bkhe->bqhe', a.astype(v.dtype), v)


def latent_attention(q, k_all, v_all, lam):
    B, L = q.shape[:2]
    nb = L // Q_BLOCK
    qb = q.reshape(B, nb, Q_BLOCK, N_HEADS, 2, HEAD_DIM).swapaxes(0, 1)
    ob = lax.map(lambda qq: diff_attend(qq, k_all, v_all, lam), qb)
    return ob.swapaxes(0, 1).reshape(B, L, N_HEADS, V_DIM)


def head_norm(o, g, lam0):
    B, L = o.shape[:2]
    return (rms_norm(o, g) * (1.0 - lam0)).reshape(B, L, ATTN_WIDTH)


def merge_branches(z, pool_y, attn_y, w_pool_proj_l, w_attn_proj_l, w_out_l):
    g = jax.nn.sigmoid(z[..., OFF_G:].astype(jnp.float32)).astype(z.dtype)
    g_pool, g_attn = g[..., :D_MODEL], g[..., D_MODEL:]
    y = g_pool * (pool_y @ w_pool_proj_l) + g_attn * (attn_y @ w_attn_proj_l)
    return y @ w_out_l


def swiglu(t, w1, w3, w2):
    return (jax.nn.silu(t @ w1) * (t @ w3)) @ w2


def moe_swiglu(h, w_router, we1, we3, we2):
    B, L, D = h.shape
    t = h.reshape(B * L, D)
    logits = (t @ w_router).astype(jnp.float32)
    top_v, top_i = lax.top_k(logits, TOP_K)
    wts = jax.nn.softmax(top_v, axis=-1)
    gate = jnp.sum(jax.nn.one_hot(top_i, N_EXPERTS, dtype=jnp.float32) * wts[..., None], axis=1)
    out = jnp.zeros((B * L, D), jnp.float32)
    for e in range(N_EXPERTS):
        out = out + gate[:, e:e + 1] * swiglu(t, we1[e], we3[e], we2[e]).astype(jnp.float32)
    return out.astype(h.dtype).reshape(B, L, D)


def channel_mixer(l, h, ffn_w1, ffn_w3, ffn_w2, router_w, moe_w1, moe_w3, moe_w2):
    if l % 2 == 0:
        i = l // 2
        B, L, D = h.shape
        return swiglu(h.reshape(B * L, D), ffn_w1[i], ffn_w3[i], ffn_w2[i]).reshape(B, L, D)
    i = l // 2
    return moe_swiglu(h, router_w[i], moe_w1[i], moe_w3[i], moe_w2[i])


def setup_inputs(seed: int = 0) -> dict:
    key = jax.random.key(seed)
    ks = iter(jax.random.split(key, 32))

    def nrm(shape, scale):
        return jax.random.normal(next(ks), shape, jnp.float32) * scale

    D = D_MODEL
    return {
        "x": nrm((BATCH, SEQ, D), 1.0),
        "c": nrm((BATCH, D), 1.0),
        "ctx": nrm((BATCH, CTX_LEN, D), 1.0),
        "c_ctx": nrm((D,), 1.0),
        "w_mod": nrm((DEPTH, D, 6 * D), 0.5 * D ** -0.5),
        "b_mod": nrm((DEPTH, 6 * D), 0.02),
        "norm1_g": 1.0 + nrm((DEPTH, D), 0.02),
        "norm2_g": 1.0 + nrm((DEPTH, D), 0.02),
        "w_in": nrm((DEPTH, D, IN_WIDTH), D ** -0.5),
        "pool_w": nrm((DEPTH, POOL_GROUPS, POOL_GROUP_DIM, POOL_GROUP_DIM), POOL_GROUP_DIM ** -0.5),
        "pool_scale": 1.0 + nrm((DEPTH, POOL_WIDTH), 0.1),
        "lam_q1": nrm((DEPTH, HEAD_DIM), 0.1),
        "lam_k1": nrm((DEPTH, HEAD_DIM), 0.1),
        "lam_q2": nrm((DEPTH, HEAD_DIM), 0.1),
        "lam_k2": nrm((DEPTH, HEAD_DIM), 0.1),
        "subln_g": 1.0 + nrm((DEPTH, V_DIM), 0.02),
        "w_pool_proj": nrm((DEPTH, POOL_WIDTH, D), POOL_WIDTH ** -0.5),
        "w_attn_proj": nrm((DEPTH, ATTN_WIDTH, D), ATTN_WIDTH ** -0.5),
        "w_out": nrm((DEPTH, D, D), D ** -0.5),
        "ffn_w1": nrm((N_DENSE, D, D_FF), D ** -0.5),
        "ffn_w3": nrm((N_DENSE, D, D_FF), D ** -0.5),
        "ffn_w2": nrm((N_DENSE, D_FF, D), D_FF ** -0.5),
        "router_w": nrm((N_MOE, D, N_EXPERTS), D ** -0.5),
        "moe_w1": nrm((N_MOE, N_EXPERTS, D, D_FF_EXPERT), D ** -0.5),
        "moe_w3": nrm((N_MOE, N_EXPERTS, D, D_FF_EXPERT), D ** -0.5),
        "moe_w2": nrm((N_MOE, N_EXPERTS, D_FF_EXPERT, D), D_FF_EXPERT ** -0.5),
        "final_g": 1.0 + nrm((D,), 0.02),
    }


def reference(x, c, ctx, c_ctx, w_mod, b_mod, norm1_g, norm2_g, w_in, pool_w, pool_scale,
              lam_q1, lam_k1, lam_q2, lam_k2, subln_g, w_pool_proj, w_attn_proj, w_out,
              ffn_w1, ffn_w3, ffn_w2, router_w, moe_w1, moe_w3, moe_w2, final_g):
    L = x.shape[1]
    cos, sin = axial_rope_tables(L, x.dtype)
    q_scale = HEAD_DIM ** -0.5
    xc = ctx
    s_lat = jax.nn.silu(c)
    s_ctx = jax.nn.silu(c_ctx)
    for l in range(DEPTH):
        last = l == DEPTH - 1
        lam0 = lambda_init(l)
        lam = (jnp.exp(jnp.sum(lam_q1[l].astype(jnp.float32) * lam_k1[l].astype(jnp.float32)))
               - jnp.exp(jnp.sum(lam_q2[l].astype(jnp.float32) * lam_k2[l].astype(jnp.float32)))
               + lam0)
        mod = jnp.split((s_lat @ w_mod[l] + b_mod[l])[:, None, :], 6, axis=-1)
        modc = jnp.split(s_ctx @ w_mod[l] + b_mod[l], 6, axis=-1)

        h = modulate(rms_norm(x, norm1_g[l]), mod[0], mod[1])
        hc = modulate(rms_norm(xc, norm1_g[l]), modc[0], modc[1])
        z = h @ w_in[l]
        if last:
            kvc = hc @ w_in[l][:, OFF_K:OFF_G]
            kc, vc = kvc[..., :K_WIDTH], kvc[..., K_WIDTH:]
        else:
            zc = hc @ w_in[l]
            kc, vc = zc[..., OFF_K:OFF_V], zc[..., OFF_V:OFF_G]
        kc_h = split_heads_q(kc)
        vc_h = split_heads_v(vc)

        q = apply_axial_rope(split_heads_q(z[..., OFF_Q:OFF_K]), cos, sin) * q_scale
        k = apply_axial_rope(split_heads_q(z[..., OFF_K:OFF_V]), cos, sin)
        v = split_heads_v(z[..., OFF_V:OFF_G])
        k_all = jnp.concatenate([kc_h, k], axis=1)
        v_all = jnp.concatenate([vc_h, v], axis=1)
        attn_y = head_norm(latent_attention(q, k_all, v_all, lam), subln_g[l], lam0)
        pool_y = pool_branch(z[..., OFF_POOL:OFF_Q], pool_w[l], pool_scale[l])
        x = x + mod[2] * merge_branches(z, pool_y, attn_y, w_pool_proj[l], w_attn_proj[l], w_out[l])

        if not last:
            qc = split_heads_q(zc[..., OFF_Q:OFF_K]) * q_scale
            attn_yc = head_norm(diff_attend(qc, kc_h, vc_h, lam), subln_g[l], lam0)
            pool_yc = pool_branch(zc[..., OFF_POOL:OFF_Q], pool_w[l], pool_scale[l])
            xc = xc + modc[2] * merge_branches(zc, pool_yc, attn_yc, w_pool_proj[l], w_attn_proj[l], w_out[l])

        h2 = modulate(rms_norm(x, norm2_g[l]), mod[3], mod[4])
        x = x + mod[5] * channel_mixer(l, h2, ffn_w1, ffn_w3, ffn_w2, router_w, moe_w1, moe_w3, moe_w2)
        if not last:
            h2c = modulate(rms_norm(xc, norm2_g[l]), modc[3], modc[4])
            xc = xc + modc[5] * channel_mixer(l, h2c, ffn_w1, ffn_w3, ffn_w2, router_w, moe_w1, moe_w3, moe_w2)
    return rms_norm(x, final_g)
```

```python
import functools
import math

import jax
import jax.numpy as jnp
from jax import lax
from jax.experimental import pallas as pl
from jax.experimental.pallas import tpu as pltpu

F32 = jnp.float32
BF16 = jnp.bfloat16

D_MODEL = 1024
EPS = 1e-6
GRID_W = 64
N_HEADS = 8
HEAD_DIM = 64
V_DIM = 2 * HEAD_DIM
ROPE_THETA = 10000.0
AXIS_DIM = HEAD_DIM // 2
ROPE_PAIRS = AXIS_DIM // 2
POOL_WINDOWS = (2, 4, 8, 16)
POOL_WIDTH = 512
POOL_GROUP_DIM = POOL_WIDTH // len(POOL_WINDOWS)
N_EXPERTS = 8
Q_SCALE = HEAD_DIM ** -0.5

QKV_WIDTH = N_HEADS * V_DIM
COL_G = 0
COL_Q = COL_G + 2 * D_MODEL
COL_K = COL_Q + QKV_WIDTH
COL_V = COL_K + QKV_WIDTH
COL_P = COL_V + QKV_WIDTH
IN_WIDTH = COL_P + POOL_WIDTH
IN_CHUNK = 512

LANES = 128
ONES_ROWS = 16
MOD_ROWS = 40

VMEM_LIMIT = 56 * 1024 * 1024


def _resident(shape):
    nd = len(shape)
    return pl.BlockSpec(shape, lambda *_: (0,) * nd, pipeline_mode=pl.Buffered(1))


def _params(*sem):
    return pltpu.CompilerParams(dimension_semantics=sem, vmem_limit_bytes=VMEM_LIMIT)


def _sigmoid(v):
    return 1.0 / (1.0 + jnp.exp(-v))


def _rms(v):
    return v * lax.rsqrt(jnp.mean(v * v, axis=-1, keepdims=True) + EPS)


def _mod_kernel(s_ref, w_ref, b_ref, o_ref):
    s = s_ref[...]
    s = s * _sigmoid(s)
    w = w_ref[...]
    s_hi = s.astype(BF16)
    s_lo = (s - s_hi.astype(F32)).astype(BF16)
    w_hi = w.astype(BF16)
    w_lo = (w - w_hi.astype(F32)).astype(BF16)
    acc = jnp.dot(s_hi, w_hi, preferred_element_type=F32)
    acc += jnp.dot(s_hi, w_lo, preferred_element_type=F32)
    acc += jnp.dot(s_lo, w_hi, preferred_element_type=F32)
    o_ref[...] = acc + b_ref[...]


def _modulation(s_in, w_mod, b_mod):
    depth, d, n = w_mod.shape
    tn = 1024
    return pl.pallas_call(
        _mod_kernel,
        grid=(depth, n // tn),
        in_specs=[
            pl.BlockSpec((MOD_ROWS, d), lambda l, j: (0, 0)),
            pl.BlockSpec((None, d, tn), lambda l, j: (l, 0, j)),
            pl.BlockSpec((None, 1, tn), lambda l, j: (l, 0, j)),
        ],
        out_specs=pl.BlockSpec((None, MOD_ROWS, tn), lambda l, j: (l, 0, j)),
        out_shape=jax.ShapeDtypeStruct((depth, MOD_ROWS, n), F32),
        compiler_params=_params("parallel", "parallel"),
        name="modulation",
    )(s_in, w_mod, b_mod.reshape(depth, 1, n))


def _rope(z, c, s_up, s_dn):
    up = pltpu.roll(z, LANES - ROPE_PAIRS, 1)
    dn = pltpu.roll(z, ROPE_PAIRS, 1)
    return z * c + up * s_up + dn * s_dn


def _inproj_kernel(*refs, kinds, rope):
    if rope:
        x_ref, mod_ref, g_ref, w_ref, c_ref, su_ref, sd_ref, o_ref = refs
    else:
        x_ref, mod_ref, g_ref, w_ref, o_ref = refs
    y = _rms(x_ref[...]) * g_ref[...]
    h = (y * (1.0 + mod_ref[1:2, :]) + mod_ref[0:1, :]).astype(BF16)
    for ci, kind in enumerate(kinds):
        lo = ci * IN_CHUNK
        z = jnp.dot(h, w_ref[:, lo:lo + IN_CHUNK], preferred_element_type=F32)
        if rope and kind in ("q", "k"):
            c, su, sd = c_ref[...], su_ref[...], sd_ref[...]
            z = jnp.concatenate(
                [_rope(z[:, j:j + LANES], c, su, sd) for j in range(0, IN_CHUNK, LANES)], axis=1)
        if kind == "q":
            z = z * Q_SCALE
        o_ref[:, lo:lo + IN_CHUNK] = z.astype(BF16)


def _inproj(x, mod, mod_row, g, w, kinds, tables, t):
    b, l, d = x.shape
    wout = IN_CHUNK * len(kinds)
    rope = tables is not None
    in_specs = [
        pl.BlockSpec((None, t, d), lambda bi, i: (bi, i, 0)),
        pl.BlockSpec((None, 6, d), lambda bi, i: (mod_row(bi), 0, 0)),
        pl.BlockSpec((1, d), lambda bi, i: (0, 0)),
        _resident((d, wout)),
    ]
    args = [x, mod, g, w]
    if rope:
        in_specs += [pl.BlockSpec((t, LANES), lambda bi, i: (i, 0))] * 3
        args += list(tables)
    return pl.pallas_call(
        functools.partial(_inproj_kernel, kinds=kinds, rope=rope),
        grid=(b, l // t),
        in_specs=in_specs,
        out_specs=pl.BlockSpec((None, t, wout), lambda bi, i: (bi, i, 0)),
        out_shape=jax.ShapeDtypeStruct((b, l, wout), BF16),
        compiler_params=_params("parallel", "parallel"),
        name="inproj",
    )(*args)


def _pool_kernel(u_ref, pw_ref, ps_ref, o_ref):
    l = u_ref.shape[0]
    t = lax.broadcasted_iota(jnp.int32, (l, POOL_GROUP_DIM), 0)

    def shifted(a, k):
        if k > 0:
            return jnp.where(t >= k, pltpu.roll(a, k, 0), 0.0)
        return jnp.where(t < l + k, pltpu.roll(a, l + k, 0), 0.0)

    for gi, w in enumerate(POOL_WINDOWS):
        lo = gi * POOL_GROUP_DIM
        u = u_ref[:, lo:lo + POOL_GROUP_DIM].astype(F32)
        back, fwd, span = u, u, 1
        while span < w // 2:
            back = back + shifted(back, span)
            fwd = fwd + shifted(fwd, -span)
            span *= 2
        win = shifted(back, 1) + fwd
        cnt = (jnp.minimum(t + w // 2, l) - jnp.maximum(t - w // 2, 0)).astype(F32)
        m = (win / cnt - u).astype(BF16)
        y = jnp.dot(m, pw_ref[gi], preferred_element_type=F32)
        o_ref[:, lo:lo + POOL_GROUP_DIM] = (y * ps_ref[:, lo:lo + POOL_GROUP_DIM]).astype(BF16)


def _pool(z, col_block, pool_w, pool_scale):
    b, l, _ = z.shape
    return pl.pallas_call(
        _pool_kernel,
        grid=(b,),
        in_specs=[
            pl.BlockSpec((None, l, POOL_WIDTH), lambda bi: (bi, 0, col_block)),
            _resident(pool_w.shape),
            pl.BlockSpec((1, POOL_WIDTH), lambda bi: (0, 0)),
        ],
        out_specs=pl.BlockSpec((None, l, POOL_WIDTH), lambda bi: (bi, 0, 0)),
        out_shape=jax.ShapeDtypeStruct((b, l, POOL_WIDTH), BF16),
        compiler_params=_params("parallel"),
        name="pool",
    )(z, pool_w, pool_scale)


def _attn_kernel(*refs, n_ctx, n_lat, tq, lam0):
    if n_lat:
        lam_ref, g_ref, q_ref, kc_ref, vc_ref, kl_ref, vl_ref, o_ref, k_sc, vt_sc = refs
    else:
        lam_ref, g_ref, q_ref, kc_ref, vc_ref, o_ref, k_sc, vt_sc = refs
    nk = n_ctx + n_lat
    k_sc[0:n_ctx, :] = kc_ref[...]
    vt_sc[0:V_DIM, 0:n_ctx] = vc_ref[...].astype(F32).T.astype(BF16)
    if n_lat:
        k_sc[n_ctx:nk, :] = kl_ref[...]
        vt_sc[0:V_DIM, n_ctx:nk] = vl_ref[...].astype(F32).T.astype(BF16)
    vt_sc[V_DIM:V_DIM + ONES_ROWS, :] = jnp.ones((ONES_ROWS, nk), BF16)

    lp = lam_ref[...]
    lam = (jnp.exp(jnp.sum(lp[0:1] * lp[1:2], axis=1, keepdims=True))
           - jnp.exp(jnp.sum(lp[2:3] * lp[3:4], axis=1, keepdims=True)) + lam0)
    k_all = k_sc[...]
    vt = vt_sc[...]
    lane = lax.broadcasted_iota(jnp.int32, (tq, V_DIM), 1)
    nq = q_ref.shape[0]
    for c0 in range(0, nq, tq):
        q = q_ref[c0:c0 + tq, :]
        outs = []
        for first in (True, False):
            qm = jnp.where((lane < HEAD_DIM) if first else (lane >= HEAD_DIM), q, jnp.zeros_like(q))
            s_t = lax.dot_general(k_all, qm, (((1,), (1,)), ((), ())), preferred_element_type=F32)
            p_t = jnp.exp(s_t - jnp.max(s_t, axis=0, keepdims=True)).astype(BF16)
            oe = jnp.dot(vt, p_t, preferred_element_type=F32)
            outs.append(oe[0:V_DIM] * (1.0 / oe[V_DIM:V_DIM + 1]))
        o_t = outs[0] - lam * outs[1]
        o_t = o_t * lax.rsqrt(jnp.mean(o_t * o_t, axis=0, keepdims=True) + EPS)
        o_t = o_t * (g_ref[...] * (1.0 - lam0))
        o_ref[c0:c0 + tq, :] = o_t.T.astype(BF16)


def _attention(lam_p, g_col, zq, q_blk, zc, kc_blk, vc_blk, zl, kl_blk, vl_blk, lam0, tq):
    b, nq, _ = zq.shape
    n_ctx = zc.shape[1]
    n_lat = 0 if zl is None else zl.shape[1]
    nk = n_ctx + n_lat

    def col(blk, n):
        return pl.BlockSpec((None, n, V_DIM), lambda bi, h: (bi, 0, blk + h))

    in_specs = [
        pl.BlockSpec(lam_p.shape, lambda bi, h: (0, 0)),
        pl.BlockSpec(g_col.shape, lambda bi, h: (0, 0)),
        col(q_blk, nq), col(kc_blk, n_ctx), col(vc_blk, n_ctx),
    ]
    args = [lam_p, g_col, zq, zc, zc]
    if n_lat:
        in_specs += [col(kl_blk, n_lat), col(vl_blk, n_lat)]
        args += [zl, zl]
    return pl.pallas_call(
        functools.partial(_attn_kernel, n_ctx=n_ctx, n_lat=n_lat, tq=min(tq, nq), lam0=lam0),
        grid=(b, N_HEADS),
        in_specs=in_specs,
        out_specs=pl.BlockSpec((None, nq, V_DIM), lambda bi, h: (bi, 0, h)),
        out_shape=jax.ShapeDtypeStruct((b, nq, QKV_WIDTH), BF16),
        scratch_shapes=[pltpu.VMEM((nk, V_DIM), BF16), pltpu.VMEM((V_DIM + ONES_ROWS, nk), BF16)],
        compiler_params=_params("parallel", "parallel"),
        name="attention",
    )(*args)


def _merge_kernel(*refs, router):
    if router:
        (x_ref, mod_ref, g2_ref, zg_ref, py_ref, ay_ref, wp_ref, wa_ref, wo_ref, rh_ref, rl_ref,
         xo_ref, h2_ref, gate_ref) = refs
    else:
        x_ref, mod_ref, g2_ref, zg_ref, py_ref, ay_ref, wp_ref, wa_ref, wo_ref, xo_ref, h2_ref = refs
    d = x_ref.shape[1]
    g_pool = _sigmoid(zg_ref[:, 0:d].astype(F32))
    g_attn = _sigmoid(zg_ref[:, d:2 * d].astype(F32))
    y = (g_pool * jnp.dot(py_ref[...], wp_ref[...], preferred_element_type=F32)
         + g_attn * jnp.dot(ay_ref[...], wa_ref[...], preferred_element_type=F32))
    o = jnp.dot(y.astype(BF16), wo_ref[...], preferred_element_type=F32)
    xn = x_ref[...] + mod_ref[2:3, :] * o
    xo_ref[...] = xn
    h2 = _rms(xn) * g2_ref[...] * (1.0 + mod_ref[4:5, :]) + mod_ref[3:4, :]
    h2_hi = h2.astype(BF16)
    h2_ref[...] = h2_hi
    if router:
        h2_lo = (h2 - h2_hi.astype(F32)).astype(BF16)
        logits = (jnp.dot(h2_hi, rh_ref[...], preferred_element_type=F32)
                  + jnp.dot(h2_hi, rl_ref[...], preferred_element_type=F32)
                  + jnp.dot(h2_lo, rh_ref[...], preferred_element_type=F32))
        lane = lax.broadcasted_iota(jnp.int32, logits.shape, 1).astype(F32)
        neg = jnp.float32(-jnp.inf)
        logits = jnp.where(lane < N_EXPERTS, logits, neg)
        v1 = jnp.max(logits, axis=1, keepdims=True)
        i1 = jnp.min(jnp.where(logits == v1, lane, float(LANES)), axis=1, keepdims=True)
        rest = jnp.where(lane == i1, neg, logits)
        v2 = jnp.max(rest, axis=1, keepdims=True)
        i2 = jnp.min(jnp.where(rest == v2, lane, float(LANES)), axis=1, keepdims=True)
        e2 = jnp.exp(v2 - v1)
        w1 = 1.0 / (1.0 + e2)
        gate_ref[...] = jnp.where(lane == i1, w1, 0.0) + jnp.where(lane == i2, e2 * w1, 0.0)


def _merge(x, mod, mod_row, g2, z, pool_y, attn_y, wp, wa, wo, router_w, t):
    b, l, d = x.shape
    router = router_w is not None
    tile = lambda w: pl.BlockSpec((None, t, w), lambda bi, i: (bi, i, 0))
    in_specs = [
        tile(d),
        pl.BlockSpec((None, 6, d), lambda bi, i: (mod_row(bi), 0, 0)),
        pl.BlockSpec((1, d), lambda bi, i: (0, 0)),
        tile(2 * d),
        tile(POOL_WIDTH), tile(QKV_WIDTH),
        _resident(wp.shape), _resident(wa.shape), _resident(wo.shape),
    ]
    args = [x, mod, g2, z, pool_y, attn_y, wp, wa, wo]
    out_specs = [tile(d), tile(d)]
    out_shape = [jax.ShapeDtypeStruct((b, l, d), F32), jax.ShapeDtypeStruct((b, l, d), BF16)]
    if router:
        in_specs += [_resident(router_w[0].shape)] * 2
        args += list(router_w)
        out_specs.append(tile(LANES))
        out_shape.append(jax.ShapeDtypeStruct((b, l, LANES), F32))
    return pl.pallas_call(
        functools.partial(_merge_kernel, router=router),
        grid=(b, l // t),
        in_specs=in_specs,
        out_specs=out_specs,
        out_shape=out_shape,
        compiler_params=_params("parallel", "parallel"),
        name="merge",
    )(*args)


def _swiglu(h, w1, w3, w2):
    a = jnp.dot(h, w1, preferred_element_type=F32)
    b = jnp.dot(h, w3, preferred_element_type=F32)
    return jnp.dot((a * _sigmoid(a) * b).astype(BF16), w2, preferred_element_type=F32)


def _ffn_kernel(*refs, final):
    if final:
        x_ref, h2_ref, mod_ref, w1_ref, w3_ref, w2_ref, fg_ref, o_ref = refs
    else:
        x_ref, h2_ref, mod_ref, w1_ref, w3_ref, w2_ref, o_ref = refs
    y = _swiglu(h2_ref[...], w1_ref[...], w3_ref[...], w2_ref[...])
    xn = x_ref[...] + mod_ref[5:6, :] * y
    if final:
        xn = _rms(xn) * fg_ref[...]
    o_ref[...] = xn


def _ffn(x, h2, mod, mod_row, w1, w3, w2, final_g, t):
    b, l, d = x.shape
    final = final_g is not None
    tile = pl.BlockSpec((None, t, d), lambda bi, i: (bi, i, 0))
    in_specs = [tile, tile, pl.BlockSpec((None, 6, d), lambda bi, i: (mod_row(bi), 0, 0)),
                _resident(w1.shape), _resident(w3.shape), _resident(w2.shape)]
    args = [x, h2, mod, w1, w3, w2]
    if final:
        in_specs.append(pl.BlockSpec((1, d), lambda bi, i: (0, 0)))
        args.append(final_g)
    return pl.pallas_call(
        functools.partial(_ffn_kernel, final=final),
        grid=(b, l // t),
        in_specs=in_specs,
        out_specs=tile,
        out_shape=jax.ShapeDtypeStruct((b, l, d), F32),
        compiler_params=_params("parallel", "parallel"),
        name="ffn",
    )(*args)


def _moe_kernel(*refs, final):
    if final:
        x_ref, h2_ref, gate_ref, mod_ref, w1_ref, w3_ref, w2_ref, fg_ref, o_ref, acc_ref = refs
    else:
        x_ref, h2_ref, gate_ref, mod_ref, w1_ref, w3_ref, w2_ref, o_ref, acc_ref = refs
    e = pl.program_id(2)

    @pl.when(e == 0)
    def _():
        acc_ref[...] = jnp.zeros_like(acc_ref)

    y = _swiglu(h2_ref[...], w1_ref[...], w3_ref[...], w2_ref[...])
    gate = gate_ref[...]
    lane = lax.broadcasted_iota(jnp.int32, gate.shape, 1)
    ge = jnp.sum(jnp.where(lane == e, gate, 0.0), axis=1, keepdims=True)
    acc_ref[...] += ge * y

    @pl.when(e == N_EXPERTS - 1)
    def _():
        xn = x_ref[...] + mod_ref[5:6, :] * acc_ref[...]
        if final:
            xn = _rms(xn) * fg_ref[...]
        o_ref[...] = xn


def _moe(x, h2, gate, mod, mod_row, w1, w3, w2, final_g, t):
    b, l, d = x.shape
    final = final_g is not None
    f = w1.shape[2]
    tile = lambda w: pl.BlockSpec((None, t, w), lambda bi, i, e: (bi, i, 0))
    in_specs = [tile(d), tile(d), tile(LANES),
                pl.BlockSpec((None, 6, d), lambda bi, i, e: (mod_row(bi), 0, 0)),
                pl.BlockSpec((None, d, f), lambda bi, i, e: (e, 0, 0)),
                pl.BlockSpec((None, d, f), lambda bi, i, e: (e, 0, 0)),
                pl.BlockSpec((None, f, d), lambda bi, i, e: (e, 0, 0))]
    args = [x, h2, gate, mod, w1, w3, w2]
    if final:
        in_specs.append(pl.BlockSpec((1, d), lambda bi, i, e: (0, 0)))
        args.append(final_g)
    return pl.pallas_call(
        functools.partial(_moe_kernel, final=final),
        grid=(b, l // t, N_EXPERTS),
        in_specs=in_specs,
        out_specs=tile(d),
        out_shape=jax.ShapeDtypeStruct((b, l, d), F32),
        scratch_shapes=[pltpu.VMEM((t, d), F32)],
        compiler_params=_params("parallel", "parallel", "arbitrary"),
        name="moe",
    )(*args)


def _rope_tables(l):
    rows = l // GRID_W
    row = jnp.repeat(jnp.arange(rows, dtype=F32), GRID_W)
    colp = jnp.tile(jnp.arange(GRID_W, dtype=F32), rows)
    inv = ROPE_THETA ** (-jnp.arange(ROPE_PAIRS, dtype=F32) * 2.0 / AXIS_DIM)
    ang_r, ang_c = row[:, None] * inv, colp[:, None] * inv
    zero = jnp.zeros_like(ang_r)
    cos64 = jnp.concatenate([jnp.cos(ang_r)] * 2 + [jnp.cos(ang_c)] * 2, axis=1)
    up64 = jnp.concatenate([-jnp.sin(ang_r), zero, -jnp.sin(ang_c), zero], axis=1)
    dn64 = jnp.concatenate([zero, jnp.sin(ang_r), zero, jnp.sin(ang_c)], axis=1)
    return tuple(jnp.tile(a, (1, LANES // HEAD_DIM)) for a in (cos64, up64, dn64))


def _lambda_init(layer):
    return 0.8 - 0.6 * math.exp(-0.3 * layer)


def _permute_in(w):
    off_q = POOL_WIDTH
    off_g = off_q + 3 * QKV_WIDTH
    return jnp.concatenate([w[:, off_g:], w[:, off_q:off_g], w[:, :off_q]], axis=1).astype(BF16)


def kernel(x, c, ctx, c_ctx, w_mod, b_mod, norm1_g, norm2_g, w_in, pool_w, pool_scale, lam_q1, lam_k1,
           lam_q2, lam_k2, subln_g, w_pool_proj, w_attn_proj, w_out, ffn_w1, ffn_w3, ffn_w2, router_w,
           moe_w1, moe_w3, moe_w2, final_g):
    b, l, d = x.shape
    n_ctx = ctx.shape[1]
    depth = w_mod.shape[0]
    assert d == D_MODEL and b + 1 <= MOD_ROWS and l % 512 == 0 and n_ctx % 256 == 0

    s_in = jnp.concatenate([c, c_ctx[None, :], jnp.zeros((MOD_ROWS - b - 1, d), F32)], axis=0)
    mod_all = _modulation(s_in, w_mod, b_mod)
    tables = _rope_tables(l)
    lat_row = lambda bi: bi
    ctx_row = lambda bi: b
    gq, gk, gv, gp = COL_Q // V_DIM, COL_K // V_DIM, COL_V // V_DIM, COL_P // POOL_WIDTH
    full_kinds = ("g",) * 4 + ("q",) * 2 + ("k",) * 2 + ("v",) * 2 + ("p",)

    xc = ctx
    for layer in range(depth):
        last = layer == depth - 1
        lam0 = _lambda_init(layer)
        mod = mod_all[layer].reshape(MOD_ROWS, 6, d)
        g1 = norm1_g[layer][None, :]
        g2 = norm2_g[layer][None, :]
        w_in_p = _permute_in(w_in[layer])
        lam_p = jnp.stack([lam_q1[layer], lam_k1[layer], lam_q2[layer], lam_k2[layer]])
        g_col = subln_g[layer][:, None]
        pw = pool_w[layer].astype(BF16)
        ps = pool_scale[layer][None, :]
        wp = w_pool_proj[layer].astype(BF16)
        wa = w_attn_proj[layer].astype(BF16)
        wo = w_out[layer].astype(BF16)
        fg = final_g[None, :] if last else None
        moe_layer = layer % 2 == 1
        if moe_layer:
            rw = jnp.pad(router_w[layer // 2], ((0, 0), (0, LANES - N_EXPERTS)))
            rw_hi = rw.astype(BF16)
            rw_lo = (rw - rw_hi.astype(F32)).astype(BF16)
            routers = (rw_hi, rw_lo)
            we1 = moe_w1[layer // 2].astype(BF16)
            we3 = moe_w3[layer // 2].astype(BF16)
            we2 = moe_w2[layer // 2].astype(BF16)
        else:
            routers = None
            wf1 = ffn_w1[layer // 2].astype(BF16)
            wf3 = ffn_w3[layer // 2].astype(BF16)
            wf2 = ffn_w2[layer // 2].astype(BF16)

        def mixer(xs, h2, gate, row, t):
            if moe_layer:
                return _moe(xs, h2, gate, mod, row, we1, we3, we2, fg, t)
            return _ffn(xs, h2, mod, row, wf1, wf3, wf2, fg, min(t, 512))

        z = _inproj(x, mod, lat_row, g1, w_in_p, full_kinds, tables, 512)
        if last:
            zc = _inproj(xc, mod, ctx_row, g1, w_in_p[:, COL_K:COL_P], ("k",) * 2 + ("v",) * 2, None, n_ctx)
            ckb, cvb = 0, QKV_WIDTH // V_DIM
        else:
            zc = _inproj(xc, mod, ctx_row, g1, w_in_p, full_kinds, None, n_ctx)
            ckb, cvb = gk, gv
        attn_y = _attention(lam_p, g_col, z, gq, zc, ckb, cvb, z, gk, gv, lam0, 512)
        pool_y = _pool(z, gp, pw, ps)
        res = _merge(x, mod, lat_row, g2, z, pool_y, attn_y, wp, wa, wo, routers, 512)
        x, h2 = res[0], res[1]
        gate = res[2] if moe_layer else None

        if not last:
            attn_yc = _attention(lam_p, g_col, zc, gq, zc, gk, gv, None, 0, 0, lam0, 512)
            pool_yc = _pool(zc, gp, pw, ps)
            resc = _merge(xc, mod, ctx_row, g2, zc, pool_yc, attn_yc, wp, wa, wo, routers, n_ctx)
            xc, h2c = resc[0], resc[1]
            gatec = resc[2] if moe_layer else None

        x = mixer(x, h2, gate, lat_row, min(1024, l))
        if not last:
            xc = mixer(xc, h2c, gatec, ctx_row, n_ctx)
    return x
```

```python
import functools
import math

import jax
import jax.numpy as jnp
from jax import lax
from jax.experimental import pallas as pl
from jax.experimental.pallas import tpu as pltpu

F32 = jnp.float32
BF16 = jnp.bfloat16

D_MODEL = 1024
EPS = 1e-6
GRID_W = 64
N_HEADS = 8
HEAD_DIM = 64
V_DIM = 2 * HEAD_DIM
ROPE_THETA = 10000.0
AXIS_DIM = HEAD_DIM // 2
ROPE_PAIRS = AXIS_DIM // 2
POOL_WINDOWS = (2, 4, 8, 16)
POOL_WIDTH = 512
POOL_GROUP_DIM = POOL_WIDTH // len(POOL_WINDOWS)
N_EXPERTS = 8
Q_SCALE = HEAD_DIM ** -0.5
LOG2E = math.log2(math.e)
MAX_UNSHIFTED_LOG2 = 80.0
SQ_NORM_SLACK = 1.05

QKV_WIDTH = N_HEADS * V_DIM
COL_G = 0
COL_Q = COL_G + 2 * D_MODEL
COL_K = COL_Q + QKV_WIDTH
COL_V = COL_K + QKV_WIDTH
COL_P = COL_V + QKV_WIDTH
IN_WIDTH = COL_P + POOL_WIDTH
IN_CHUNK = 512

LANES = 128
ONES_ROWS = 16
MOD_ROWS = 40

VMEM_LIMIT = 56 * 1024 * 1024


def _resident(shape):
    nd = len(shape)
    return pl.BlockSpec(shape, lambda *_: (0,) * nd, pipeline_mode=pl.Buffered(1))


def _params(*sem):
    return pltpu.CompilerParams(dimension_semantics=sem, vmem_limit_bytes=VMEM_LIMIT)


def _sigmoid(v):
    return 1.0 / (1.0 + jnp.exp(-v))


def _rms(v):
    return v * lax.rsqrt(jnp.mean(v * v, axis=-1, keepdims=True) + EPS)


def _mod_kernel(s_ref, w_ref, b_ref, o_ref):
    s = s_ref[...]
    s = s * _sigmoid(s)
    w = w_ref[...]
    s_hi = s.astype(BF16)
    s_lo = (s - s_hi.astype(F32)).astype(BF16)
    w_hi = w.astype(BF16)
    w_lo = (w - w_hi.astype(F32)).astype(BF16)
    acc = jnp.dot(s_hi, w_hi, preferred_element_type=F32)
    acc += jnp.dot(s_hi, w_lo, preferred_element_type=F32)
    acc += jnp.dot(s_lo, w_hi, preferred_element_type=F32)
    o_ref[...] = acc + b_ref[...]


def _modulation(s_in, w_mod, b_mod):
    depth, d, n = w_mod.shape
    tn = 1024
    return pl.pallas_call(
        _mod_kernel,
        grid=(depth, n // tn),
        in_specs=[
            pl.BlockSpec((MOD_ROWS, d), lambda l, j: (0, 0)),
            pl.BlockSpec((None, d, tn), lambda l, j: (l, 0, j)),
            pl.BlockSpec((None, 1, tn), lambda l, j: (l, 0, j)),
        ],
        out_specs=pl.BlockSpec((None, MOD_ROWS, tn), lambda l, j: (l, 0, j)),
        out_shape=jax.ShapeDtypeStruct((depth, MOD_ROWS, n), F32),
        compiler_params=_params("parallel", "parallel"),
        name="modulation",
    )(s_in, w_mod, b_mod.reshape(depth, 1, n))


def _rope(z, c, s_up, s_dn):
    up = pltpu.roll(z, LANES - ROPE_PAIRS, 1)
    dn = pltpu.roll(z, ROPE_PAIRS, 1)
    return z * c + up * s_up + dn * s_dn


def _inproj_kernel(*refs, kinds, rope):
    if rope:
        x_ref, mod_ref, g_ref, w_ref, c_ref, su_ref, sd_ref, o_ref = refs
    else:
        x_ref, mod_ref, g_ref, w_ref, o_ref = refs
    y = _rms(x_ref[...]) * g_ref[...]
    h = (y * (1.0 + mod_ref[1:2, :]) + mod_ref[0:1, :]).astype(BF16)
    for ci, kind in enumerate(kinds):
        lo = ci * IN_CHUNK
        z = jnp.dot(h, w_ref[:, lo:lo + IN_CHUNK], preferred_element_type=F32)
        if rope and kind in ("q", "k"):
            c, su, sd = c_ref[...], su_ref[...], sd_ref[...]
            z = jnp.concatenate(
                [_rope(z[:, j:j + LANES], c, su, sd) for j in range(0, IN_CHUNK, LANES)], axis=1)
        if kind == "q":
            z = z * (Q_SCALE * LOG2E)
        o_ref[:, lo:lo + IN_CHUNK] = z.astype(BF16)


def _inproj(x, mod, mod_row, g, w, kinds, tables, t):
    b, l, d = x.shape
    wout = IN_CHUNK * len(kinds)
    rope = tables is not None
    in_specs = [
        pl.BlockSpec((None, t, d), lambda bi, i: (bi, i, 0)),
        pl.BlockSpec((None, 6, d), lambda bi, i: (mod_row(bi), 0, 0)),
        pl.BlockSpec((1, d), lambda bi, i: (0, 0)),
        _resident((d, wout)),
    ]
    args = [x, mod, g, w]
    if rope:
        in_specs += [pl.BlockSpec((t, LANES), lambda bi, i: (i, 0))] * 3
        args += list(tables)
    return pl.pallas_call(
        functools.partial(_inproj_kernel, kinds=kinds, rope=rope),
        grid=(b, l // t),
        in_specs=in_specs,
        out_specs=pl.BlockSpec((None, t, wout), lambda bi, i: (bi, i, 0)),
        out_shape=jax.ShapeDtypeStruct((b, l, wout), BF16),
        compiler_params=_params("parallel", "parallel"),
        name="inproj",
    )(*args)


def _pool_kernel(u_ref, pw_ref, ps_ref, o_ref):
    l = u_ref.shape[0]
    t = lax.broadcasted_iota(jnp.int32, (l, POOL_GROUP_DIM), 0)

    def shifted(a, k):
        if k > 0:
            return jnp.where(t >= k, pltpu.roll(a, k, 0), 0.0)
        return jnp.where(t < l + k, pltpu.roll(a, l + k, 0), 0.0)

    for gi, w in enumerate(POOL_WINDOWS):
        lo = gi * POOL_GROUP_DIM
        u = u_ref[:, lo:lo + POOL_GROUP_DIM].astype(F32)
        back, fwd, span = u, u, 1
        while span < w // 2:
            back = back + shifted(back, span)
            fwd = fwd + shifted(fwd, -span)
            span *= 2
        win = shifted(back, 1) + fwd
        cnt = (jnp.minimum(t + w // 2, l) - jnp.maximum(t - w // 2, 0)).astype(F32)
        m = (win / cnt - u).astype(BF16)
        y = jnp.dot(m, pw_ref[gi], preferred_element_type=F32)
        o_ref[:, lo:lo + POOL_GROUP_DIM] = (y * ps_ref[:, lo:lo + POOL_GROUP_DIM]).astype(BF16)


def _pool(z, col_block, pool_w, pool_scale):
    b, l, _ = z.shape
    return pl.pallas_call(
        _pool_kernel,
        grid=(b,),
        in_specs=[
            pl.BlockSpec((None, l, POOL_WIDTH), lambda bi: (bi, 0, col_block)),
            _resident(pool_w.shape),
            pl.BlockSpec((1, POOL_WIDTH), lambda bi: (0, 0)),
        ],
        out_specs=pl.BlockSpec((None, l, POOL_WIDTH), lambda bi: (bi, 0, 0)),
        out_shape=jax.ShapeDtypeStruct((b, l, POOL_WIDTH), BF16),
        compiler_params=_params("parallel"),
        name="pool",
    )(z, pool_w, pool_scale)


def _attn_kernel(*refs, n_ctx, n_lat, tq, lam0):
    if n_lat:
        lam_ref, g_ref, q_ref, kc_ref, vc_ref, kl_ref, vl_ref, o_ref, k_sc, vt_sc = refs
    else:
        lam_ref, g_ref, q_ref, kc_ref, vc_ref, o_ref, k_sc, vt_sc = refs
    nk = n_ctx + n_lat
    k_sc[0:n_ctx, :] = kc_ref[...]
    vt_sc[0:V_DIM, 0:n_ctx] = vc_ref[...].astype(F32).T.astype(BF16)
    if n_lat:
        k_sc[n_ctx:nk, :] = kl_ref[...]
        vt_sc[0:V_DIM, n_ctx:nk] = vl_ref[...].astype(F32).T.astype(BF16)
    vt_sc[V_DIM:V_DIM + ONES_ROWS, :] = jnp.ones((ONES_ROWS, nk), BF16)

    lp = lam_ref[...]
    lam = (jnp.exp(jnp.sum(lp[0:1] * lp[1:2], axis=1, keepdims=True))
           - jnp.exp(jnp.sum(lp[2:3] * lp[3:4], axis=1, keepdims=True)) + lam0)
    k_all = k_sc[...]
    vt = vt_sc[...]
    lane = lax.broadcasted_iota(jnp.int32, (tq, V_DIM), 1)
    nq = q_ref.shape[0]

    half = lax.broadcasted_iota(jnp.int32, (8, V_DIM), 1) // HEAD_DIM
    sel = (half == lax.broadcasted_iota(jnp.int32, (8, V_DIM), 0)).astype(BF16)
    q_all = q_ref[...]
    nt = (((1,), (1,)), ((), ()))
    q_sq = jnp.max(lax.dot_general(sel, q_all * q_all, nt, preferred_element_type=F32), axis=1, keepdims=True)
    k_sq = jnp.max(lax.dot_general(sel, k_all * k_all, nt, preferred_element_type=F32), axis=1, keepdims=True)
    small = jnp.max(q_sq * k_sq) * SQ_NORM_SLACK < MAX_UNSHIFTED_LOG2 ** 2

    def attend(shift):
        for c0 in range(0, nq, tq):
            q = q_ref[c0:c0 + tq, :]
            outs = []
            for first in (True, False):
                qm = jnp.where((lane < HEAD_DIM) if first else (lane >= HEAD_DIM), q, jnp.zeros_like(q))
                s_t = lax.dot_general(k_all, qm, nt, preferred_element_type=F32)
                if shift:
                    s_t = s_t - jnp.max(s_t, axis=0, keepdims=True)
                p_t = jnp.exp2(s_t).astype(BF16)
                oe = jnp.dot(vt, p_t, preferred_element_type=F32)
                outs.append(oe[0:V_DIM] * (1.0 / oe[V_DIM:V_DIM + 1]))
            o_t = outs[0] - lam * outs[1]
            o_t = o_t * lax.rsqrt(jnp.mean(o_t * o_t, axis=0, keepdims=True) + EPS)
            o_t = o_t * (g_ref[...] * (1.0 - lam0))
            o_ref[c0:c0 + tq, :] = o_t.T.astype(BF16)

    lax.cond(small, lambda: attend(False), lambda: attend(True))


def _attention(lam_p, g_col, zq, q_blk, zc, kc_blk, vc_blk, zl, kl_blk, vl_blk, lam0, tq):
    b, nq, _ = zq.shape
    n_ctx = zc.shape[1]
    n_lat = 0 if zl is None else zl.shape[1]
    nk = n_ctx + n_lat

    def col(blk, n):
        return pl.BlockSpec((None, n, V_DIM), lambda bi, h: (bi, 0, blk + h))

    in_specs = [
        pl.BlockSpec(lam_p.shape, lambda bi, h: (0, 0)),
        pl.BlockSpec(g_col.shape, lambda bi, h: (0, 0)),
        col(q_blk, nq), col(kc_blk, n_ctx), col(vc_blk, n_ctx),
    ]
    args = [lam_p, g_col, zq, zc, zc]
    if n_lat:
        in_specs += [col(kl_blk, n_lat), col(vl_blk, n_lat)]
        args += [zl, zl]
    return pl.pallas_call(
        functools.partial(_attn_kernel, n_ctx=n_ctx, n_lat=n_lat, tq=min(tq, nq), lam0=lam0),
        grid=(b, N_HEADS),
        in_specs=in_specs,
        out_specs=pl.BlockSpec((None, nq, V_DIM), lambda bi, h: (bi, 0, h)),
        out_shape=jax.ShapeDtypeStruct((b, nq, QKV_WIDTH), BF16),
        scratch_shapes=[pltpu.VMEM((nk, V_DIM), BF16), pltpu.VMEM((V_DIM + ONES_ROWS, nk), BF16)],
        compiler_params=_params("parallel", "parallel"),
        name="attention",
    )(*args)


def _merge_kernel(*refs, router):
    if router:
        (x_ref, mod_ref, g2_ref, zg_ref, py_ref, ay_ref, wp_ref, wa_ref, wo_ref, rh_ref, rl_ref,
         xo_ref, h2_ref, gate_ref) = refs
    else:
        x_ref, mod_ref, g2_ref, zg_ref, py_ref, ay_ref, wp_ref, wa_ref, wo_ref, xo_ref, h2_ref = refs
    d = x_ref.shape[1]
    g_pool = _sigmoid(zg_ref[:, 0:d].astype(F32))
    g_attn = _sigmoid(zg_ref[:, d:2 * d].astype(F32))
    y = (g_pool * jnp.dot(py_ref[...], wp_ref[...], preferred_element_type=F32)
         + g_attn * jnp.dot(ay_ref[...], wa_ref[...], preferred_element_type=F32))
    o = jnp.dot(y.astype(BF16), wo_ref[...], preferred_element_type=F32)
    xn = x_ref[...] + mod_ref[2:3, :] * o
    xo_ref[...] = xn
    h2 = _rms(xn) * g2_ref[...] * (1.0 + mod_ref[4:5, :]) + mod_ref[3:4, :]
    h2_hi = h2.astype(BF16)
    h2_ref[...] = h2_hi
    if router:
        h2_lo = (h2 - h2_hi.astype(F32)).astype(BF16)
        logits = (jnp.dot(h2_hi, rh_ref[...], preferred_element_type=F32)
                  + jnp.dot(h2_hi, rl_ref[...], preferred_element_type=F32)
                  + jnp.dot(h2_lo, rh_ref[...], preferred_element_type=F32))
        lane = lax.broadcasted_iota(jnp.int32, logits.shape, 1).astype(F32)
        neg = jnp.float32(-jnp.inf)
        logits = jnp.where(lane < N_EXPERTS, logits, neg)
        v1 = jnp.max(logits, axis=1, keepdims=True)
        i1 = jnp.min(jnp.where(logits == v1, lane, float(LANES)), axis=1, keepdims=True)
        rest = jnp.where(lane == i1, neg, logits)
        v2 = jnp.max(rest, axis=1, keepdims=True)
        i2 = jnp.min(jnp.where(rest == v2, lane, float(LANES)), axis=1, keepdims=True)
        e2 = jnp.exp(v2 - v1)
        w1 = 1.0 / (1.0 + e2)
        gate_ref[...] = jnp.where(lane == i1, w1, 0.0) + jnp.where(lane == i2, e2 * w1, 0.0)


def _merge(x, mod, mod_row, g2, z, pool_y, attn_y, wp, wa, wo, router_w, t):
    b, l, d = x.shape
    router = router_w is not None
    tile = lambda w: pl.BlockSpec((None, t, w), lambda bi, i: (bi, i, 0))
    in_specs = [
        tile(d),
        pl.BlockSpec((None, 6, d), lambda bi, i: (mod_row(bi), 0, 0)),
        pl.BlockSpec((1, d), lambda bi, i: (0, 0)),
        tile(2 * d),
        tile(POOL_WIDTH), tile(QKV_WIDTH),
        _resident(wp.shape), _resident(wa.shape), _resident(wo.shape),
    ]
    args = [x, mod, g2, z, pool_y, attn_y, wp, wa, wo]
    out_specs = [tile(d), tile(d)]
    out_shape = [jax.ShapeDtypeStruct((b, l, d), F32), jax.ShapeDtypeStruct((b, l, d), BF16)]
    if router:
        in_specs += [_resident(router_w[0].shape)] * 2
        args += list(router_w)
        out_specs.append(tile(LANES))
        out_shape.append(jax.ShapeDtypeStruct((b, l, LANES), F32))
    return pl.pallas_call(
        functools.partial(_merge_kernel, router=router),
        grid=(b, l // t),
        in_specs=in_specs,
        out_specs=out_specs,
        out_shape=out_shape,
        compiler_params=_params("parallel", "parallel"),
        name="merge",
    )(*args)


def _swiglu(h, w1, w3, w2):
    a = jnp.dot(h, w1, preferred_element_type=F32)
    b = jnp.dot(h, w3, preferred_element_type=F32)
    return jnp.dot((a * _sigmoid(a) * b).astype(BF16), w2, preferred_element_type=F32)


def _ffn_kernel(*refs, final):
    if final:
        x_ref, h2_ref, mod_ref, w1_ref, w3_ref, w2_ref, fg_ref, o_ref = refs
    else:
        x_ref, h2_ref, mod_ref, w1_ref, w3_ref, w2_ref, o_ref = refs
    y = _swiglu(h2_ref[...], w1_ref[...], w3_ref[...], w2_ref[...])
    xn = x_ref[...] + mod_ref[5:6, :] * y
    if final:
        xn = _rms(xn) * fg_ref[...]
    o_ref[...] = xn


def _ffn(x, h2, mod, mod_row, w1, w3, w2, final_g, t):
    b, l, d = x.shape
    final = final_g is not None
    tile = pl.BlockSpec((None, t, d), lambda bi, i: (bi, i, 0))
    in_specs = [tile, tile, pl.BlockSpec((None, 6, d), lambda bi, i: (mod_row(bi), 0, 0)),
                _resident(w1.shape), _resident(w3.shape), _resident(w2.shape)]
    args = [x, h2, mod, w1, w3, w2]
    if final:
        in_specs.append(pl.BlockSpec((1, d), lambda bi, i: (0, 0)))
        args.append(final_g)
    return pl.pallas_call(
        functools.partial(_ffn_kernel, final=final),
        grid=(b, l // t),
        in_specs=in_specs,
        out_specs=tile,
        out_shape=jax.ShapeDtypeStruct((b, l, d), F32),
        compiler_params=_params("parallel", "parallel"),
        name="ffn",
    )(*args)


def _moe_kernel(*refs, final):
    if final:
        x_ref, h2_ref, gate_ref, mod_ref, w1_ref, w3_ref, w2_ref, fg_ref, o_ref, acc_ref = refs
    else:
        x_ref, h2_ref, gate_ref, mod_ref, w1_ref, w3_ref, w2_ref, o_ref, acc_ref = refs
    e = pl.program_id(2)

    @pl.when(e == 0)
    def _():
        acc_ref[...] = jnp.zeros_like(acc_ref)

    y = _swiglu(h2_ref[...], w1_ref[...], w3_ref[...], w2_ref[...])
    gate = gate_ref[...]
    lane = lax.broadcasted_iota(jnp.int32, gate.shape, 1)
    ge = jnp.sum(jnp.where(lane == e, gate, 0.0), axis=1, keepdims=True)
    acc_ref[...] += ge * y

    @pl.when(e == N_EXPERTS - 1)
    def _():
        xn = x_ref[...] + mod_ref[5:6, :] * acc_ref[...]
        if final:
            xn = _rms(xn) * fg_ref[...]
        o_ref[...] = xn


def _moe(x, h2, gate, mod, mod_row, w1, w3, w2, final_g, t):
    b, l, d = x.shape
    final = final_g is not None
    f = w1.shape[2]
    tile = lambda w: pl.BlockSpec((None, t, w), lambda bi, i, e: (bi, i, 0))
    in_specs = [tile(d), tile(d), tile(LANES),
                pl.BlockSpec((None, 6, d), lambda bi, i, e: (mod_row(bi), 0, 0)),
                pl.BlockSpec((None, d, f), lambda bi, i, e: (e, 0, 0)),
                pl.BlockSpec((None, d, f), lambda bi, i, e: (e, 0, 0)),
                pl.BlockSpec((None, f, d), lambda bi, i, e: (e, 0, 0))]
    args = [x, h2, gate, mod, w1, w3, w2]
    if final:
        in_specs.append(pl.BlockSpec((1, d), lambda bi, i, e: (0, 0)))
        args.append(final_g)
    return pl.pallas_call(
        functools.partial(_moe_kernel, final=final),
        grid=(b, l // t, N_EXPERTS),
        in_specs=in_specs,
        out_specs=tile(d),
        out_shape=jax.ShapeDtypeStruct((b, l, d), F32),
        scratch_shapes=[pltpu.VMEM((t, d), F32)],
        compiler_params=_params("parallel", "parallel", "arbitrary"),
        name="moe",
    )(*args)


def _rope_tables(l):
    rows = l // GRID_W
    row = jnp.repeat(jnp.arange(rows, dtype=F32), GRID_W)
    colp = jnp.tile(jnp.arange(GRID_W, dtype=F32), rows)
    inv = ROPE_THETA ** (-jnp.arange(ROPE_PAIRS, dtype=F32) * 2.0 / AXIS_DIM)
    ang_r, ang_c = row[:, None] * inv, colp[:, None] * inv
    zero = jnp.zeros_like(ang_r)
    cos64 = jnp.concatenate([jnp.cos(ang_r)] * 2 + [jnp.cos(ang_c)] * 2, axis=1)
    up64 = jnp.concatenate([-jnp.sin(ang_r), zero, -jnp.sin(ang_c), zero], axis=1)
    dn64 = jnp.concatenate([zero, jnp.sin(ang_r), zero, jnp.sin(ang_c)], axis=1)
    return tuple(jnp.tile(a, (1, LANES // HEAD_DIM)) for a in (cos64, up64, dn64))


def _lambda_init(layer):
    return 0.8 - 0.6 * math.exp(-0.3 * layer)


def _permute_in(w):
    off_q = POOL_WIDTH
    off_g = off_q + 3 * QKV_WIDTH
    return jnp.concatenate([w[:, off_g:], w[:, off_q:off_g], w[:, :off_q]], axis=1).astype(BF16)


def kernel(x, c, ctx, c_ctx, w_mod, b_mod, norm1_g, norm2_g, w_in, pool_w, pool_scale, lam_q1, lam_k1,
           lam_q2, lam_k2, subln_g, w_pool_proj, w_attn_proj, w_out, ffn_w1, ffn_w3, ffn_w2, router_w,
           moe_w1, moe_w3, moe_w2, final_g):
    b, l, d = x.shape
    n_ctx = ctx.shape[1]
    depth = w_mod.shape[0]
    assert d == D_MODEL and b + 1 <= MOD_ROWS and l % 512 == 0 and n_ctx % 256 == 0

    s_in = jnp.concatenate([c, c_ctx[None, :], jnp.zeros((MOD_ROWS - b - 1, d), F32)], axis=0)
    mod_all = _modulation(s_in, w_mod, b_mod)
    tables = _rope_tables(l)
    lat_row = lambda bi: bi
    ctx_row = lambda bi: b
    gq, gk, gv, gp = COL_Q // V_DIM, COL_K // V_DIM, COL_V // V_DIM, COL_P // POOL_WIDTH
    full_kinds = ("g",) * 4 + ("q",) * 2 + ("k",) * 2 + ("v",) * 2 + ("p",)

    xc = ctx
    for layer in range(depth):
        last = layer == depth - 1
        lam0 = _lambda_init(layer)
        mod = mod_all[layer].reshape(MOD_ROWS, 6, d)
        g1 = norm1_g[layer][None, :]
        g2 = norm2_g[layer][None, :]
        w_in_p = _permute_in(w_in[layer])
        lam_p = jnp.stack([lam_q1[layer], lam_k1[layer], lam_q2[layer], lam_k2[layer]])
        g_col = subln_g[layer][:, None]
        pw = pool_w[layer].astype(BF16)
        ps = pool_scale[layer][None, :]
        wp = w_pool_proj[layer].astype(BF16)
        wa = w_attn_proj[layer].astype(BF16)
        wo = w_out[layer].astype(BF16)
        fg = final_g[None, :] if last else None
        moe_layer = layer % 2 == 1
        if moe_layer:
            rw = jnp.pad(router_w[layer // 2], ((0, 0), (0, LANES - N_EXPERTS)))
            rw_hi = rw.astype(BF16)
            rw_lo = (rw - rw_hi.astype(F32)).astype(BF16)
            routers = (rw_hi, rw_lo)
            we1 = moe_w1[layer // 2].astype(BF16)
            we3 = moe_w3[layer // 2].astype(BF16)
            we2 = moe_w2[layer // 2].astype(BF16)
        else:
            routers = None
            wf1 = ffn_w1[layer // 2].astype(BF16)
            wf3 = ffn_w3[layer // 2].astype(BF16)
            wf2 = ffn_w2[layer // 2].astype(BF16)

        def mixer(xs, h2, gate, row, t):
            if moe_layer:
                return _moe(xs, h2, gate, mod, row, we1, we3, we2, fg, t)
            return _ffn(xs, h2, mod, row, wf1, wf3, wf2, fg, min(t, 512))

        z = _inproj(x, mod, lat_row, g1, w_in_p, full_kinds, tables, 512)
        if last:
            zc = _inproj(xc, mod, ctx_row, g1, w_in_p[:, COL_K:COL_P], ("k",) * 2 + ("v",) * 2, None, n_ctx)
            ckb, cvb = 0, QKV_WIDTH // V_DIM
        else:
            zc = _inproj(xc, mod, ctx_row, g1, w_in_p, full_kinds, None, n_ctx)
            ckb, cvb = gk, gv
        attn_y = _attention(lam_p, g_col, z, gq, zc, ckb, cvb, z, gk, gv, lam0, 512)
        pool_y = _pool(z, gp, pw, ps)
        res = _merge(x, mod, lat_row, g2, z, pool_y, attn_y, wp, wa, wo, routers, 512)
        x, h2 = res[0], res[1]
        gate = res[2] if moe_layer else None

        if not last:
            attn_yc = _attention(lam_p, g_col, zc, gq, zc, gk, gv, None, 0, 0, lam0, 512)
            pool_yc = _pool(zc, gp, pw, ps)
            resc = _merge(xc, mod, ctx_row, g2, zc, pool_yc, attn_yc, wp, wa, wo, routers, n_ctx)
            xc, h2c = resc[0], resc[1]
            gatec = resc[2] if moe_layer else None

        x = mixer(x, h2, gate, lat_row, min(1024, l))
        if not last:
            xc = mixer(xc, h2c, gatec, ctx_row, n_ctx)
    return x
```

```python
import functools
import math

import jax
import jax.numpy as jnp
from jax import lax
from jax.experimental import pallas as pl
from jax.experimental.pallas import tpu as pltpu

F32 = jnp.float32
BF16 = jnp.bfloat16

D_MODEL = 1024
EPS = 1e-6
GRID_W = 64
N_HEADS = 8
HEAD_DIM = 64
V_DIM = 2 * HEAD_DIM
ROPE_THETA = 10000.0
AXIS_DIM = HEAD_DIM // 2
ROPE_PAIRS = AXIS_DIM // 2
POOL_WINDOWS = (2, 4, 8, 16)
POOL_WIDTH = 512
POOL_GROUP_DIM = POOL_WIDTH // len(POOL_WINDOWS)
N_EXPERTS = 8
Q_SCALE = HEAD_DIM ** -0.5
LOG2E = math.log2(math.e)
MAX_UNSHIFTED_LOG2 = 80.0
SQ_NORM_SLACK = 1.05

QKV_WIDTH = N_HEADS * V_DIM
COL_G = 0
COL_Q = COL_G + 2 * D_MODEL
COL_K = COL_Q + QKV_WIDTH
COL_V = COL_K + QKV_WIDTH
COL_P = COL_V + QKV_WIDTH
IN_WIDTH = COL_P + POOL_WIDTH
IN_CHUNK = 512

LANES = 128
ONES_ROWS = 16
MOD_ROWS = 40

VMEM_LIMIT = 56 * 1024 * 1024


def _resident(shape):
    nd = len(shape)
    return pl.BlockSpec(shape, lambda *_: (0,) * nd, pipeline_mode=pl.Buffered(1))


def _params(*sem):
    return pltpu.CompilerParams(dimension_semantics=sem, vmem_limit_bytes=VMEM_LIMIT)


def _sigmoid(v):
    return 1.0 / (1.0 + jnp.exp(-v))


def _rms(v):
    return v * lax.rsqrt(jnp.mean(v * v, axis=-1, keepdims=True) + EPS)


def _mod_kernel(s_ref, w_ref, b_ref, o_ref):
    s = s_ref[...]
    s = s * _sigmoid(s)
    w = w_ref[...]
    s_hi = s.astype(BF16)
    s_lo = (s - s_hi.astype(F32)).astype(BF16)
    w_hi = w.astype(BF16)
    w_lo = (w - w_hi.astype(F32)).astype(BF16)
    acc = jnp.dot(s_hi, w_hi, preferred_element_type=F32)
    acc += jnp.dot(s_hi, w_lo, preferred_element_type=F32)
    acc += jnp.dot(s_lo, w_hi, preferred_element_type=F32)
    o_ref[...] = acc + b_ref[...]


def _modulation(s_in, w_mod, b_mod):
    depth, d, n = w_mod.shape
    tn = 1024
    return pl.pallas_call(
        _mod_kernel,
        grid=(depth, n // tn),
        in_specs=[
            pl.BlockSpec((MOD_ROWS, d), lambda l, j: (0, 0)),
            pl.BlockSpec((None, d, tn), lambda l, j: (l, 0, j)),
            pl.BlockSpec((None, 1, tn), lambda l, j: (l, 0, j)),
        ],
        out_specs=pl.BlockSpec((None, MOD_ROWS, tn), lambda l, j: (l, 0, j)),
        out_shape=jax.ShapeDtypeStruct((depth, MOD_ROWS, n), F32),
        compiler_params=_params("parallel", "parallel"),
        name="modulation",
    )(s_in, w_mod, b_mod.reshape(depth, 1, n))


def _rope(z, c, s_up, s_dn):
    up = pltpu.roll(z, LANES - ROPE_PAIRS, 1)
    dn = pltpu.roll(z, ROPE_PAIRS, 1)
    return z * c + up * s_up + dn * s_dn


def _inproj_kernel(*refs, kinds, rope):
    if rope:
        x_ref, mod_ref, g_ref, w_ref, c_ref, su_ref, sd_ref, o_ref = refs
    else:
        x_ref, mod_ref, g_ref, w_ref, o_ref = refs
    y = _rms(x_ref[...]) * g_ref[...]
    h = (y * (1.0 + mod_ref[1:2, :]) + mod_ref[0:1, :]).astype(BF16)
    for ci, kind in enumerate(kinds):
        lo = ci * IN_CHUNK
        z = jnp.dot(h, w_ref[:, lo:lo + IN_CHUNK], preferred_element_type=F32)
        if rope and kind in ("q", "k"):
            c, su, sd = c_ref[...], su_ref[...], sd_ref[...]
            z = jnp.concatenate(
                [_rope(z[:, j:j + LANES], c, su, sd) for j in range(0, IN_CHUNK, LANES)], axis=1)
        if kind == "q":
            z = z * (Q_SCALE * LOG2E)
        o_ref[:, lo:lo + IN_CHUNK] = z.astype(BF16)


def _inproj(x, mod, mod_row, g, w, kinds, tables, t):
    b, l, d = x.shape
    wout = IN_CHUNK * len(kinds)
    rope = tables is not None
    in_specs = [
        pl.BlockSpec((None, t, d), lambda bi, i: (bi, i, 0)),
        pl.BlockSpec((None, 6, d), lambda bi, i: (mod_row(bi), 0, 0)),
        pl.BlockSpec((1, d), lambda bi, i: (0, 0)),
        _resident((d, wout)),
    ]
    args = [x, mod, g, w]
    if rope:
        in_specs += [pl.BlockSpec((t, LANES), lambda bi, i: (i, 0))] * 3
        args += list(tables)
    return pl.pallas_call(
        functools.partial(_inproj_kernel, kinds=kinds, rope=rope),
        grid=(b, l // t),
        in_specs=in_specs,
        out_specs=pl.BlockSpec((None, t, wout), lambda bi, i: (bi, i, 0)),
        out_shape=jax.ShapeDtypeStruct((b, l, wout), BF16),
        compiler_params=_params("parallel", "parallel"),
        name="inproj",
    )(*args)


def _pool_kernel(u_ref, pw_ref, ps_ref, o_ref):
    l = u_ref.shape[0]
    t = lax.broadcasted_iota(jnp.int32, (l, POOL_GROUP_DIM), 0)

    def shifted(a, k):
        if k > 0:
            return jnp.where(t >= k, pltpu.roll(a, k, 0), 0.0)
        return jnp.where(t < l + k, pltpu.roll(a, l + k, 0), 0.0)

    for gi, w in enumerate(POOL_WINDOWS):
        lo = gi * POOL_GROUP_DIM
        u = u_ref[:, lo:lo + POOL_GROUP_DIM].astype(F32)
        back, fwd, span = u, u, 1
        while span < w // 2:
            back = back + shifted(back, span)
            fwd = fwd + shifted(fwd, -span)
            span *= 2
        win = shifted(back, 1) + fwd
        cnt = (jnp.minimum(t + w // 2, l) - jnp.maximum(t - w // 2, 0)).astype(F32)
        m = (win / cnt - u).astype(BF16)
        y = jnp.dot(m, pw_ref[gi], preferred_element_type=F32)
        o_ref[:, lo:lo + POOL_GROUP_DIM] = (y * ps_ref[:, lo:lo + POOL_GROUP_DIM]).astype(BF16)


def _pool(z, col_block, pool_w, pool_scale):
    b, l, _ = z.shape
    return pl.pallas_call(
        _pool_kernel,
        grid=(b,),
        in_specs=[
            pl.BlockSpec((None, l, POOL_WIDTH), lambda bi: (bi, 0, col_block)),
            _resident(pool_w.shape),
            pl.BlockSpec((1, POOL_WIDTH), lambda bi: (0, 0)),
        ],
        out_specs=pl.BlockSpec((None, l, POOL_WIDTH), lambda bi: (bi, 0, 0)),
        out_shape=jax.ShapeDtypeStruct((b, l, POOL_WIDTH), BF16),
        compiler_params=_params("parallel"),
        name="pool",
    )(z, pool_w, pool_scale)


def _attn_kernel(*refs, n_ctx, n_lat, tq, lam0):
    if n_lat:
        lam_ref, g_ref, q_ref, kc_ref, vc_ref, kl_ref, vl_ref, o_ref, k_sc, vt_sc = refs
    else:
        lam_ref, g_ref, q_ref, kc_ref, vc_ref, o_ref, k_sc, vt_sc = refs
    nk = n_ctx + n_lat
    k_sc[0:n_ctx, :] = kc_ref[...]
    vt_sc[0:V_DIM, 0:n_ctx] = vc_ref[...].astype(F32).T.astype(BF16)
    if n_lat:
        k_sc[n_ctx:nk, :] = kl_ref[...]
        vt_sc[0:V_DIM, n_ctx:nk] = vl_ref[...].astype(F32).T.astype(BF16)
    vt_sc[V_DIM:V_DIM + ONES_ROWS, :] = jnp.ones((ONES_ROWS, nk), BF16)

    lp = lam_ref[...]
    lam = (jnp.exp(jnp.sum(lp[0:1] * lp[1:2], axis=1, keepdims=True))
           - jnp.exp(jnp.sum(lp[2:3] * lp[3:4], axis=1, keepdims=True)) + lam0)
    k_all = k_sc[...]
    vt = vt_sc[...]
    lane = lax.broadcasted_iota(jnp.int32, (tq, V_DIM), 1)
    nq = q_ref.shape[0]

    half = lax.broadcasted_iota(jnp.int32, (8, V_DIM), 1) // HEAD_DIM
    sel = (half == lax.broadcasted_iota(jnp.int32, (8, V_DIM), 0)).astype(BF16)
    q_all = q_ref[...]
    nt = (((1,), (1,)), ((), ()))
    q_sq = jnp.max(lax.dot_general(sel, q_all * q_all, nt, preferred_element_type=F32), axis=1, keepdims=True)
    k_sq = jnp.max(lax.dot_general(sel, k_all * k_all, nt, preferred_element_type=F32), axis=1, keepdims=True)
    small = jnp.max(q_sq * k_sq) * SQ_NORM_SLACK < MAX_UNSHIFTED_LOG2 ** 2

    def attend(shift):
        for c0 in range(0, nq, tq):
            q = q_ref[c0:c0 + tq, :]
            outs = []
            for first in (True, False):
                qm = jnp.where((lane < HEAD_DIM) if first else (lane >= HEAD_DIM), q, jnp.zeros_like(q))
                s_t = lax.dot_general(k_all, qm, nt, preferred_element_type=F32)
                if shift:
                    s_t = s_t - jnp.max(s_t, axis=0, keepdims=True)
                p_t = jnp.exp2(s_t).astype(BF16)
                oe = jnp.dot(vt, p_t, preferred_element_type=F32)
                outs.append(oe[0:V_DIM] * (1.0 / oe[V_DIM:V_DIM + 1]))
            o_t = outs[0] - lam * outs[1]
            o_t = o_t * lax.rsqrt(jnp.mean(o_t * o_t, axis=0, keepdims=True) + EPS)
            o_t = o_t * (g_ref[...] * (1.0 - lam0))
            o_ref[c0:c0 + tq, :] = o_t.T.astype(BF16)

    lax.cond(small, lambda: attend(False), lambda: attend(True))


def _attention(lam_p, g_col, zq, q_blk, zc, kc_blk, vc_blk, zl, kl_blk, vl_blk, lam0, tq):
    b, nq, _ = zq.shape
    n_ctx = zc.shape[1]
    n_lat = 0 if zl is None else zl.shape[1]
    nk = n_ctx + n_lat

    def col(blk, n):
        return pl.BlockSpec((None, n, V_DIM), lambda bi, h: (bi, 0, blk + h))

    in_specs = [
        pl.BlockSpec(lam_p.shape, lambda bi, h: (0, 0)),
        pl.BlockSpec(g_col.shape, lambda bi, h: (0, 0)),
        col(q_blk, nq), col(kc_blk, n_ctx), col(vc_blk, n_ctx),
    ]
    args = [lam_p, g_col, zq, zc, zc]
    if n_lat:
        in_specs += [col(kl_blk, n_lat), col(vl_blk, n_lat)]
        args += [zl, zl]
    return pl.pallas_call(
        functools.partial(_attn_kernel, n_ctx=n_ctx, n_lat=n_lat, tq=min(tq, nq), lam0=lam0),
        grid=(b, N_HEADS),
        in_specs=in_specs,
        out_specs=pl.BlockSpec((None, nq, V_DIM), lambda bi, h: (bi, 0, h)),
        out_shape=jax.ShapeDtypeStruct((b, nq, QKV_WIDTH), BF16),
        scratch_shapes=[pltpu.VMEM((nk, V_DIM), BF16), pltpu.VMEM((V_DIM + ONES_ROWS, nk), BF16)],
        compiler_params=_params("parallel", "parallel"),
        name="attention",
    )(*args)


def _merge_kernel(*refs, router):
    if router:
        (x_ref, mod_ref, g2_ref, zg_ref, py_ref, ay_ref, wp_ref, wa_ref, wo_ref, rh_ref, rl_ref,
         xo_ref, h2_ref, gate_ref) = refs
    else:
        x_ref, mod_ref, g2_ref, zg_ref, py_ref, ay_ref, wp_ref, wa_ref, wo_ref, xo_ref, h2_ref = refs
    d = x_ref.shape[1]
    g_pool = _sigmoid(zg_ref[:, 0:d].astype(F32))
    g_attn = _sigmoid(zg_ref[:, d:2 * d].astype(F32))
    y = (g_pool * jnp.dot(py_ref[...], wp_ref[...], preferred_element_type=F32)
         + g_attn * jnp.dot(ay_ref[...], wa_ref[...], preferred_element_type=F32))
    o = jnp.dot(y.astype(BF16), wo_ref[...], preferred_element_type=F32)
    xn = x_ref[...] + mod_ref[2:3, :] * o
    xo_ref[...] = xn
    h2 = _rms(xn) * g2_ref[...] * (1.0 + mod_ref[4:5, :]) + mod_ref[3:4, :]
    h2_hi = h2.astype(BF16)
    h2_ref[...] = h2_hi
    if router:
        h2_lo = (h2 - h2_hi.astype(F32)).astype(BF16)
        logits = (jnp.dot(h2_hi, rh_ref[...], preferred_element_type=F32)
                  + jnp.dot(h2_hi, rl_ref[...], preferred_element_type=F32)
                  + jnp.dot(h2_lo, rh_ref[...], preferred_element_type=F32))
        lane = lax.broadcasted_iota(jnp.int32, logits.shape, 1).astype(F32)
        neg = jnp.float32(-jnp.inf)
        logits = jnp.where(lane < N_EXPERTS, logits, neg)
        v1 = jnp.max(logits, axis=1, keepdims=True)
        i1 = jnp.min(jnp.where(logits == v1, lane, float(LANES)), axis=1, keepdims=True)
        rest = jnp.where(lane == i1, neg, logits)
        v2 = jnp.max(rest, axis=1, keepdims=True)
        i2 = jnp.min(jnp.where(rest == v2, lane, float(LANES)), axis=1, keepdims=True)
        e2 = jnp.exp(v2 - v1)
        w1 = 1.0 / (1.0 + e2)
        gate_ref[...] = jnp.where(lane == i1, w1, 0.0) + jnp.where(lane == i2, e2 * w1, 0.0)


def _merge(x, mod, mod_row, g2, z, pool_y, attn_y, wp, wa, wo, router_w, t):
    b, l, d = x.shape
    router = router_w is not None
    tile = lambda w: pl.BlockSpec((None, t, w), lambda bi, i: (bi, i, 0))
    in_specs = [
        tile(d),
        pl.BlockSpec((None, 6, d), lambda bi, i: (mod_row(bi), 0, 0)),
        pl.BlockSpec((1, d), lambda bi, i: (0, 0)),
        tile(2 * d),
        tile(POOL_WIDTH), tile(QKV_WIDTH),
        _resident(wp.shape), _resident(wa.shape), _resident(wo.shape),
    ]
    args = [x, mod, g2, z, pool_y, attn_y, wp, wa, wo]
    out_specs = [tile(d), tile(d)]
    out_shape = [jax.ShapeDtypeStruct((b, l, d), F32), jax.ShapeDtypeStruct((b, l, d), BF16)]
    if router:
        in_specs += [_resident(router_w[0].shape)] * 2
        args += list(router_w)
        out_specs.append(tile(LANES))
        out_shape.append(jax.ShapeDtypeStruct((b, l, LANES), F32))
    return pl.pallas_call(
        functools.partial(_merge_kernel, router=router),
        grid=(b, l // t),
        in_specs=in_specs,
        out_specs=out_specs,
        out_shape=out_shape,
        compiler_params=_params("parallel", "parallel"),
        name="merge",
    )(*args)


def _swiglu(h, w1, w3, w2):
    a = jnp.dot(h, w1, preferred_element_type=F32)
    b = jnp.dot(h, w3, preferred_element_type=F32)
    return jnp.dot((a * _sigmoid(a) * b).astype(BF16), w2, preferred_element_type=F32)


def _ffn_kernel(*refs, final):
    if final:
        x_ref, h2_ref, mod_ref, w1_ref, w3_ref, w2_ref, fg_ref, o_ref = refs
    else:
        x_ref, h2_ref, mod_ref, w1_ref, w3_ref, w2_ref, o_ref = refs
    y = _swiglu(h2_ref[...], w1_ref[...], w3_ref[...], w2_ref[...])
    xn = x_ref[...] + mod_ref[5:6, :] * y
    if final:
        xn = _rms(xn) * fg_ref[...]
    o_ref[...] = xn


def _ffn(x, h2, mod, mod_row, w1, w3, w2, final_g, t):
    b, l, d = x.shape
    final = final_g is not None
    tile = pl.BlockSpec((None, t, d), lambda bi, i: (bi, i, 0))
    in_specs = [tile, tile, pl.BlockSpec((None, 6, d), lambda bi, i: (mod_row(bi), 0, 0)),
                _resident(w1.shape), _resident(w3.shape), _resident(w2.shape)]
    args = [x, h2, mod, w1, w3, w2]
    if final:
        in_specs.append(pl.BlockSpec((1, d), lambda bi, i: (0, 0)))
        args.append(final_g)
    return pl.pallas_call(
        functools.partial(_ffn_kernel, final=final),
        grid=(b, l // t),
        in_specs=in_specs,
        out_specs=tile,
        out_shape=jax.ShapeDtypeStruct((b, l, d), F32),
        compiler_params=_params("parallel", "parallel"),
        name="ffn",
    )(*args)


def _moe_kernel(*refs, final):
    if final:
        x_ref, h2_ref, gate_ref, mod_ref, w1_ref, w3_ref, w2_ref, fg_ref, o_ref, acc_ref = refs
    else:
        x_ref, h2_ref, gate_ref, mod_ref, w1_ref, w3_ref, w2_ref, o_ref, acc_ref = refs
    e = pl.program_id(2)

    @pl.when(e == 0)
    def _():
        acc_ref[...] = jnp.zeros_like(acc_ref)

    y = _swiglu(h2_ref[...], w1_ref[...], w3_ref[...], w2_ref[...])
    gate = gate_ref[...]
    lane = lax.broadcasted_iota(jnp.int32, gate.shape, 1)
    ge = jnp.sum(jnp.where(lane == e, gate, 0.0), axis=1, keepdims=True)
    acc_ref[...] += ge * y

    @pl.when(e == N_EXPERTS - 1)
    def _():
        xn = x_ref[...] + mod_ref[5:6, :] * acc_ref[...]
        if final:
            xn = _rms(xn) * fg_ref[...]
        o_ref[...] = xn


def _moe(x, h2, gate, mod, mod_row, w1, w3, w2, final_g, t):
    b, l, d = x.shape
    final = final_g is not None
    f = w1.shape[2]
    tile = lambda w: pl.BlockSpec((None, t, w), lambda bi, i, e: (bi, i, 0))
    in_specs = [tile(d), tile(d), tile(LANES),
                pl.BlockSpec((None, 6, d), lambda bi, i, e: (mod_row(bi), 0, 0)),
                pl.BlockSpec((None, d, f), lambda bi, i, e: (e, 0, 0)),
                pl.BlockSpec((None, d, f), lambda bi, i, e: (e, 0, 0)),
                pl.BlockSpec((None, f, d), lambda bi, i, e: (e, 0, 0))]
    args = [x, h2, gate, mod, w1, w3, w2]
    if final:
        in_specs.append(pl.BlockSpec((1, d), lambda bi, i, e: (0, 0)))
        args.append(final_g)
    return pl.pallas_call(
        functools.partial(_moe_kernel, final=final),
        grid=(b, l // t, N_EXPERTS),
        in_specs=in_specs,
        out_specs=tile(d),
        out_shape=jax.ShapeDtypeStruct((b, l, d), F32),
        scratch_shapes=[pltpu.VMEM((t, d), F32)],
        compiler_params=_params("parallel", "parallel", "arbitrary"),
        name="moe",
    )(*args)


MOE_TG = 512
MOE_TM = 512
PIECE = 16
MOE_SLOTS = 2 * MOE_TG + N_EXPERTS * PIECE
ROUTER_ROWS = 16


def _plan_kernel(x_ref, mod_ref, g2_ref, rh_ref, rl_ref, prow_ref, pcol_ref, cnt_ref):
    tg = x_ref.shape[0]
    h2 = _rms(x_ref[...]) * g2_ref[...] * (1.0 + mod_ref[4:5, :]) + mod_ref[3:4, :]
    h2_hi = h2.astype(BF16)
    h2_lo = (h2 - h2_hi.astype(F32)).astype(BF16)
    nt = (((1,), (1,)), ((), ()))
    logits = (lax.dot_general(rh_ref[...], h2_hi, nt, preferred_element_type=F32)
              + lax.dot_general(rl_ref[...], h2_hi, nt, preferred_element_type=F32)
              + lax.dot_general(rh_ref[...], h2_lo, nt, preferred_element_type=F32))
    sub = lax.broadcasted_iota(jnp.int32, logits.shape, 0).astype(F32)
    neg = jnp.float32(-jnp.inf)
    logits = jnp.where(sub < N_EXPERTS, logits, neg)
    v1 = jnp.max(logits, axis=0, keepdims=True)
    i1 = jnp.min(jnp.where(logits == v1, sub, float(ROUTER_ROWS)), axis=0, keepdims=True)
    rest = jnp.where(sub == i1, neg, logits)
    v2 = jnp.max(rest, axis=0, keepdims=True)
    i2 = jnp.min(jnp.where(rest == v2, sub, float(ROUTER_ROWS)), axis=0, keepdims=True)
    e2 = jnp.exp(v2 - v1)
    w1 = 1.0 / (1.0 + e2)
    w2 = e2 * w1
    two = w2 != 0.0
    sel = ((sub == i1) | ((sub == i2) & two)).astype(F32)
    before = (lax.broadcasted_iota(jnp.int32, (tg, tg), 0)
              < lax.broadcasted_iota(jnp.int32, (tg, tg), 1)).astype(BF16)
    rank = jnp.dot(sel.astype(BF16), before, preferred_element_type=F32)
    cnt = jnp.sum(sel, axis=1, keepdims=True)
    cpad = jnp.ceil(cnt * (1.0 / PIECE)) * PIECE
    subc = sub[:, 0:1]
    lbase = jnp.zeros_like(cpad)
    for e in range(N_EXPERTS - 1):
        c_e = jnp.sum(jnp.where(subc == e, cpad, 0.0), axis=0, keepdims=True)
        lbase = lbase + jnp.where(subc > e, c_e, 0.0)
    slot = lbase + rank
    e_a = jnp.where(two, jnp.minimum(i1, i2), i1)
    e_b = jnp.where(two, jnp.maximum(i1, i2), i1)
    ls_a = jnp.sum(jnp.where(sub == e_a, slot, 0.0), axis=0, keepdims=True)
    ls_b = jnp.sum(jnp.where(sub == e_b, slot, 0.0), axis=0, keepdims=True)
    w_a = jnp.where(e_a == i1, w1, w2)
    w_b = jnp.where(two, jnp.where(e_b == i1, w1, w2), 0.0)
    rows = jnp.concatenate([ls_a, ls_b, w_a, w_b, jnp.zeros((4, tg), F32)], axis=0)
    prow_ref[...] = rows
    pcol_ref[...] = jnp.concatenate([rows, jnp.zeros((LANES - 8, tg), F32)], axis=0).T
    cnt_ref[...] = jnp.broadcast_to(cnt[0:N_EXPERTS], (N_EXPERTS, LANES))


def _plan(x2, mod, mod_row, g2, rh, rl, tg):
    n, d = x2.shape
    nt = n // tg
    return pl.pallas_call(
        _plan_kernel,
        grid=(nt,),
        in_specs=[
            pl.BlockSpec((tg, d), lambda i: (i, 0)),
            pl.BlockSpec((None, 6, d), lambda i: (mod_row(i), 0, 0)),
            pl.BlockSpec((1, d), lambda i: (0, 0)),
            _resident(rh.shape), _resident(rl.shape),
        ],
        out_specs=[
            pl.BlockSpec((None, 8, tg), lambda i: (i, 0, 0)),
            pl.BlockSpec((tg, LANES), lambda i: (i, 0)),
            pl.BlockSpec((None, N_EXPERTS, LANES), lambda i: (i, 0, 0)),
        ],
        out_shape=[
            jax.ShapeDtypeStruct((nt, 8, tg), F32),
            jax.ShapeDtypeStruct((n, LANES), F32),
            jax.ShapeDtypeStruct((nt, N_EXPERTS, LANES), F32),
        ],
        compiler_params=_params("parallel"),
        name="moe_plan",
    )(x2, mod, g2, rh, rl)


def _segment_copies(i, lb_ref, gb_ref, np_ref, local_buf, sorted_hbm, sem, to_sorted, action):
    for e in range(N_EXPERTS):
        k = i * N_EXPERTS + e

        def body(j, carry, k=k):
            lo = pl.multiple_of(lb_ref[k] + j * PIECE, PIECE)
            go = pl.multiple_of(gb_ref[k] + j * PIECE, PIECE)
            local = local_buf.at[pl.ds(lo, PIECE), :]
            remote = sorted_hbm.at[pl.ds(go, PIECE), :]
            copy = pltpu.make_async_copy(local, remote, sem) if to_sorted else pltpu.make_async_copy(remote, local, sem)
            getattr(copy, action)()
            return carry

        lax.fori_loop(0, np_ref[k], body, 0)


def _dispatch_kernel(lb_ref, gb_ref, np_ref, h2_ref, prow_ref, init_ref, out_ref, cbuf, sem):
    del init_ref
    i = pl.program_id(0)
    tg = h2_ref.shape[0]
    s = lax.broadcasted_iota(jnp.int32, (MOE_SLOTS, tg), 0).astype(F32)
    onehot = ((s == prow_ref[0:1, :]) | (s == prow_ref[1:2, :])).astype(F32).astype(BF16)
    cbuf[...] = jnp.dot(onehot, h2_ref[...], preferred_element_type=F32).astype(BF16)
    _segment_copies(i, lb_ref, gb_ref, np_ref, cbuf, out_ref, sem, True, "start")
    _segment_copies(i, lb_ref, gb_ref, np_ref, cbuf, out_ref, sem, True, "wait")


def _dispatch(lb, gb, npc, h2, prow, rows, tg):
    n, d = h2.shape
    init = jnp.zeros((rows, d), BF16)
    grid_spec = pltpu.PrefetchScalarGridSpec(
        num_scalar_prefetch=3,
        grid=(n // tg,),
        in_specs=[
            pl.BlockSpec((tg, d), lambda i, *_: (i, 0)),
            pl.BlockSpec((None, 8, tg), lambda i, *_: (i, 0, 0)),
            pl.BlockSpec(memory_space=pl.ANY),
        ],
        out_specs=pl.BlockSpec(memory_space=pl.ANY),
        scratch_shapes=[pltpu.VMEM((MOE_SLOTS, d), BF16), pltpu.SemaphoreType.DMA],
    )
    return pl.pallas_call(
        _dispatch_kernel,
        grid_spec=grid_spec,
        out_shape=jax.ShapeDtypeStruct((rows, d), BF16),
        input_output_aliases={5: 0},
        compiler_params=_params("arbitrary"),
        name="moe_dispatch",
    )(lb, gb, npc, h2, prow, init)


def _group_kernel(te_ref, nu_ref, x_ref, w1_ref, w3_ref, w2_ref, o_ref):
    del te_ref
    live = pl.program_id(0) < nu_ref[0]

    @pl.when(live)
    def _():
        o_ref[...] = _swiglu(x_ref[...], w1_ref[...], w3_ref[...], w2_ref[...]).astype(BF16)

    @pl.when(jnp.logical_not(live))
    def _():
        o_ref[...] = jnp.zeros_like(o_ref)


def _group(tile_expert, n_used, xs, w1, w3, w2, tm):
    rows, d = xs.shape
    f = w1.shape[2]
    used = lambda r, te, nu: (jnp.minimum(r, nu[0] - 1), 0)
    grid_spec = pltpu.PrefetchScalarGridSpec(
        num_scalar_prefetch=2,
        grid=(rows // tm,),
        in_specs=[
            pl.BlockSpec((tm, d), used),
            pl.BlockSpec((None, d, f), lambda r, te, nu: (te[r], 0, 0)),
            pl.BlockSpec((None, d, f), lambda r, te, nu: (te[r], 0, 0)),
            pl.BlockSpec((None, f, d), lambda r, te, nu: (te[r], 0, 0)),
        ],
        out_specs=pl.BlockSpec((tm, d), lambda r, te, nu: (r, 0)),
    )
    return pl.pallas_call(
        _group_kernel,
        grid_spec=grid_spec,
        out_shape=jax.ShapeDtypeStruct((rows, d), BF16),
        compiler_params=_params("arbitrary"),
        name="moe_group",
    )(tile_expert, n_used, xs, w1, w3, w2)


def _combine_kernel(*refs, final):
    if final:
        lb_ref, gb_ref, np_ref, x_ref, pcol_ref, mod_ref, fg_ref, y_ref, o_ref, ybuf, sem = refs
    else:
        lb_ref, gb_ref, np_ref, x_ref, pcol_ref, mod_ref, y_ref, o_ref, ybuf, sem = refs
    i = pl.program_id(0)
    tg = x_ref.shape[0]

    @pl.when(i == 0)
    def _():
        ybuf[...] = jnp.zeros_like(ybuf)

    _segment_copies(i, lb_ref, gb_ref, np_ref, ybuf, y_ref, sem, False, "start")
    s = lax.broadcasted_iota(jnp.int32, (tg, MOE_SLOTS), 1).astype(F32)
    pc = pcol_ref[...]
    scatter = (jnp.where(s == pc[:, 0:1], pc[:, 2:3], 0.0)
               + jnp.where(s == pc[:, 1:2], pc[:, 3:4], 0.0)).astype(BF16)
    _segment_copies(i, lb_ref, gb_ref, np_ref, ybuf, y_ref, sem, False, "wait")
    y = jnp.dot(scatter, ybuf[...], preferred_element_type=F32)
    xn = x_ref[...] + mod_ref[5:6, :] * y
    if final:
        xn = _rms(xn) * fg_ref[...]
    o_ref[...] = xn


def _combine(lb, gb, npc, x2, pcol, mod, mod_row, final_g, ys, tg):
    n, d = x2.shape
    final = final_g is not None
    in_specs = [
        pl.BlockSpec((tg, d), lambda i, *_: (i, 0)),
        pl.BlockSpec((tg, LANES), lambda i, *_: (i, 0)),
        pl.BlockSpec((None, 6, d), lambda i, *_: (mod_row(i), 0, 0)),
    ]
    args = [x2, pcol, mod]
    if final:
        in_specs.append(pl.BlockSpec((1, d), lambda i, *_: (0, 0)))
        args.append(final_g)
    in_specs.append(pl.BlockSpec(memory_space=pl.ANY))
    args.append(ys)
    grid_spec = pltpu.PrefetchScalarGridSpec(
        num_scalar_prefetch=3,
        grid=(n // tg,),
        in_specs=in_specs,
        out_specs=pl.BlockSpec((tg, d), lambda i, *_: (i, 0)),
        scratch_shapes=[pltpu.VMEM((MOE_SLOTS, d), BF16), pltpu.SemaphoreType.DMA],
    )
    return pl.pallas_call(
        functools.partial(_combine_kernel, final=final),
        grid_spec=grid_spec,
        out_shape=jax.ShapeDtypeStruct((n, d), F32),
        compiler_params=_params("arbitrary"),
        name="moe_combine",
    )(lb, gb, npc, *args)


def _routed_moe(x, mod, batch_row, g2, router_w, w1, w3, w2, h2, final_g):
    b, l, d = x.shape
    n = b * l
    tg, tm = MOE_TG, MOE_TM
    nt = n // tg
    x2, h22 = x.reshape(n, d), h2.reshape(n, d)
    mod_row = lambda i: batch_row((i * tg) // l)
    rw = jnp.pad(router_w.T, ((0, ROUTER_ROWS - N_EXPERTS), (0, 0)))
    rh = rw.astype(BF16)
    rl = (rw - rh.astype(F32)).astype(BF16)
    prow, pcol, cnt = _plan(x2, mod, mod_row, g2, rh, rl, tg)

    cnt = cnt[:, :, 0].astype(jnp.int32)
    cpad = (cnt + PIECE - 1) // PIECE * PIECE
    lbase = jnp.cumsum(cpad, axis=1) - cpad
    tot = (jnp.sum(cpad, axis=0) + tm - 1) // tm * tm
    ends = jnp.cumsum(tot)
    gbase = (ends - tot)[None, :] + jnp.cumsum(cpad, axis=0) - cpad
    rows = (2 * n + nt * N_EXPERTS * (PIECE - 1) + N_EXPERTS * (tm - 1) + tm - 1) // tm * tm
    n_used = (ends[-1] // tm).astype(jnp.int32)
    tile_start = jnp.arange(rows // tm, dtype=jnp.int32) * tm
    tile_expert = jnp.searchsorted(ends, jnp.minimum(tile_start, ends[-1] - 1), side="right").astype(jnp.int32)
    lb, gb, npc = (a.reshape(-1).astype(jnp.int32) for a in (lbase, gbase, cpad // PIECE))

    xs = _dispatch(lb, gb, npc, h22, prow, rows, tg)
    ys = _group(tile_expert, n_used.reshape(1), xs, w1, w3, w2, tm)
    out = _combine(lb, gb, npc, x2, pcol, mod, mod_row, final_g, ys, tg)
    return out.reshape(b, l, d)


def _rope_tables(l):
    rows = l // GRID_W
    row = jnp.repeat(jnp.arange(rows, dtype=F32), GRID_W)
    colp = jnp.tile(jnp.arange(GRID_W, dtype=F32), rows)
    inv = ROPE_THETA ** (-jnp.arange(ROPE_PAIRS, dtype=F32) * 2.0 / AXIS_DIM)
    ang_r, ang_c = row[:, None] * inv, colp[:, None] * inv
    zero = jnp.zeros_like(ang_r)
    cos64 = jnp.concatenate([jnp.cos(ang_r)] * 2 + [jnp.cos(ang_c)] * 2, axis=1)
    up64 = jnp.concatenate([-jnp.sin(ang_r), zero, -jnp.sin(ang_c), zero], axis=1)
    dn64 = jnp.concatenate([zero, jnp.sin(ang_r), zero, jnp.sin(ang_c)], axis=1)
    return tuple(jnp.tile(a, (1, LANES // HEAD_DIM)) for a in (cos64, up64, dn64))


def _lambda_init(layer):
    return 0.8 - 0.6 * math.exp(-0.3 * layer)


def _permute_in(w):
    off_q = POOL_WIDTH
    off_g = off_q + 3 * QKV_WIDTH
    return jnp.concatenate([w[:, off_g:], w[:, off_q:off_g], w[:, :off_q]], axis=1).astype(BF16)


def kernel(x, c, ctx, c_ctx, w_mod, b_mod, norm1_g, norm2_g, w_in, pool_w, pool_scale, lam_q1, lam_k1,
           lam_q2, lam_k2, subln_g, w_pool_proj, w_attn_proj, w_out, ffn_w1, ffn_w3, ffn_w2, router_w,
           moe_w1, moe_w3, moe_w2, final_g):
    b, l, d = x.shape
    n_ctx = ctx.shape[1]
    depth = w_mod.shape[0]
    assert d == D_MODEL and b + 1 <= MOD_ROWS and l % 512 == 0 and n_ctx % 256 == 0

    s_in = jnp.concatenate([c, c_ctx[None, :], jnp.zeros((MOD_ROWS - b - 1, d), F32)], axis=0)
    mod_all = _modulation(s_in, w_mod, b_mod)
    tables = _rope_tables(l)
    lat_row = lambda bi: bi
    ctx_row = lambda bi: b
    gq, gk, gv, gp = COL_Q // V_DIM, COL_K // V_DIM, COL_V // V_DIM, COL_P // POOL_WIDTH
    full_kinds = ("g",) * 4 + ("q",) * 2 + ("k",) * 2 + ("v",) * 2 + ("p",)

    xc = ctx
    for layer in range(depth):
        last = layer == depth - 1
        lam0 = _lambda_init(layer)
        mod = mod_all[layer].reshape(MOD_ROWS, 6, d)
        g1 = norm1_g[layer][None, :]
        g2 = norm2_g[layer][None, :]
        w_in_p = _permute_in(w_in[layer])
        lam_p = jnp.stack([lam_q1[layer], lam_k1[layer], lam_q2[layer], lam_k2[layer]])
        g_col = subln_g[layer][:, None]
        pw = pool_w[layer].astype(BF16)
        ps = pool_scale[layer][None, :]
        wp = w_pool_proj[layer].astype(BF16)
        wa = w_attn_proj[layer].astype(BF16)
        wo = w_out[layer].astype(BF16)
        fg = final_g[None, :] if last else None
        moe_layer = layer % 2 == 1
        if moe_layer:
            rw = jnp.pad(router_w[layer // 2], ((0, 0), (0, LANES - N_EXPERTS)))
            rw_hi = rw.astype(BF16)
            rw_lo = (rw - rw_hi.astype(F32)).astype(BF16)
            routers = (rw_hi, rw_lo)
            we1 = moe_w1[layer // 2].astype(BF16)
            we3 = moe_w3[layer // 2].astype(BF16)
            we2 = moe_w2[layer // 2].astype(BF16)
        else:
            routers = None
            wf1 = ffn_w1[layer // 2].astype(BF16)
            wf3 = ffn_w3[layer // 2].astype(BF16)
            wf2 = ffn_w2[layer // 2].astype(BF16)

        def mixer(xs, h2, gate, row, t):
            if moe_layer:
                return _moe(xs, h2, gate, mod, row, we1, we3, we2, fg, t)
            return _ffn(xs, h2, mod, row, wf1, wf3, wf2, fg, min(t, 512))

        z = _inproj(x, mod, lat_row, g1, w_in_p, full_kinds, tables, 512)
        if last:
            zc = _inproj(xc, mod, ctx_row, g1, w_in_p[:, COL_K:COL_P], ("k",) * 2 + ("v",) * 2, None, n_ctx)
            ckb, cvb = 0, QKV_WIDTH // V_DIM
        else:
            zc = _inproj(xc, mod, ctx_row, g1, w_in_p, full_kinds, None, n_ctx)
            ckb, cvb = gk, gv
        attn_y = _attention(lam_p, g_col, z, gq, zc, ckb, cvb, z, gk, gv, lam0, 512)
        pool_y = _pool(z, gp, pw, ps)
        x, h2 = _merge(x, mod, lat_row, g2, z, pool_y, attn_y, wp, wa, wo, None, 512)

        if not last:
            attn_yc = _attention(lam_p, g_col, zc, gq, zc, gk, gv, None, 0, 0, lam0, 512)
            pool_yc = _pool(zc, gp, pw, ps)
            resc = _merge(xc, mod, ctx_row, g2, zc, pool_yc, attn_yc, wp, wa, wo, routers, n_ctx)
            xc, h2c = resc[0], resc[1]
            gatec = resc[2] if moe_layer else None

        if moe_layer:
            x = _routed_moe(x, mod, lat_row, g2, router_w[layer // 2], we1, we3, we2, h2, fg)
        else:
            x = mixer(x, h2, None, lat_row, 512)
        if not last:
            xc = mixer(xc, h2c, gatec, ctx_row, n_ctx)
    return x
```

```python
import functools
import math

import jax
import jax.numpy as jnp
from jax import lax
from jax.experimental import pallas as pl
from jax.experimental.pallas import tpu as pltpu

F32 = jnp.float32
BF16 = jnp.bfloat16

D_MODEL = 1024
EPS = 1e-6
GRID_W = 64
N_HEADS = 8
HEAD_DIM = 64
V_DIM = 2 * HEAD_DIM
ROPE_THETA = 10000.0
AXIS_DIM = HEAD_DIM // 2
ROPE_PAIRS = AXIS_DIM // 2
POOL_WINDOWS = (2, 4, 8, 16)
POOL_WIDTH = 512
POOL_GROUP_DIM = POOL_WIDTH // len(POOL_WINDOWS)
N_EXPERTS = 8
Q_SCALE = HEAD_DIM ** -0.5
LOG2E = math.log2(math.e)
MAX_UNSHIFTED_LOG2 = 80.0
SQ_NORM_SLACK = 1.05

QKV_WIDTH = N_HEADS * V_DIM
COL_G = 0
COL_Q = COL_G + 2 * D_MODEL
COL_K = COL_Q + QKV_WIDTH
COL_V = COL_K + QKV_WIDTH
IN_WIDTH = COL_V + QKV_WIDTH + POOL_WIDTH
COL_P = COL_V
IN_CHUNK = 512

LANES = 128
ONES_ROWS = 16
VT_ROWS = V_DIM + ONES_ROWS
MOD_ROWS = 40

VMEM_LIMIT = 56 * 1024 * 1024


def _resident(shape):
    nd = len(shape)
    return pl.BlockSpec(shape, lambda *_: (0,) * nd, pipeline_mode=pl.Buffered(1))


def _params(*sem):
    return pltpu.CompilerParams(dimension_semantics=sem, vmem_limit_bytes=VMEM_LIMIT)


def _sigmoid(v):
    return 1.0 / (1.0 + jnp.exp(-v))


def _rms(v):
    return v * lax.rsqrt(jnp.mean(v * v, axis=-1, keepdims=True) + EPS)


def _mod_kernel(s_ref, w_ref, b_ref, o_ref):
    s = s_ref[...]
    s = s * _sigmoid(s)
    w = w_ref[...]
    s_hi = s.astype(BF16)
    s_lo = (s - s_hi.astype(F32)).astype(BF16)
    w_hi = w.astype(BF16)
    w_lo = (w - w_hi.astype(F32)).astype(BF16)
    acc = jnp.dot(s_hi, w_hi, preferred_element_type=F32)
    acc += jnp.dot(s_hi, w_lo, preferred_element_type=F32)
    acc += jnp.dot(s_lo, w_hi, preferred_element_type=F32)
    o_ref[...] = acc + b_ref[...]


def _modulation(s_in, w_mod, b_mod):
    depth, d, n = w_mod.shape
    tn = 1024
    return pl.pallas_call(
        _mod_kernel,
        grid=(depth, n // tn),
        in_specs=[
            pl.BlockSpec((MOD_ROWS, d), lambda l, j: (0, 0)),
            pl.BlockSpec((None, d, tn), lambda l, j: (l, 0, j)),
            pl.BlockSpec((None, 1, tn), lambda l, j: (l, 0, j)),
        ],
        out_specs=pl.BlockSpec((None, MOD_ROWS, tn), lambda l, j: (l, 0, j)),
        out_shape=jax.ShapeDtypeStruct((depth, MOD_ROWS, n), F32),
        compiler_params=_params("parallel", "parallel"),
        name="modulation",
    )(s_in, w_mod, b_mod.reshape(depth, 1, n))


def _rope(z, c, s_up, s_dn):
    up = pltpu.roll(z, LANES - ROPE_PAIRS, 1)
    dn = pltpu.roll(z, ROPE_PAIRS, 1)
    return z * c + up * s_up + dn * s_dn


def _inproj_kernel(*refs, kinds, rope):
    if rope:
        x_ref, mod_ref, g_ref, w_ref, c_ref, su_ref, sd_ref, z_ref, vt_ref = refs
    else:
        x_ref, mod_ref, g_ref, w_ref, z_ref, vt_ref = refs
    t = x_ref.shape[0]
    y = _rms(x_ref[...]) * g_ref[...]
    h = (y * (1.0 + mod_ref[1:2, :]) + mod_ref[0:1, :]).astype(BF16)
    z_col, head = 0, 0
    for ci, kind in enumerate(kinds):
        lo = ci * IN_CHUNK
        z = jnp.dot(h, w_ref[:, lo:lo + IN_CHUNK], preferred_element_type=F32)
        if kind == "v":
            for j in range(0, IN_CHUNK, V_DIM):
                r0 = head * VT_ROWS
                vt_ref[r0:r0 + V_DIM, :] = z[:, j:j + V_DIM].T.astype(BF16)
                vt_ref[r0 + V_DIM:r0 + VT_ROWS, :] = jnp.ones((ONES_ROWS, t), BF16)
                head += 1
            continue
        if rope and kind in ("q", "k"):
            c, su, sd = c_ref[...], su_ref[...], sd_ref[...]
            z = jnp.concatenate(
                [_rope(z[:, j:j + LANES], c, su, sd) for j in range(0, IN_CHUNK, LANES)], axis=1)
        if kind == "q":
            z = z * (Q_SCALE * LOG2E)
        z_ref[:, z_col:z_col + IN_CHUNK] = z.astype(BF16)
        z_col += IN_CHUNK


def _inproj(x, mod, mod_row, g, w, kinds, tables, t):
    b, l, d = x.shape
    n_v = sum(k == "v" for k in kinds)
    wz = IN_CHUNK * (len(kinds) - n_v)
    rows_vt = n_v * (IN_CHUNK // V_DIM) * VT_ROWS
    rope = tables is not None
    in_specs = [
        pl.BlockSpec((None, t, d), lambda bi, i: (bi, i, 0)),
        pl.BlockSpec((None, 6, d), lambda bi, i: (mod_row(bi), 0, 0)),
        pl.BlockSpec((1, d), lambda bi, i: (0, 0)),
        _resident((d, IN_CHUNK * len(kinds))),
    ]
    args = [x, mod, g, w]
    if rope:
        in_specs += [pl.BlockSpec((t, LANES), lambda bi, i: (i, 0))] * 3
        args += list(tables)
    return pl.pallas_call(
        functools.partial(_inproj_kernel, kinds=kinds, rope=rope),
        grid=(b, l // t),
        in_specs=in_specs,
        out_specs=[pl.BlockSpec((None, t, wz), lambda bi, i: (bi, i, 0)),
                   pl.BlockSpec((None, rows_vt, t), lambda bi, i: (bi, 0, i))],
        out_shape=[jax.ShapeDtypeStruct((b, l, wz), BF16), jax.ShapeDtypeStruct((b, rows_vt, l), BF16)],
        compiler_params=_params("parallel", "parallel"),
        name="inproj",
    )(*args)


def _pool_kernel(u_ref, pw_ref, ps_ref, o_ref):
    l = u_ref.shape[0]
    t = lax.broadcasted_iota(jnp.int32, (l, POOL_GROUP_DIM), 0)

    def shifted(a, k):
        if k > 0:
            return jnp.where(t >= k, pltpu.roll(a, k, 0), 0.0)
        return jnp.where(t < l + k, pltpu.roll(a, l + k, 0), 0.0)

    for gi, w in enumerate(POOL_WINDOWS):
        lo = gi * POOL_GROUP_DIM
        u = u_ref[:, lo:lo + POOL_GROUP_DIM].astype(F32)
        back, fwd, span = u, u, 1
        while span < w // 2:
            back = back + shifted(back, span)
            fwd = fwd + shifted(fwd, -span)
            span *= 2
        win = shifted(back, 1) + fwd
        cnt = (jnp.minimum(t + w // 2, l) - jnp.maximum(t - w // 2, 0)).astype(F32)
        m = (win / cnt - u).astype(BF16)
        y = jnp.dot(m, pw_ref[gi], preferred_element_type=F32)
        o_ref[:, lo:lo + POOL_GROUP_DIM] = (y * ps_ref[:, lo:lo + POOL_GROUP_DIM]).astype(BF16)


def _pool(z, col_block, pool_w, pool_scale):
    b, l, _ = z.shape
    return pl.pallas_call(
        _pool_kernel,
        grid=(b,),
        in_specs=[
            pl.BlockSpec((None, l, POOL_WIDTH), lambda bi: (bi, 0, col_block)),
            _resident(pool_w.shape),
            pl.BlockSpec((1, POOL_WIDTH), lambda bi: (0, 0)),
        ],
        out_specs=pl.BlockSpec((None, l, POOL_WIDTH), lambda bi: (bi, 0, 0)),
        out_shape=jax.ShapeDtypeStruct((b, l, POOL_WIDTH), BF16),
        compiler_params=_params("parallel"),
        name="pool",
    )(z, pool_w, pool_scale)


def _attn_kernel(*refs, n_lat, tq, lam0, heads):
    if n_lat:
        lam_ref, g_ref, q_ref, kc_ref, vtc_ref, kl_ref, vtl_ref, o_ref = refs
    else:
        lam_ref, g_ref, q_ref, kc_ref, vtc_ref, o_ref = refs
    lp = lam_ref[...]
    lam = (jnp.exp(jnp.sum(lp[0:1] * lp[1:2], axis=1, keepdims=True))
           - jnp.exp(jnp.sum(lp[2:3] * lp[3:4], axis=1, keepdims=True)) + lam0)
    nq = q_ref.shape[0]
    lane = lax.broadcasted_iota(jnp.int32, (tq, V_DIM), 1)
    half = lax.broadcasted_iota(jnp.int32, (8, V_DIM), 1) // HEAD_DIM
    sel = (half == lax.broadcasted_iota(jnp.int32, (8, V_DIM), 0)).astype(BF16)
    nt = (((1,), (1,)), ((), ()))

    def sq_norm_max(a):
        return jnp.max(lax.dot_general(sel, a * a, nt, preferred_element_type=F32), axis=1, keepdims=True)

    def head_groups(hh):
        cs = slice(hh * V_DIM, (hh + 1) * V_DIM)
        rs = slice(hh * VT_ROWS, (hh + 1) * VT_ROWS)
        groups = [(kc_ref[:, cs], vtc_ref[rs, :])]
        if n_lat:
            groups.append((kl_ref[:, cs], vtl_ref[rs, :]))
        return cs, groups

    bound = None
    for hh in range(heads):
        cs, groups = head_groups(hh)
        k_sq = sq_norm_max(groups[0][0])
        for k, _ in groups[1:]:
            k_sq = jnp.maximum(k_sq, sq_norm_max(k))
        b_h = sq_norm_max(q_ref[:, cs]) * k_sq
        bound = b_h if bound is None else jnp.maximum(bound, b_h)
    small = jnp.max(bound) * SQ_NORM_SLACK < MAX_UNSHIFTED_LOG2 ** 2

    def attend(shift):
        for hh in range(heads):
            cs, groups = head_groups(hh)
            for c0 in range(0, nq, tq):
                q = q_ref[c0:c0 + tq, cs]
                outs = []
                for first in (True, False):
                    qm = jnp.where((lane < HEAD_DIM) if first else (lane >= HEAD_DIM), q, jnp.zeros_like(q))
                    s_t = [lax.dot_general(k, qm, nt, preferred_element_type=F32) for k, _ in groups]
                    if shift:
                        m = jnp.max(s_t[0], axis=0, keepdims=True)
                        for s_g in s_t[1:]:
                            m = jnp.maximum(m, jnp.max(s_g, axis=0, keepdims=True))
                        s_t = [s_g - m for s_g in s_t]
                    oe = None
                    for s_g, (_, vt) in zip(s_t, groups):
                        part = jnp.dot(vt, jnp.exp2(s_g).astype(BF16), preferred_element_type=F32)
                        oe = part if oe is None else oe + part
                    outs.append(oe[0:V_DIM] * (1.0 / oe[V_DIM:V_DIM + 1]))
                o_t = outs[0] - lam * outs[1]
                o_t = o_t * lax.rsqrt(jnp.mean(o_t * o_t, axis=0, keepdims=True) + EPS)
                o_t = o_t * (g_ref[...] * (1.0 - lam0))
                o_ref[c0:c0 + tq, cs] = o_t.T.astype(BF16)

    lax.cond(small, lambda: attend(False), lambda: attend(True))


def _attention(lam_p, g_col, zq, q_blk, zc, kc_blk, vtc, zl, kl_blk, vtl, lam0, tq, heads):
    b, nq, _ = zq.shape
    n_ctx = zc.shape[1]
    n_lat = 0 if zl is None else zl.shape[1]
    w = heads * V_DIM

    def col(blk, n):
        return pl.BlockSpec((None, n, w), lambda bi, h: (bi, 0, blk + h))

    def vt_rows(n):
        return pl.BlockSpec((None, heads * VT_ROWS, n), lambda bi, h: (bi, h, 0))

    in_specs = [
        pl.BlockSpec(lam_p.shape, lambda bi, h: (0, 0)),
        pl.BlockSpec(g_col.shape, lambda bi, h: (0, 0)),
        col(q_blk, nq), col(kc_blk, n_ctx), vt_rows(n_ctx),
    ]
    args = [lam_p, g_col, zq, zc, vtc]
    if n_lat:
        in_specs += [col(kl_blk, n_lat), vt_rows(n_lat)]
        args += [zl, vtl]
    return pl.pallas_call(
        functools.partial(_attn_kernel, n_lat=n_lat, tq=min(tq, nq), lam0=lam0, heads=heads),
        grid=(b, N_HEADS // heads),
        in_specs=in_specs,
        out_specs=pl.BlockSpec((None, nq, w), lambda bi, h: (bi, 0, h)),
        out_shape=jax.ShapeDtypeStruct((b, nq, QKV_WIDTH), BF16),
        compiler_params=_params("parallel", "parallel"),
        name="attention",
    )(*args)


def _merge_kernel(*refs, router):
    if router:
        (x_ref, mod_ref, g2_ref, zg_ref, py_ref, ay_ref, wp_ref, wa_ref, wo_ref, rh_ref, rl_ref,
         xo_ref, h2_ref, gate_ref) = refs
    else:
        x_ref, mod_ref, g2_ref, zg_ref, py_ref, ay_ref, wp_ref, wa_ref, wo_ref, xo_ref, h2_ref = refs
    d = x_ref.shape[1]
    g_pool = _sigmoid(zg_ref[:, 0:d].astype(F32))
    g_attn = _sigmoid(zg_ref[:, d:2 * d].astype(F32))
    y = (g_pool * jnp.dot(py_ref[...], wp_ref[...], preferred_element_type=F32)
         + g_attn * jnp.dot(ay_ref[...], wa_ref[...], preferred_element_type=F32))
    o = jnp.dot(y.astype(BF16), wo_ref[...], preferred_element_type=F32)
    xn = x_ref[...] + mod_ref[2:3, :] * o
    xo_ref[...] = xn
    h2 = _rms(xn) * g2_ref[...] * (1.0 + mod_ref[4:5, :]) + mod_ref[3:4, :]
    h2_hi = h2.astype(BF16)
    h2_ref[...] = h2_hi
    if router:
        h2_lo = (h2 - h2_hi.astype(F32)).astype(BF16)
        logits = (jnp.dot(h2_hi, rh_ref[...], preferred_element_type=F32)
                  + jnp.dot(h2_hi, rl_ref[...], preferred_element_type=F32)
                  + jnp.dot(h2_lo, rh_ref[...], preferred_element_type=F32))
        lane = lax.broadcasted_iota(jnp.int32, logits.shape, 1).astype(F32)
        neg = jnp.float32(-jnp.inf)
        logits = jnp.where(lane < N_EXPERTS, logits, neg)
        v1 = jnp.max(logits, axis=1, keepdims=True)
        i1 = jnp.min(jnp.where(logits == v1, lane, float(LANES)), axis=1, keepdims=True)
        rest = jnp.where(lane == i1, neg, logits)
        v2 = jnp.max(rest, axis=1, keepdims=True)
        i2 = jnp.min(jnp.where(rest == v2, lane, float(LANES)), axis=1, keepdims=True)
        e2 = jnp.exp(v2 - v1)
        w1 = 1.0 / (1.0 + e2)
        gate_ref[...] = jnp.where(lane == i1, w1, 0.0) + jnp.where(lane == i2, e2 * w1, 0.0)


def _merge(x, mod, mod_row, g2, z, pool_y, attn_y, wp, wa, wo, router_w, t):
    b, l, d = x.shape
    router = router_w is not None
    tile = lambda w: pl.BlockSpec((None, t, w), lambda bi, i: (bi, i, 0))
    in_specs = [
        tile(d),
        pl.BlockSpec((None, 6, d), lambda bi, i: (mod_row(bi), 0, 0)),
        pl.BlockSpec((1, d), lambda bi, i: (0, 0)),
        tile(2 * d),
        tile(POOL_WIDTH), tile(QKV_WIDTH),
        _resident(wp.shape), _resident(wa.shape), _resident(wo.shape),
    ]
    args = [x, mod, g2, z, pool_y, attn_y, wp, wa, wo]
    out_specs = [tile(d), tile(d)]
    out_shape = [jax.ShapeDtypeStruct((b, l, d), F32), jax.ShapeDtypeStruct((b, l, d), BF16)]
    if router:
        in_specs += [_resident(router_w[0].shape)] * 2
        args += list(router_w)
        out_specs.append(tile(LANES))
        out_shape.append(jax.ShapeDtypeStruct((b, l, LANES), F32))
    return pl.pallas_call(
        functools.partial(_merge_kernel, router=router),
        grid=(b, l // t),
        in_specs=in_specs,
        out_specs=out_specs,
        out_shape=out_shape,
        compiler_params=_params("parallel", "parallel"),
        name="merge",
    )(*args)


def _swiglu(h, w1, w3, w2):
    a = jnp.dot(h, w1, preferred_element_type=F32)
    b = jnp.dot(h, w3, preferred_element_type=F32)
    return jnp.dot((a * _sigmoid(a) * b).astype(BF16), w2, preferred_element_type=F32)


def _ffn_kernel(*refs, final):
    if final:
        x_ref, h2_ref, mod_ref, w1_ref, w3_ref, w2_ref, fg_ref, o_ref = refs
    else:
        x_ref, h2_ref, mod_ref, w1_ref, w3_ref, w2_ref, o_ref = refs
    y = _swiglu(h2_ref[...], w1_ref[...], w3_ref[...], w2_ref[...])
    xn = x_ref[...] + mod_ref[5:6, :] * y
    if final:
        xn = _rms(xn) * fg_ref[...]
    o_ref[...] = xn


def _ffn(x, h2, mod, mod_row, w1, w3, w2, final_g, t):
    b, l, d = x.shape
    final = final_g is not None
    tile = pl.BlockSpec((None, t, d), lambda bi, i: (bi, i, 0))
    in_specs = [tile, tile, pl.BlockSpec((None, 6, d), lambda bi, i: (mod_row(bi), 0, 0)),
                _resident(w1.shape), _resident(w3.shape), _resident(w2.shape)]
    args = [x, h2, mod, w1, w3, w2]
    if final:
        in_specs.append(pl.BlockSpec((1, d), lambda bi, i: (0, 0)))
        args.append(final_g)
    return pl.pallas_call(
        functools.partial(_ffn_kernel, final=final),
        grid=(b, l // t),
        in_specs=in_specs,
        out_specs=tile,
        out_shape=jax.ShapeDtypeStruct((b, l, d), F32),
        compiler_params=_params("parallel", "parallel"),
        name="ffn",
    )(*args)


def _moe_kernel(*refs, final):
    if final:
        x_ref, h2_ref, gate_ref, mod_ref, w1_ref, w3_ref, w2_ref, fg_ref, o_ref, acc_ref = refs
    else:
        x_ref, h2_ref, gate_ref, mod_ref, w1_ref, w3_ref, w2_ref, o_ref, acc_ref = refs
    e = pl.program_id(2)

    @pl.when(e == 0)
    def _():
        acc_ref[...] = jnp.zeros_like(acc_ref)

    y = _swiglu(h2_ref[...], w1_ref[...], w3_ref[...], w2_ref[...])
    gate = gate_ref[...]
    lane = lax.broadcasted_iota(jnp.int32, gate.shape, 1)
    ge = jnp.sum(jnp.where(lane == e, gate, 0.0), axis=1, keepdims=True)
    acc_ref[...] += ge * y

    @pl.when(e == N_EXPERTS - 1)
    def _():
        xn = x_ref[...] + mod_ref[5:6, :] * acc_ref[...]
        if final:
            xn = _rms(xn) * fg_ref[...]
        o_ref[...] = xn


def _moe(x, h2, gate, mod, mod_row, w1, w3, w2, final_g, t):
    b, l, d = x.shape
    final = final_g is not None
    f = w1.shape[2]
    tile = lambda w: pl.BlockSpec((None, t, w), lambda bi, i, e: (bi, i, 0))
    in_specs = [tile(d), tile(d), tile(LANES),
                pl.BlockSpec((None, 6, d), lambda bi, i, e: (mod_row(bi), 0, 0)),
                pl.BlockSpec((None, d, f), lambda bi, i, e: (e, 0, 0)),
                pl.BlockSpec((None, d, f), lambda bi, i, e: (e, 0, 0)),
                pl.BlockSpec((None, f, d), lambda bi, i, e: (e, 0, 0))]
    args = [x, h2, gate, mod, w1, w3, w2]
    if final:
        in_specs.append(pl.BlockSpec((1, d), lambda bi, i, e: (0, 0)))
        args.append(final_g)
    return pl.pallas_call(
        functools.partial(_moe_kernel, final=final),
        grid=(b, l // t, N_EXPERTS),
        in_specs=in_specs,
        out_specs=tile(d),
        out_shape=jax.ShapeDtypeStruct((b, l, d), F32),
        scratch_shapes=[pltpu.VMEM((t, d), F32)],
        compiler_params=_params("parallel", "parallel", "arbitrary"),
        name="moe",
    )(*args)


MOE_TG = 512
MOE_TM = 512
PIECE = 16
BIG_PIECE = 64
MOE_SLOTS = 2 * MOE_TG + N_EXPERTS * PIECE
ROUTER_ROWS = 16


def _plan_kernel(x_ref, mod_ref, g2_ref, rh_ref, rl_ref, prow_ref, pcol_ref, cnt_ref):
    tg = x_ref.shape[0]
    h2 = _rms(x_ref[...]) * g2_ref[...] * (1.0 + mod_ref[4:5, :]) + mod_ref[3:4, :]
    h2_hi = h2.astype(BF16)
    h2_lo = (h2 - h2_hi.astype(F32)).astype(BF16)
    nt = (((1,), (1,)), ((), ()))
    logits = (lax.dot_general(rh_ref[...], h2_hi, nt, preferred_element_type=F32)
              + lax.dot_general(rl_ref[...], h2_hi, nt, preferred_element_type=F32)
              + lax.dot_general(rh_ref[...], h2_lo, nt, preferred_element_type=F32))
    sub = lax.broadcasted_iota(jnp.int32, logits.shape, 0).astype(F32)
    neg = jnp.float32(-jnp.inf)
    logits = jnp.where(sub < N_EXPERTS, logits, neg)
    v1 = jnp.max(logits, axis=0, keepdims=True)
    i1 = jnp.min(jnp.where(logits == v1, sub, float(ROUTER_ROWS)), axis=0, keepdims=True)
    rest = jnp.where(sub == i1, neg, logits)
    v2 = jnp.max(rest, axis=0, keepdims=True)
    i2 = jnp.min(jnp.where(rest == v2, sub, float(ROUTER_ROWS)), axis=0, keepdims=True)
    e2 = jnp.exp(v2 - v1)
    w1 = 1.0 / (1.0 + e2)
    w2 = e2 * w1
    two = w2 != 0.0
    sel = ((sub == i1) | ((sub == i2) & two)).astype(F32)
    before = (lax.broadcasted_iota(jnp.int32, (tg, tg), 0)
              < lax.broadcasted_iota(jnp.int32, (tg, tg), 1)).astype(BF16)
    rank = jnp.dot(sel.astype(BF16), before, preferred_element_type=F32)
    cnt = jnp.sum(sel, axis=1, keepdims=True)
    cpad = jnp.ceil(cnt * (1.0 / PIECE)) * PIECE
    subc = sub[:, 0:1]
    lbase = jnp.zeros_like(cpad)
    for e in range(N_EXPERTS - 1):
        c_e = jnp.sum(jnp.where(subc == e, cpad, 0.0), axis=0, keepdims=True)
        lbase = lbase + jnp.where(subc > e, c_e, 0.0)
    slot = lbase + rank
    e_a = jnp.where(two, jnp.minimum(i1, i2), i1)
    e_b = jnp.where(two, jnp.maximum(i1, i2), i1)
    ls_a = jnp.sum(jnp.where(sub == e_a, slot, 0.0), axis=0, keepdims=True)
    ls_b = jnp.sum(jnp.where(sub == e_b, slot, 0.0), axis=0, keepdims=True)
    w_a = jnp.where(e_a == i1, w1, w2)
    w_b = jnp.where(two, jnp.where(e_b == i1, w1, w2), 0.0)
    rows = jnp.concatenate([ls_a, ls_b, w_a, w_b, jnp.zeros((4, tg), F32)], axis=0)
    prow_ref[...] = rows
    pcol_ref[...] = jnp.concatenate([rows, jnp.zeros((LANES - 8, tg), F32)], axis=0).T
    cnt_ref[...] = jnp.broadcast_to(cnt[0:N_EXPERTS], (N_EXPERTS, LANES))


def _plan(x2, mod, mod_row, g2, rh, rl, tg):
    n, d = x2.shape
    nt = n // tg
    return pl.pallas_call(
        _plan_kernel,
        grid=(nt,),
        in_specs=[
            pl.BlockSpec((tg, d), lambda i: (i, 0)),
            pl.BlockSpec((None, 6, d), lambda i: (mod_row(i), 0, 0)),
            pl.BlockSpec((1, d), lambda i: (0, 0)),
            _resident(rh.shape), _resident(rl.shape),
        ],
        out_specs=[
            pl.BlockSpec((None, 8, tg), lambda i: (i, 0, 0)),
            pl.BlockSpec((tg, LANES), lambda i: (i, 0)),
            pl.BlockSpec((None, N_EXPERTS, LANES), lambda i: (i, 0, 0)),
        ],
        out_shape=[
            jax.ShapeDtypeStruct((nt, 8, tg), F32),
            jax.ShapeDtypeStruct((n, LANES), F32),
            jax.ShapeDtypeStruct((nt, N_EXPERTS, LANES), F32),
        ],
        compiler_params=_params("parallel"),
        name="moe_plan",
    )(x2, mod, g2, rh, rl)


def _segment_copies(i, seg, local_buf, sorted_hbm, sem, to_sorted, action):
    lb_ref, gb_ref, nb_ref, ns_ref = seg
    for e in range(N_EXPERTS):
        k = i * N_EXPERTS + e

        def copy(off, rows, k=k):
            lo = pl.multiple_of(lb_ref[k] + off, PIECE)
            go = pl.multiple_of(gb_ref[k] + off, PIECE)
            local = local_buf.at[pl.ds(lo, rows), :]
            remote = sorted_hbm.at[pl.ds(go, rows), :]
            c = pltpu.make_async_copy(local, remote, sem) if to_sorted else pltpu.make_async_copy(remote, local, sem)
            getattr(c, action)()

        def big(j, carry, copy=copy):
            copy(j * BIG_PIECE, BIG_PIECE)
            return carry

        def small(j, carry, copy=copy, k=k):
            copy(nb_ref[k] * BIG_PIECE + j * PIECE, PIECE)
            return carry

        lax.fori_loop(0, nb_ref[k], big, 0)
        lax.fori_loop(0, ns_ref[k], small, 0)


def _dispatch_kernel(lb_ref, gb_ref, nb_ref, ns_ref, h2_ref, prow_ref, init_ref, out_ref, cbuf, sems):
    del init_ref
    seg = (lb_ref, gb_ref, nb_ref, ns_ref)
    i = pl.program_id(0)
    nt = pl.num_programs(0)
    slot = i % 2
    tg = h2_ref.shape[0]

    @pl.when(i >= 2)
    def _():
        _segment_copies(i - 2, seg, cbuf.at[slot], out_ref, sems.at[slot], True, "wait")

    s = lax.broadcasted_iota(jnp.int32, (MOE_SLOTS, tg), 0).astype(F32)
    onehot = ((s == prow_ref[0:1, :]) | (s == prow_ref[1:2, :])).astype(F32).astype(BF16)
    cbuf[slot] = jnp.dot(onehot, h2_ref[...], preferred_element_type=F32).astype(BF16)
    _segment_copies(i, seg, cbuf.at[slot], out_ref, sems.at[slot], True, "start")

    @pl.when(i == nt - 1)
    def _():
        @pl.when(i >= 1)
        def _():
            _segment_copies(i - 1, seg, cbuf.at[1 - slot], out_ref, sems.at[1 - slot], True, "wait")

        _segment_copies(i, seg, cbuf.at[slot], out_ref, sems.at[slot], True, "wait")


def _dispatch(seg, h2, prow, rows, tg):
    n, d = h2.shape
    init = jnp.zeros((rows, d), BF16)
    grid_spec = pltpu.PrefetchScalarGridSpec(
        num_scalar_prefetch=len(seg),
        grid=(n // tg,),
        in_specs=[
            pl.BlockSpec((tg, d), lambda i, *_: (i, 0)),
            pl.BlockSpec((None, 8, tg), lambda i, *_: (i, 0, 0)),
            pl.BlockSpec(memory_space=pl.ANY),
        ],
        out_specs=pl.BlockSpec(memory_space=pl.ANY),
        scratch_shapes=[pltpu.VMEM((2, MOE_SLOTS, d), BF16), pltpu.SemaphoreType.DMA((2,))],
    )
    return pl.pallas_call(
        _dispatch_kernel,
        grid_spec=grid_spec,
        out_shape=jax.ShapeDtypeStruct((rows, d), BF16),
        input_output_aliases={len(seg) + 2: 0},
        compiler_params=_params("arbitrary"),
        name="moe_dispatch",
    )(*seg, h2, prow, init)


def _group_kernel(te_ref, nu_ref, x_ref, w1_ref, w3_ref, w2_ref, o_ref):
    del te_ref
    live = pl.program_id(0) < nu_ref[0]

    @pl.when(live)
    def _():
        o_ref[...] = _swiglu(x_ref[...], w1_ref[...], w3_ref[...], w2_ref[...]).astype(BF16)

    @pl.when(jnp.logical_not(live))
    def _():
        o_ref[...] = jnp.zeros_like(o_ref)


def _group(tile_expert, n_used, xs, w1, w3, w2, tm):
    rows, d = xs.shape
    f = w1.shape[2]
    used = lambda r, te, nu: (jnp.minimum(r, nu[0] - 1), 0)
    grid_spec = pltpu.PrefetchScalarGridSpec(
        num_scalar_prefetch=2,
        grid=(rows // tm,),
        in_specs=[
            pl.BlockSpec((tm, d), used),
            pl.BlockSpec((None, d, f), lambda r, te, nu: (te[r], 0, 0)),
            pl.BlockSpec((None, d, f), lambda r, te, nu: (te[r], 0, 0)),
            pl.BlockSpec((None, f, d), lambda r, te, nu: (te[r], 0, 0)),
        ],
        out_specs=pl.BlockSpec((tm, d), lambda r, te, nu: (r, 0)),
    )
    return pl.pallas_call(
        _group_kernel,
        grid_spec=grid_spec,
        out_shape=jax.ShapeDtypeStruct((rows, d), BF16),
        compiler_params=_params("arbitrary"),
        name="moe_group",
    )(tile_expert, n_used, xs, w1, w3, w2)


def _combine_kernel(*refs, final):
    if final:
        lb_ref, gb_ref, nb_ref, ns_ref, x_ref, pcol_ref, mod_ref, fg_ref, y_ref, o_ref, ybuf, sems = refs
    else:
        lb_ref, gb_ref, nb_ref, ns_ref, x_ref, pcol_ref, mod_ref, y_ref, o_ref, ybuf, sems = refs
    seg = (lb_ref, gb_ref, nb_ref, ns_ref)
    i = pl.program_id(0)
    nt = pl.num_programs(0)
    slot = i % 2
    tg = x_ref.shape[0]

    @pl.when(i == 0)
    def _():
        ybuf[...] = jnp.zeros_like(ybuf)
        _segment_copies(i, seg, ybuf.at[slot], y_ref, sems.at[slot], False, "start")

    @pl.when(i + 1 < nt)
    def _():
        _segment_copies(i + 1, seg, ybuf.at[1 - slot], y_ref, sems.at[1 - slot], False, "start")

    s = lax.broadcasted_iota(jnp.int32, (tg, MOE_SLOTS), 1).astype(F32)
    pc = pcol_ref[...]
    scatter = (jnp.where(s == pc[:, 0:1], pc[:, 2:3], 0.0)
               + jnp.where(s == pc[:, 1:2], pc[:, 3:4], 0.0)).astype(BF16)
    _segment_copies(i, seg, ybuf.at[slot], y_ref, sems.at[slot], False, "wait")
    y = jnp.dot(scatter, ybuf[slot], preferred_element_type=F32)
    xn = x_ref[...] + mod_ref[5:6, :] * y
    if final:
        xn = _rms(xn) * fg_ref[...]
    o_ref[...] = xn


def _combine(seg, x2, pcol, mod, mod_row, final_g, ys, tg):
    n, d = x2.shape
    final = final_g is not None
    in_specs = [
        pl.BlockSpec((tg, d), lambda i, *_: (i, 0)),
        pl.BlockSpec((tg, LANES), lambda i, *_: (i, 0)),
        pl.BlockSpec((None, 6, d), lambda i, *_: (mod_row(i), 0, 0)),
    ]
    args = [x2, pcol, mod]
    if final:
        in_specs.append(pl.BlockSpec((1, d), lambda i, *_: (0, 0)))
        args.append(final_g)
    in_specs.append(pl.BlockSpec(memory_space=pl.ANY))
    args.append(ys)
    grid_spec = pltpu.PrefetchScalarGridSpec(
        num_scalar_prefetch=len(seg),
        grid=(n // tg,),
        in_specs=in_specs,
        out_specs=pl.BlockSpec((tg, d), lambda i, *_: (i, 0)),
        scratch_shapes=[pltpu.VMEM((2, MOE_SLOTS, d), BF16), pltpu.SemaphoreType.DMA((2,))],
    )
    return pl.pallas_call(
        functools.partial(_combine_kernel, final=final),
        grid_spec=grid_spec,
        out_shape=jax.ShapeDtypeStruct((n, d), F32),
        compiler_params=_params("arbitrary"),
        name="moe_combine",
    )(*seg, *args)


def _routed_moe(x, mod, batch_row, g2, router_w, w1, w3, w2, h2, final_g):
    b, l, d = x.shape
    n = b * l
    tg, tm = MOE_TG, MOE_TM
    nt = n // tg
    x2, h22 = x.reshape(n, d), h2.reshape(n, d)
    mod_row = lambda i: batch_row((i * tg) // l)
    rw = jnp.pad(router_w.T, ((0, ROUTER_ROWS - N_EXPERTS), (0, 0)))
    rh = rw.astype(BF16)
    rl = (rw - rh.astype(F32)).astype(BF16)
    prow, pcol, cnt = _plan(x2, mod, mod_row, g2, rh, rl, tg)

    cnt = cnt[:, :, 0].astype(jnp.int32)
    cpad = (cnt + PIECE - 1) // PIECE * PIECE
    lbase = jnp.cumsum(cpad, axis=1) - cpad
    tot = (jnp.sum(cpad, axis=0) + tm - 1) // tm * tm
    ends = jnp.cumsum(tot)
    gbase = (ends - tot)[None, :] + jnp.cumsum(cpad, axis=0) - cpad
    rows = (2 * n + nt * N_EXPERTS * (PIECE - 1) + N_EXPERTS * (tm - 1) + tm - 1) // tm * tm
    n_used = (ends[-1] // tm).astype(jnp.int32)
    tile_start = jnp.arange(rows // tm, dtype=jnp.int32) * tm
    tile_expert = jnp.searchsorted(ends, jnp.minimum(tile_start, ends[-1] - 1), side="right").astype(jnp.int32)
    seg = tuple(a.reshape(-1).astype(jnp.int32)
                for a in (lbase, gbase, cpad // BIG_PIECE, cpad % BIG_PIECE // PIECE))

    xs = _dispatch(seg, h22, prow, rows, tg)
    ys = _group(tile_expert, n_used.reshape(1), xs, w1, w3, w2, tm)
    out = _combine(seg, x2, pcol, mod, mod_row, final_g, ys, tg)
    return out.reshape(b, l, d)


def _rope_tables(l):
    rows = l // GRID_W
    row = jnp.repeat(jnp.arange(rows, dtype=F32), GRID_W)
    colp = jnp.tile(jnp.arange(GRID_W, dtype=F32), rows)
    inv = ROPE_THETA ** (-jnp.arange(ROPE_PAIRS, dtype=F32) * 2.0 / AXIS_DIM)
    ang_r, ang_c = row[:, None] * inv, colp[:, None] * inv
    zero = jnp.zeros_like(ang_r)
    cos64 = jnp.concatenate([jnp.cos(ang_r)] * 2 + [jnp.cos(ang_c)] * 2, axis=1)
    up64 = jnp.concatenate([-jnp.sin(ang_r), zero, -jnp.sin(ang_c), zero], axis=1)
    dn64 = jnp.concatenate([zero, jnp.sin(ang_r), zero, jnp.sin(ang_c)], axis=1)
    return tuple(jnp.tile(a, (1, LANES // HEAD_DIM)) for a in (cos64, up64, dn64))


def _lambda_init(layer):
    return 0.8 - 0.6 * math.exp(-0.3 * layer)


def _permute_in(w):
    off_q = POOL_WIDTH
    off_g = off_q + 3 * QKV_WIDTH
    return jnp.concatenate([w[:, off_g:], w[:, off_q:off_g], w[:, :off_q]], axis=1).astype(BF16)


def kernel(x, c, ctx, c_ctx, w_mod, b_mod, norm1_g, norm2_g, w_in, pool_w, pool_scale, lam_q1, lam_k1,
           lam_q2, lam_k2, subln_g, w_pool_proj, w_attn_proj, w_out, ffn_w1, ffn_w3, ffn_w2, router_w,
           moe_w1, moe_w3, moe_w2, final_g):
    b, l, d = x.shape
    n_ctx = ctx.shape[1]
    depth = w_mod.shape[0]
    assert d == D_MODEL and b + 1 <= MOD_ROWS and l % 512 == 0 and n_ctx % 256 == 0

    s_in = jnp.concatenate([c, c_ctx[None, :], jnp.zeros((MOD_ROWS - b - 1, d), F32)], axis=0)
    mod_all = _modulation(s_in, w_mod, b_mod)
    tables = _rope_tables(l)
    lat_row = lambda bi: bi
    ctx_row = lambda bi: b
    gq, gk, gp = COL_Q // V_DIM, COL_K // V_DIM, COL_P // POOL_WIDTH
    full_kinds = ("g",) * 4 + ("q",) * 2 + ("k",) * 2 + ("v",) * 2 + ("p",)

    xc = ctx
    for layer in range(depth):
        last = layer == depth - 1
        lam0 = _lambda_init(layer)
        mod = mod_all[layer].reshape(MOD_ROWS, 6, d)
        g1 = norm1_g[layer][None, :]
        g2 = norm2_g[layer][None, :]
        w_in_p = _permute_in(w_in[layer])
        lam_p = jnp.stack([lam_q1[layer], lam_k1[layer], lam_q2[layer], lam_k2[layer]])
        g_col = subln_g[layer][:, None]
        pw = pool_w[layer].astype(BF16)
        ps = pool_scale[layer][None, :]
        wp = w_pool_proj[layer].astype(BF16)
        wa = w_attn_proj[layer].astype(BF16)
        wo = w_out[layer].astype(BF16)
        fg = final_g[None, :] if last else None
        moe_layer = layer % 2 == 1
        if moe_layer:
            rw = jnp.pad(router_w[layer // 2], ((0, 0), (0, LANES - N_EXPERTS)))
            rw_hi = rw.astype(BF16)
            rw_lo = (rw - rw_hi.astype(F32)).astype(BF16)
            routers = (rw_hi, rw_lo)
            we1 = moe_w1[layer // 2].astype(BF16)
            we3 = moe_w3[layer // 2].astype(BF16)
            we2 = moe_w2[layer // 2].astype(BF16)
        else:
            routers = None
            wf1 = ffn_w1[layer // 2].astype(BF16)
            wf3 = ffn_w3[layer // 2].astype(BF16)
            wf2 = ffn_w2[layer // 2].astype(BF16)

        def mixer(xs, h2, gate, row, t):
            if moe_layer:
                return _moe(xs, h2, gate, mod, row, we1, we3, we2, fg, t)
            return _ffn(xs, h2, mod, row, wf1, wf3, wf2, fg, min(t, 512))

        z, vt = _inproj(x, mod, lat_row, g1, w_in_p, full_kinds, tables, 512)
        if last:
            zc, vtc = _inproj(xc, mod, ctx_row, g1, w_in_p[:, COL_K:COL_V + QKV_WIDTH], ("k",) * 2 + ("v",) * 2,
                              None, n_ctx)
            ckb = 0
        else:
            zc, vtc = _inproj(xc, mod, ctx_row, g1, w_in_p, full_kinds, None, n_ctx)
            ckb = gk
        attn_y = _attention(lam_p, g_col, z, gq, zc, ckb, vtc, z, gk, vt, lam0, 512, 1)
        pool_y = _pool(z, gp, pw, ps)
        x, h2 = _merge(x, mod, lat_row, g2, z, pool_y, attn_y, wp, wa, wo, None, 512)

        if not last:
            attn_yc = _attention(lam_p, g_col, zc, COL_Q // QKV_WIDTH, zc, COL_K // QKV_WIDTH, vtc, None, 0, None,
                                 lam0, 512, N_HEADS)
            pool_yc = _pool(zc, gp, pw, ps)
            resc = _merge(xc, mod, ctx_row, g2, zc, pool_yc, attn_yc, wp, wa, wo, routers, n_ctx)
            xc, h2c = resc[0], resc[1]
            gatec = resc[2] if moe_layer else None

        if moe_layer:
            x = _routed_moe(x, mod, lat_row, g2, router_w[layer // 2], we1, we3, we2, h2, fg)
        else:
            x = mixer(x, h2, None, lat_row, 512)
        if not last:
            xc = mixer(xc, h2c, gatec, ctx_row, n_ctx)
    return x
```

```python
import functools
import math

import jax
import jax.numpy as jnp
from jax import lax
from jax.experimental import pallas as pl
from jax.experimental.pallas import tpu as pltpu

F32 = jnp.float32
BF16 = jnp.bfloat16

D_MODEL = 1024
EPS = 1e-6
GRID_W = 64
N_HEADS = 8
HEAD_DIM = 64
V_DIM = 2 * HEAD_DIM
ROPE_THETA = 10000.0
AXIS_DIM = HEAD_DIM // 2
ROPE_PAIRS = AXIS_DIM // 2
POOL_WINDOWS = (2, 4, 8, 16)
POOL_WIDTH = 512
POOL_GROUP_DIM = POOL_WIDTH // len(POOL_WINDOWS)
N_EXPERTS = 8
Q_SCALE = HEAD_DIM ** -0.5
LOG2E = math.log2(math.e)
MAX_UNSHIFTED_LOG2 = 80.0
SQ_NORM_SLACK = 1.05

QKV_WIDTH = N_HEADS * V_DIM
COL_G = 0
COL_Q = COL_G + 2 * D_MODEL
COL_K = COL_Q + QKV_WIDTH
COL_V = COL_K + QKV_WIDTH
IN_WIDTH = COL_V + QKV_WIDTH + POOL_WIDTH
COL_P = COL_V
IN_CHUNK = 512

LANES = 128
MOD_ROWS = 40

VMEM_LIMIT = 56 * 1024 * 1024


def _resident(shape):
    nd = len(shape)
    return pl.BlockSpec(shape, lambda *_: (0,) * nd, pipeline_mode=pl.Buffered(1))


def _params(*sem):
    return pltpu.CompilerParams(dimension_semantics=sem, vmem_limit_bytes=VMEM_LIMIT)


def _sigmoid(v):
    return 1.0 / (1.0 + jnp.exp(-v))


def _rms(v):
    return v * lax.rsqrt(jnp.mean(v * v, axis=-1, keepdims=True) + EPS)


def _mod_kernel(s_ref, w_ref, b_ref, o_ref):
    s = s_ref[...]
    s = s * _sigmoid(s)
    w = w_ref[...]
    s_hi = s.astype(BF16)
    s_lo = (s - s_hi.astype(F32)).astype(BF16)
    w_hi = w.astype(BF16)
    w_lo = (w - w_hi.astype(F32)).astype(BF16)
    acc = jnp.dot(s_hi, w_hi, preferred_element_type=F32)
    acc += jnp.dot(s_hi, w_lo, preferred_element_type=F32)
    acc += jnp.dot(s_lo, w_hi, preferred_element_type=F32)
    o_ref[...] = acc + b_ref[...]


def _modulation(s_in, w_mod, b_mod):
    depth, d, n = w_mod.shape
    tn = 1024
    return pl.pallas_call(
        _mod_kernel,
        grid=(depth, n // tn),
        in_specs=[
            pl.BlockSpec((MOD_ROWS, d), lambda l, j: (0, 0)),
            pl.BlockSpec((None, d, tn), lambda l, j: (l, 0, j)),
            pl.BlockSpec((None, 1, tn), lambda l, j: (l, 0, j)),
        ],
        out_specs=pl.BlockSpec((None, MOD_ROWS, tn), lambda l, j: (l, 0, j)),
        out_shape=jax.ShapeDtypeStruct((depth, MOD_ROWS, n), F32),
        compiler_params=_params("parallel", "parallel"),
        name="modulation",
    )(s_in, w_mod, b_mod.reshape(depth, 1, n))


def _rope(z, c, s_up, s_dn):
    up = pltpu.roll(z, LANES - ROPE_PAIRS, 1)
    dn = pltpu.roll(z, ROPE_PAIRS, 1)
    return z * c + up * s_up + dn * s_dn


def _inproj_kernel(*refs, kinds, rope):
    if rope:
        x_ref, mod_ref, g_ref, w_ref, c_ref, su_ref, sd_ref, z_ref, vt_ref = refs
    else:
        x_ref, mod_ref, g_ref, w_ref, z_ref, vt_ref = refs
    y = _rms(x_ref[...]) * g_ref[...]
    h = (y * (1.0 + mod_ref[1:2, :]) + mod_ref[0:1, :]).astype(BF16)
    z_col, head = 0, 0
    for ci, kind in enumerate(kinds):
        lo = ci * IN_CHUNK
        z = jnp.dot(h, w_ref[:, lo:lo + IN_CHUNK], preferred_element_type=F32)
        if kind == "v":
            for j in range(0, IN_CHUNK, V_DIM):
                vt_ref[head * V_DIM:(head + 1) * V_DIM, :] = z[:, j:j + V_DIM].T.astype(BF16)
                head += 1
            continue
        if rope and kind in ("q", "k"):
            c, su, sd = c_ref[...], su_ref[...], sd_ref[...]
            z = jnp.concatenate(
                [_rope(z[:, j:j + LANES], c, su, sd) for j in range(0, IN_CHUNK, LANES)], axis=1)
        if kind == "q":
            z = z * (Q_SCALE * LOG2E)
        z_ref[:, z_col:z_col + IN_CHUNK] = z.astype(BF16)
        z_col += IN_CHUNK


def _inproj(x, mod, mod_row, g, w, kinds, tables, t):
    b, l, d = x.shape
    n_v = sum(k == "v" for k in kinds)
    wz = IN_CHUNK * (len(kinds) - n_v)
    rows_vt = n_v * IN_CHUNK
    rope = tables is not None
    in_specs = [
        pl.BlockSpec((None, t, d), lambda bi, i: (bi, i, 0)),
        pl.BlockSpec((None, 6, d), lambda bi, i: (mod_row(bi), 0, 0)),
        pl.BlockSpec((1, d), lambda bi, i: (0, 0)),
        _resident((d, IN_CHUNK * len(kinds))),
    ]
    args = [x, mod, g, w]
    if rope:
        in_specs += [pl.BlockSpec((t, LANES), lambda bi, i: (i, 0))] * 3
        args += list(tables)
    return pl.pallas_call(
        functools.partial(_inproj_kernel, kinds=kinds, rope=rope),
        grid=(b, l // t),
        in_specs=in_specs,
        out_specs=[pl.BlockSpec((None, t, wz), lambda bi, i: (bi, i, 0)),
                   pl.BlockSpec((None, rows_vt, t), lambda bi, i: (bi, 0, i))],
        out_shape=[jax.ShapeDtypeStruct((b, l, wz), BF16), jax.ShapeDtypeStruct((b, rows_vt, l), BF16)],
        compiler_params=_params("parallel", "parallel"),
        name="inproj",
    )(*args)


def _pool_kernel(u_ref, pw_ref, ps_ref, o_ref):
    l = u_ref.shape[0]
    t = lax.broadcasted_iota(jnp.int32, (l, POOL_GROUP_DIM), 0)

    def shifted(a, k):
        if k > 0:
            return jnp.where(t >= k, pltpu.roll(a, k, 0), 0.0)
        return jnp.where(t < l + k, pltpu.roll(a, l + k, 0), 0.0)

    for gi, w in enumerate(POOL_WINDOWS):
        lo = gi * POOL_GROUP_DIM
        u = u_ref[:, lo:lo + POOL_GROUP_DIM].astype(F32)
        back, fwd, span = u, u, 1
        while span < w // 2:
            back = back + shifted(back, span)
            fwd = fwd + shifted(fwd, -span)
            span *= 2
        win = shifted(back, 1) + fwd
        cnt = (jnp.minimum(t + w // 2, l) - jnp.maximum(t - w // 2, 0)).astype(F32)
        m = (win / cnt - u).astype(BF16)
        y = jnp.dot(m, pw_ref[gi], preferred_element_type=F32)
        o_ref[:, lo:lo + POOL_GROUP_DIM] = (y * ps_ref[:, lo:lo + POOL_GROUP_DIM]).astype(BF16)


def _pool(z, col_block, pool_w, pool_scale):
    b, l, _ = z.shape
    return pl.pallas_call(
        _pool_kernel,
        grid=(b,),
        in_specs=[
            pl.BlockSpec((None, l, POOL_WIDTH), lambda bi: (bi, 0, col_block)),
            _resident(pool_w.shape),
            pl.BlockSpec((1, POOL_WIDTH), lambda bi: (0, 0)),
        ],
        out_specs=pl.BlockSpec((None, l, POOL_WIDTH), lambda bi: (bi, 0, 0)),
        out_shape=jax.ShapeDtypeStruct((b, l, POOL_WIDTH), BF16),
        compiler_params=_params("parallel"),
        name="pool",
    )(z, pool_w, pool_scale)


def _attn_kernel(*refs, n_lat, tq, lam0, heads):
    if n_lat:
        lam_ref, g_ref, q_ref, kc_ref, vtc_ref, kl_ref, vtl_ref, o_ref = refs
    else:
        lam_ref, g_ref, q_ref, kc_ref, vtc_ref, o_ref = refs
    lp = lam_ref[...]
    lam = (jnp.exp(jnp.sum(lp[0:1] * lp[1:2], axis=1, keepdims=True))
           - jnp.exp(jnp.sum(lp[2:3] * lp[3:4], axis=1, keepdims=True)) + lam0)
    nq = q_ref.shape[0]
    lane = lax.broadcasted_iota(jnp.int32, (tq, V_DIM), 1)
    half = lax.broadcasted_iota(jnp.int32, (8, V_DIM), 1) // HEAD_DIM
    sel = (half == lax.broadcasted_iota(jnp.int32, (8, V_DIM), 0)).astype(BF16)
    nt = (((1,), (1,)), ((), ()))

    def sq_norm_max(a):
        return jnp.max(lax.dot_general(sel, a * a, nt, preferred_element_type=F32), axis=1, keepdims=True)

    def head_groups(hh):
        cs = slice(hh * V_DIM, (hh + 1) * V_DIM)
        groups = [(kc_ref[:, cs], vtc_ref[cs, :])]
        if n_lat:
            groups.append((kl_ref[:, cs], vtl_ref[cs, :]))
        return cs, groups

    bound = None
    for hh in range(heads):
        cs, groups = head_groups(hh)
        k_sq = sq_norm_max(groups[0][0])
        for k, _ in groups[1:]:
            k_sq = jnp.maximum(k_sq, sq_norm_max(k))
        b_h = sq_norm_max(q_ref[:, cs]) * k_sq
        bound = b_h if bound is None else jnp.maximum(bound, b_h)
    small = jnp.max(bound) * SQ_NORM_SLACK < MAX_UNSHIFTED_LOG2 ** 2

    def attend(shift):
        for hh in range(heads):
            cs, groups = head_groups(hh)
            for c0 in range(0, nq, tq):
                q = q_ref[c0:c0 + tq, cs]
                outs = []
                for first in (True, False):
                    qm = jnp.where((lane < HEAD_DIM) if first else (lane >= HEAD_DIM), q, jnp.zeros_like(q))
                    s_t = [lax.dot_general(k, qm, nt, preferred_element_type=F32) for k, _ in groups]
                    if shift:
                        m = jnp.max(s_t[0], axis=0, keepdims=True)
                        for s_g in s_t[1:]:
                            m = jnp.maximum(m, jnp.max(s_g, axis=0, keepdims=True))
                        s_t = [s_g - m for s_g in s_t]
                    o_m, l_m = None, None
                    for s_g, (_, vt) in zip(s_t, groups):
                        p_g = jnp.exp2(s_g)
                        part = jnp.dot(vt, p_g.astype(BF16), preferred_element_type=F32)
                        l_g = jnp.sum(p_g, axis=0, keepdims=True)
                        o_m, l_m = (part, l_g) if o_m is None else (o_m + part, l_m + l_g)
                    outs.append(o_m * (1.0 / l_m))
                o_t = outs[0] - lam * outs[1]
                o_t = o_t * lax.rsqrt(jnp.mean(o_t * o_t, axis=0, keepdims=True) + EPS)
                o_t = o_t * (g_ref[...] * (1.0 - lam0))
                o_ref[c0:c0 + tq, cs] = o_t.T.astype(BF16)

    lax.cond(small, lambda: attend(False), lambda: attend(True))


def _attention(lam_p, g_col, zq, q_blk, zc, kc_blk, vtc, zl, kl_blk, vtl, lam0, tq, heads):
    b, nq, _ = zq.shape
    n_ctx = zc.shape[1]
    n_lat = 0 if zl is None else zl.shape[1]
    w = heads * V_DIM

    def col(blk, n):
        return pl.BlockSpec((None, n, w), lambda bi, h: (bi, 0, blk + h))

    def vt_rows(n):
        return pl.BlockSpec((None, heads * V_DIM, n), lambda bi, h: (bi, h, 0))

    in_specs = [
        pl.BlockSpec(lam_p.shape, lambda bi, h: (0, 0)),
        pl.BlockSpec(g_col.shape, lambda bi, h: (0, 0)),
        col(q_blk, nq), col(kc_blk, n_ctx), vt_rows(n_ctx),
    ]
    args = [lam_p, g_col, zq, zc, vtc]
    if n_lat:
        in_specs += [col(kl_blk, n_lat), vt_rows(n_lat)]
        args += [zl, vtl]
    return pl.pallas_call(
        functools.partial(_attn_kernel, n_lat=n_lat, tq=min(tq, nq), lam0=lam0, heads=heads),
        grid=(b, N_HEADS // heads),
        in_specs=in_specs,
        out_specs=pl.BlockSpec((None, nq, w), lambda bi, h: (bi, 0, h)),
        out_shape=jax.ShapeDtypeStruct((b, nq, QKV_WIDTH), BF16),
        compiler_params=_params("parallel", "parallel"),
        name="attention",
    )(*args)


def _merge_kernel(*refs, router):
    if router:
        (x_ref, mod_ref, g2_ref, zg_ref, py_ref, ay_ref, wp_ref, wa_ref, wo_ref, rh_ref, rl_ref,
         xo_ref, h2_ref, gate_ref) = refs
    else:
        x_ref, mod_ref, g2_ref, zg_ref, py_ref, ay_ref, wp_ref, wa_ref, wo_ref, xo_ref, h2_ref = refs
    d = x_ref.shape[1]
    g_pool = _sigmoid(zg_ref[:, 0:d].astype(F32))
    g_attn = _sigmoid(zg_ref[:, d:2 * d].astype(F32))
    y = (g_pool * jnp.dot(py_ref[...], wp_ref[...], preferred_element_type=F32)
         + g_attn * jnp.dot(ay_ref[...], wa_ref[...], preferred_element_type=F32))
    o = jnp.dot(y.astype(BF16), wo_ref[...], preferred_element_type=F32)
    xn = x_ref[...] + mod_ref[2:3, :] * o
    xo_ref[...] = xn
    h2 = _rms(xn) * g2_ref[...] * (1.0 + mod_ref[4:5, :]) + mod_ref[3:4, :]
    h2_hi = h2.astype(BF16)
    h2_ref[...] = h2_hi
    if router:
        h2_lo = (h2 - h2_hi.astype(F32)).astype(BF16)
        logits = (jnp.dot(h2_hi, rh_ref[...], preferred_element_type=F32)
                  + jnp.dot(h2_hi, rl_ref[...], preferred_element_type=F32)
                  + jnp.dot(h2_lo, rh_ref[...], preferred_element_type=F32))
        lane = lax.broadcasted_iota(jnp.int32, logits.shape, 1).astype(F32)
        neg = jnp.float32(-jnp.inf)
        logits = jnp.where(lane < N_EXPERTS, logits, neg)
        v1 = jnp.max(logits, axis=1, keepdims=True)
        i1 = jnp.min(jnp.where(logits == v1, lane, float(LANES)), axis=1, keepdims=True)
        rest = jnp.where(lane == i1, neg, logits)
        v2 = jnp.max(rest, axis=1, keepdims=True)
        i2 = jnp.min(jnp.where(rest == v2, lane, float(LANES)), axis=1, keepdims=True)
        e2 = jnp.exp(v2 - v1)
        w1 = 1.0 / (1.0 + e2)
        gate_ref[...] = jnp.where(lane == i1, w1, 0.0) + jnp.where(lane == i2, e2 * w1, 0.0)


def _merge(x, mod, mod_row, g2, z, pool_y, attn_y, wp, wa, wo, router_w, t):
    b, l, d = x.shape
    router = router_w is not None
    tile = lambda w: pl.BlockSpec((None, t, w), lambda bi, i: (bi, i, 0))
    in_specs = [
        tile(d),
        pl.BlockSpec((None, 6, d), lambda bi, i: (mod_row(bi), 0, 0)),
        pl.BlockSpec((1, d), lambda bi, i: (0, 0)),
        tile(2 * d),
        tile(POOL_WIDTH), tile(QKV_WIDTH),
        _resident(wp.shape), _resident(wa.shape), _resident(wo.shape),
    ]
    args = [x, mod, g2, z, pool_y, attn_y, wp, wa, wo]
    out_specs = [tile(d), tile(d)]
    out_shape = [jax.ShapeDtypeStruct((b, l, d), F32), jax.ShapeDtypeStruct((b, l, d), BF16)]
    if router:
        in_specs += [_resident(router_w[0].shape)] * 2
        args += list(router_w)
        out_specs.append(tile(LANES))
        out_shape.append(jax.ShapeDtypeStruct((b, l, LANES), F32))
    return pl.pallas_call(
        functools.partial(_merge_kernel, router=router),
        grid=(b, l // t),
        in_specs=in_specs,
        out_specs=out_specs,
        out_shape=out_shape,
        compiler_params=_params("parallel", "parallel"),
        name="merge",
    )(*args)


def _swiglu(h, w1, w3, w2):
    a = jnp.dot(h, w1, preferred_element_type=F32)
    b = jnp.dot(h, w3, preferred_element_type=F32)
    return jnp.dot((a * _sigmoid(a) * b).astype(BF16), w2, preferred_element_type=F32)


def _ffn_kernel(*refs, final):
    if final:
        x_ref, h2_ref, mod_ref, w1_ref, w3_ref, w2_ref, fg_ref, o_ref = refs
    else:
        x_ref, h2_ref, mod_ref, w1_ref, w3_ref, w2_ref, o_ref = refs
    y = _swiglu(h2_ref[...], w1_ref[...], w3_ref[...], w2_ref[...])
    xn = x_ref[...] + mod_ref[5:6, :] * y
    if final:
        xn = _rms(xn) * fg_ref[...]
    o_ref[...] = xn


def _ffn(x, h2, mod, mod_row, w1, w3, w2, final_g, t):
    b, l, d = x.shape
    final = final_g is not None
    tile = pl.BlockSpec((None, t, d), lambda bi, i: (bi, i, 0))
    in_specs = [tile, tile, pl.BlockSpec((None, 6, d), lambda bi, i: (mod_row(bi), 0, 0)),
                _resident(w1.shape), _resident(w3.shape), _resident(w2.shape)]
    args = [x, h2, mod, w1, w3, w2]
    if final:
        in_specs.append(pl.BlockSpec((1, d), lambda bi, i: (0, 0)))
        args.append(final_g)
    return pl.pallas_call(
        functools.partial(_ffn_kernel, final=final),
        grid=(b, l // t),
        in_specs=in_specs,
        out_specs=tile,
        out_shape=jax.ShapeDtypeStruct((b, l, d), F32),
        compiler_params=_params("parallel", "parallel"),
        name="ffn",
    )(*args)


def _moe_kernel(*refs, final):
    if final:
        x_ref, h2_ref, gate_ref, mod_ref, w1_ref, w3_ref, w2_ref, fg_ref, o_ref, acc_ref = refs
    else:
        x_ref, h2_ref, gate_ref, mod_ref, w1_ref, w3_ref, w2_ref, o_ref, acc_ref = refs
    e = pl.program_id(2)

    @pl.when(e == 0)
    def _():
        acc_ref[...] = jnp.zeros_like(acc_ref)

    y = _swiglu(h2_ref[...], w1_ref[...], w3_ref[...], w2_ref[...])
    gate = gate_ref[...]
    lane = lax.broadcasted_iota(jnp.int32, gate.shape, 1)
    ge = jnp.sum(jnp.where(lane == e, gate, 0.0), axis=1, keepdims=True)
    acc_ref[...] += ge * y

    @pl.when(e == N_EXPERTS - 1)
    def _():
        xn = x_ref[...] + mod_ref[5:6, :] * acc_ref[...]
        if final:
            xn = _rms(xn) * fg_ref[...]
        o_ref[...] = xn


def _moe(x, h2, gate, mod, mod_row, w1, w3, w2, final_g, t):
    b, l, d = x.shape
    final = final_g is not None
    f = w1.shape[2]
    tile = lambda w: pl.BlockSpec((None, t, w), lambda bi, i, e: (bi, i, 0))
    in_specs = [tile(d), tile(d), tile(LANES),
                pl.BlockSpec((None, 6, d), lambda bi, i, e: (mod_row(bi), 0, 0)),
                pl.BlockSpec((None, d, f), lambda bi, i, e: (e, 0, 0)),
                pl.BlockSpec((None, d, f), lambda bi, i, e: (e, 0, 0)),
                pl.BlockSpec((None, f, d), lambda bi, i, e: (e, 0, 0))]
    args = [x, h2, gate, mod, w1, w3, w2]
    if final:
        in_specs.append(pl.BlockSpec((1, d), lambda bi, i, e: (0, 0)))
        args.append(final_g)
    return pl.pallas_call(
        functools.partial(_moe_kernel, final=final),
        grid=(b, l // t, N_EXPERTS),
        in_specs=in_specs,
        out_specs=tile(d),
        out_shape=jax.ShapeDtypeStruct((b, l, d), F32),
        scratch_shapes=[pltpu.VMEM((t, d), F32)],
        compiler_params=_params("parallel", "parallel", "arbitrary"),
        name="moe",
    )(*args)


MOE_TG = 512
MOE_TM = 512
PIECE = 16
BIG_PIECE = 64
MOE_SLOTS = 2 * MOE_TG + N_EXPERTS * PIECE
ROUTER_ROWS = 16


def _plan_kernel(x_ref, mod_ref, g2_ref, rh_ref, rl_ref, prow_ref, pcol_ref, cnt_ref):
    tg = x_ref.shape[0]
    h2 = _rms(x_ref[...]) * g2_ref[...] * (1.0 + mod_ref[4:5, :]) + mod_ref[3:4, :]
    h2_hi = h2.astype(BF16)
    h2_lo = (h2 - h2_hi.astype(F32)).astype(BF16)
    nt = (((1,), (1,)), ((), ()))
    logits = (lax.dot_general(rh_ref[...], h2_hi, nt, preferred_element_type=F32)
              + lax.dot_general(rl_ref[...], h2_hi, nt, preferred_element_type=F32)
              + lax.dot_general(rh_ref[...], h2_lo, nt, preferred_element_type=F32))
    sub = lax.broadcasted_iota(jnp.int32, logits.shape, 0).astype(F32)
    neg = jnp.float32(-jnp.inf)
    logits = jnp.where(sub < N_EXPERTS, logits, neg)
    v1 = jnp.max(logits, axis=0, keepdims=True)
    i1 = jnp.min(jnp.where(logits == v1, sub, float(ROUTER_ROWS)), axis=0, keepdims=True)
    rest = jnp.where(sub == i1, neg, logits)
    v2 = jnp.max(rest, axis=0, keepdims=True)
    i2 = jnp.min(jnp.where(rest == v2, sub, float(ROUTER_ROWS)), axis=0, keepdims=True)
    e2 = jnp.exp(v2 - v1)
    w1 = 1.0 / (1.0 + e2)
    w2 = e2 * w1
    two = w2 != 0.0
    sel = ((sub == i1) | ((sub == i2) & two)).astype(F32)
    before = (lax.broadcasted_iota(jnp.int32, (tg, tg), 0)
              < lax.broadcasted_iota(jnp.int32, (tg, tg), 1)).astype(BF16)
    rank = jnp.dot(sel.astype(BF16), before, preferred_element_type=F32)
    cnt = jnp.sum(sel, axis=1, keepdims=True)
    cpad = jnp.ceil(cnt * (1.0 / PIECE)) * PIECE
    subc = sub[:, 0:1]
    lbase = jnp.zeros_like(cpad)
    for e in range(N_EXPERTS - 1):
        c_e = jnp.sum(jnp.where(subc == e, cpad, 0.0), axis=0, keepdims=True)
        lbase = lbase + jnp.where(subc > e, c_e, 0.0)
    slot = lbase + rank
    e_a = jnp.where(two, jnp.minimum(i1, i2), i1)
    e_b = jnp.where(two, jnp.maximum(i1, i2), i1)
    ls_a = jnp.sum(jnp.where(sub == e_a, slot, 0.0), axis=0, keepdims=True)
    ls_b = jnp.sum(jnp.where(sub == e_b, slot, 0.0), axis=0, keepdims=True)
    w_a = jnp.where(e_a == i1, w1, w2)
    w_b = jnp.where(two, jnp.where(e_b == i1, w1, w2), 0.0)
    rows = jnp.concatenate([ls_a, ls_b, w_a, w_b, jnp.zeros((4, tg), F32)], axis=0)
    prow_ref[...] = rows
    pcol_ref[...] = jnp.concatenate([rows, jnp.zeros((LANES - 8, tg), F32)], axis=0).T
    cnt_ref[...] = jnp.broadcast_to(cnt[0:N_EXPERTS], (N_EXPERTS, LANES))


def _plan(x2, mod, mod_row, g2, rh, rl, tg):
    n, d = x2.shape
    nt = n // tg
    return pl.pallas_call(
        _plan_kernel,
        grid=(nt,),
        in_specs=[
            pl.BlockSpec((tg, d), lambda i: (i, 0)),
            pl.BlockSpec((None, 6, d), lambda i: (mod_row(i), 0, 0)),
            pl.BlockSpec((1, d), lambda i: (0, 0)),
            _resident(rh.shape), _resident(rl.shape),
        ],
        out_specs=[
            pl.BlockSpec((None, 8, tg), lambda i: (i, 0, 0)),
            pl.BlockSpec((tg, LANES), lambda i: (i, 0)),
            pl.BlockSpec((None, N_EXPERTS, LANES), lambda i: (i, 0, 0)),
        ],
        out_shape=[
            jax.ShapeDtypeStruct((nt, 8, tg), F32),
            jax.ShapeDtypeStruct((n, LANES), F32),
            jax.ShapeDtypeStruct((nt, N_EXPERTS, LANES), F32),
        ],
        compiler_params=_params("parallel"),
        name="moe_plan",
    )(x2, mod, g2, rh, rl)


def _segment_copies(i, seg, local_buf, sorted_hbm, sem, to_sorted, action):
    lb_ref, gb_ref, nb_ref, ns_ref = seg
    for e in range(N_EXPERTS):
        k = i * N_EXPERTS + e

        def copy(off, rows, k=k):
            lo = pl.multiple_of(lb_ref[k] + off, PIECE)
            go = pl.multiple_of(gb_ref[k] + off, PIECE)
            local = local_buf.at[pl.ds(lo, rows), :]
            remote = sorted_hbm.at[pl.ds(go, rows), :]
            c = pltpu.make_async_copy(local, remote, sem) if to_sorted else pltpu.make_async_copy(remote, local, sem)
            getattr(c, action)()

        def big(j, carry, copy=copy):
            copy(j * BIG_PIECE, BIG_PIECE)
            return carry

        def small(j, carry, copy=copy, k=k):
            copy(nb_ref[k] * BIG_PIECE + j * PIECE, PIECE)
            return carry

        lax.fori_loop(0, nb_ref[k], big, 0)
        lax.fori_loop(0, ns_ref[k], small, 0)


def _zero_fill(fill_ref, zbuf, out_ref, sem, action):
    def copy(go, rows):
        c = pltpu.make_async_copy(zbuf.at[pl.ds(0, rows), :], out_ref.at[pl.ds(go, rows), :], sem)
        getattr(c, action)()

    for e in range(N_EXPERTS):
        def tail(j, carry, e=e):
            copy(pl.multiple_of(fill_ref[e] + j * PIECE, PIECE), PIECE)
            return carry

        lax.fori_loop(0, fill_ref[N_EXPERTS + e], tail, 0)

    def rest(j, carry):
        copy(pl.multiple_of(fill_ref[2 * N_EXPERTS] + j * MOE_TM, MOE_TM), MOE_TM)
        return carry

    lax.fori_loop(0, fill_ref[2 * N_EXPERTS + 1], rest, 0)


def _dispatch_kernel(lb_ref, gb_ref, nb_ref, ns_ref, fill_ref, h2_ref, prow_ref, out_ref, cbuf, zbuf, sems):
    seg = (lb_ref, gb_ref, nb_ref, ns_ref)
    i = pl.program_id(0)
    nt = pl.num_programs(0)
    slot = i % 2
    tg = h2_ref.shape[0]

    @pl.when(i >= 2)
    def _():
        _segment_copies(i - 2, seg, cbuf.at[slot], out_ref, sems.at[slot], True, "wait")

    s = lax.broadcasted_iota(jnp.int32, (MOE_SLOTS, tg), 0).astype(F32)
    onehot = jnp.where(s == prow_ref[0:1, :], 1.0, jnp.where(s == prow_ref[1:2, :], 1.0, 0.0)).astype(BF16)
    cbuf[slot] = jnp.dot(onehot, h2_ref[...], preferred_element_type=F32).astype(BF16)
    _segment_copies(i, seg, cbuf.at[slot], out_ref, sems.at[slot], True, "start")

    @pl.when(i == nt - 1)
    def _():
        @pl.when(i >= 1)
        def _():
            _segment_copies(i - 1, seg, cbuf.at[1 - slot], out_ref, sems.at[1 - slot], True, "wait")

        _segment_copies(i, seg, cbuf.at[slot], out_ref, sems.at[slot], True, "wait")
        zbuf[...] = jnp.zeros_like(zbuf)
        _zero_fill(fill_ref, zbuf, out_ref, sems.at[2], "start")
        _zero_fill(fill_ref, zbuf, out_ref, sems.at[2], "wait")


def _dispatch(seg, fill, h2, prow, rows, tg):
    n, d = h2.shape
    grid_spec = pltpu.PrefetchScalarGridSpec(
        num_scalar_prefetch=len(seg) + 1,
        grid=(n // tg,),
        in_specs=[
            pl.BlockSpec((tg, d), lambda i, *_: (i, 0)),
            pl.BlockSpec((None, 8, tg), lambda i, *_: (i, 0, 0)),
        ],
        out_specs=pl.BlockSpec(memory_space=pl.ANY),
        scratch_shapes=[pltpu.VMEM((2, MOE_SLOTS, d), BF16), pltpu.VMEM((MOE_TM, d), BF16),
                        pltpu.SemaphoreType.DMA((3,))],
    )
    return pl.pallas_call(
        _dispatch_kernel,
        grid_spec=grid_spec,
        out_shape=jax.ShapeDtypeStruct((rows, d), BF16),
        compiler_params=_params("arbitrary"),
        name="moe_dispatch",
    )(*seg, fill, h2, prow)


def _group_kernel(te_ref, nu_ref, x_ref, w1_ref, w3_ref, w2_ref, o_ref):
    del te_ref
    live = pl.program_id(0) < nu_ref[0]

    @pl.when(live)
    def _():
        o_ref[...] = _swiglu(x_ref[...], w1_ref[...], w3_ref[...], w2_ref[...]).astype(BF16)

    @pl.when(jnp.logical_not(live))
    def _():
        o_ref[...] = jnp.zeros_like(o_ref)


def _group(tile_expert, n_used, xs, w1, w3, w2, tm):
    rows, d = xs.shape
    f = w1.shape[2]
    used = lambda r, te, nu: (jnp.minimum(r, nu[0] - 1), 0)
    grid_spec = pltpu.PrefetchScalarGridSpec(
        num_scalar_prefetch=2,
        grid=(rows // tm,),
        in_specs=[
            pl.BlockSpec((tm, d), used),
            pl.BlockSpec((None, d, f), lambda r, te, nu: (te[r], 0, 0)),
            pl.BlockSpec((None, d, f), lambda r, te, nu: (te[r], 0, 0)),
            pl.BlockSpec((None, f, d), lambda r, te, nu: (te[r], 0, 0)),
        ],
        out_specs=pl.BlockSpec((tm, d), lambda r, te, nu: (r, 0)),
    )
    return pl.pallas_call(
        _group_kernel,
        grid_spec=grid_spec,
        out_shape=jax.ShapeDtypeStruct((rows, d), BF16),
        compiler_params=_params("arbitrary"),
        name="moe_group",
    )(tile_expert, n_used, xs, w1, w3, w2)


def _combine_kernel(*refs, final):
    if final:
        lb_ref, gb_ref, nb_ref, ns_ref, x_ref, pcol_ref, mod_ref, fg_ref, y_ref, o_ref, ybuf, sems = refs
    else:
        lb_ref, gb_ref, nb_ref, ns_ref, x_ref, pcol_ref, mod_ref, y_ref, o_ref, ybuf, sems = refs
    seg = (lb_ref, gb_ref, nb_ref, ns_ref)
    i = pl.program_id(0)
    nt = pl.num_programs(0)
    slot = i % 2
    tg = x_ref.shape[0]

    @pl.when(i == 0)
    def _():
        ybuf[...] = jnp.zeros_like(ybuf)
        _segment_copies(i, seg, ybuf.at[slot], y_ref, sems.at[slot], False, "start")

    @pl.when(i + 1 < nt)
    def _():
        _segment_copies(i + 1, seg, ybuf.at[1 - slot], y_ref, sems.at[1 - slot], False, "start")

    s = lax.broadcasted_iota(jnp.int32, (tg, MOE_SLOTS), 1).astype(F32)
    pc = pcol_ref[...]
    scatter = jnp.where(s == pc[:, 0:1], pc[:, 2:3], jnp.where(s == pc[:, 1:2], pc[:, 3:4], 0.0)).astype(BF16)
    _segment_copies(i, seg, ybuf.at[slot], y_ref, sems.at[slot], False, "wait")
    y = jnp.dot(scatter, ybuf[slot], preferred_element_type=F32)
    xn = x_ref[...] + mod_ref[5:6, :] * y
    if final:
        xn = _rms(xn) * fg_ref[...]
    o_ref[...] = xn


def _combine(seg, x2, pcol, mod, mod_row, final_g, ys, tg):
    n, d = x2.shape
    final = final_g is not None
    in_specs = [
        pl.BlockSpec((tg, d), lambda i, *_: (i, 0)),
        pl.BlockSpec((tg, LANES), lambda i, *_: (i, 0)),
        pl.BlockSpec((None, 6, d), lambda i, *_: (mod_row(i), 0, 0)),
    ]
    args = [x2, pcol, mod]
    if final:
        in_specs.append(pl.BlockSpec((1, d), lambda i, *_: (0, 0)))
        args.append(final_g)
    in_specs.append(pl.BlockSpec(memory_space=pl.ANY))
    args.append(ys)
    grid_spec = pltpu.PrefetchScalarGridSpec(
        num_scalar_prefetch=len(seg),
        grid=(n // tg,),
        in_specs=in_specs,
        out_specs=pl.BlockSpec((tg, d), lambda i, *_: (i, 0)),
        scratch_shapes=[pltpu.VMEM((2, MOE_SLOTS, d), BF16), pltpu.SemaphoreType.DMA((2,))],
    )
    return pl.pallas_call(
        functools.partial(_combine_kernel, final=final),
        grid_spec=grid_spec,
        out_shape=jax.ShapeDtypeStruct((n, d), F32),
        compiler_params=_params("arbitrary"),
        name="moe_combine",
    )(*seg, *args)


def _routed_moe(x, mod, batch_row, g2, router_w, w1, w3, w2, h2, final_g):
    b, l, d = x.shape
    n = b * l
    tg, tm = MOE_TG, MOE_TM
    nt = n // tg
    x2, h22 = x.reshape(n, d), h2.reshape(n, d)
    mod_row = lambda i: batch_row((i * tg) // l)
    rw = jnp.pad(router_w.T, ((0, ROUTER_ROWS - N_EXPERTS), (0, 0)))
    rh = rw.astype(BF16)
    rl = (rw - rh.astype(F32)).astype(BF16)
    prow, pcol, cnt = _plan(x2, mod, mod_row, g2, rh, rl, tg)

    cnt = cnt[:, :, 0].astype(jnp.int32)
    cpad = (cnt + PIECE - 1) // PIECE * PIECE
    lbase = jnp.cumsum(cpad, axis=1) - cpad
    tot = (jnp.sum(cpad, axis=0) + tm - 1) // tm * tm
    ends = jnp.cumsum(tot)
    gbase = (ends - tot)[None, :] + jnp.cumsum(cpad, axis=0) - cpad
    rows = (2 * n + nt * N_EXPERTS * (PIECE - 1) + N_EXPERTS * (tm - 1) + tm - 1) // tm * tm
    n_used = (ends[-1] // tm).astype(jnp.int32)
    tile_start = jnp.arange(rows // tm, dtype=jnp.int32) * tm
    tile_expert = jnp.searchsorted(ends, jnp.minimum(tile_start, ends[-1] - 1), side="right").astype(jnp.int32)
    seg = tuple(a.reshape(-1).astype(jnp.int32)
                for a in (lbase, gbase, cpad // BIG_PIECE, cpad % BIG_PIECE // PIECE))

    data_end = ends - tot + jnp.sum(cpad, axis=0)
    fill = jnp.concatenate([data_end, (ends - data_end) // PIECE, ends[-1:], (rows - ends[-1:]) // tm]).astype(jnp.int32)

    xs = _dispatch(seg, fill, h22, prow, rows, tg)
    ys = _group(tile_expert, n_used.reshape(1), xs, w1, w3, w2, tm)
    out = _combine(seg, x2, pcol, mod, mod_row, final_g, ys, tg)
    return out.reshape(b, l, d)


def _rope_tables(l):
    rows = l // GRID_W
    row = jnp.repeat(jnp.arange(rows, dtype=F32), GRID_W)
    colp = jnp.tile(jnp.arange(GRID_W, dtype=F32), rows)
    inv = ROPE_THETA ** (-jnp.arange(ROPE_PAIRS, dtype=F32) * 2.0 / AXIS_DIM)
    ang_r, ang_c = row[:, None] * inv, colp[:, None] * inv
    zero = jnp.zeros_like(ang_r)
    cos64 = jnp.concatenate([jnp.cos(ang_r)] * 2 + [jnp.cos(ang_c)] * 2, axis=1)
    up64 = jnp.concatenate([-jnp.sin(ang_r), zero, -jnp.sin(ang_c), zero], axis=1)
    dn64 = jnp.concatenate([zero, jnp.sin(ang_r), zero, jnp.sin(ang_c)], axis=1)
    return tuple(jnp.tile(a, (1, LANES // HEAD_DIM)) for a in (cos64, up64, dn64))


def _lambda_init(layer):
    return 0.8 - 0.6 * math.exp(-0.3 * layer)


def _permute_in(w):
    off_q = POOL_WIDTH
    off_g = off_q + 3 * QKV_WIDTH
    return jnp.concatenate([w[:, off_g:], w[:, off_q:off_g], w[:, :off_q]], axis=1).astype(BF16)


def kernel(x, c, ctx, c_ctx, w_mod, b_mod, norm1_g, norm2_g, w_in, pool_w, pool_scale, lam_q1, lam_k1,
           lam_q2, lam_k2, subln_g, w_pool_proj, w_attn_proj, w_out, ffn_w1, ffn_w3, ffn_w2, router_w,
           moe_w1, moe_w3, moe_w2, final_g):
    b, l, d = x.shape
    n_ctx = ctx.shape[1]
    depth = w_mod.shape[0]
    assert d == D_MODEL and b + 1 <= MOD_ROWS and l % 512 == 0 and n_ctx % 256 == 0

    s_in = jnp.concatenate([c, c_ctx[None, :], jnp.zeros((MOD_ROWS - b - 1, d), F32)], axis=0)
    mod_all = _modulation(s_in, w_mod, b_mod)
    tables = _rope_tables(l)
    lat_row = lambda bi: bi
    ctx_row = lambda bi: b
    gq, gk, gp = COL_Q // V_DIM, COL_K // V_DIM, COL_P // POOL_WIDTH
    full_kinds = ("g",) * 4 + ("q",) * 2 + ("k",) * 2 + ("v",) * 2 + ("p",)

    xc = ctx
    for layer in range(depth):
        last = layer == depth - 1
        lam0 = _lambda_init(layer)
        mod = mod_all[layer].reshape(MOD_ROWS, 6, d)
        g1 = norm1_g[layer][None, :]
        g2 = norm2_g[layer][None, :]
        w_in_p = _permute_in(w_in[layer])
        lam_p = jnp.stack([lam_q1[layer], lam_k1[layer], lam_q2[layer], lam_k2[layer]])
        g_col = subln_g[layer][:, None]
        pw = pool_w[layer].astype(BF16)
        ps = pool_scale[layer][None, :]
        wp = w_pool_proj[layer].astype(BF16)
        wa = w_attn_proj[layer].astype(BF16)
        wo = w_out[layer].astype(BF16)
        fg = final_g[None, :] if last else None
        moe_layer = layer % 2 == 1
        if moe_layer:
            rw = jnp.pad(router_w[layer // 2], ((0, 0), (0, LANES - N_EXPERTS)))
            rw_hi = rw.astype(BF16)
            rw_lo = (rw - rw_hi.astype(F32)).astype(BF16)
            routers = (rw_hi, rw_lo)
            we1 = moe_w1[layer // 2].astype(BF16)
            we3 = moe_w3[layer // 2].astype(BF16)
            we2 = moe_w2[layer // 2].astype(BF16)
        else:
            routers = None
            wf1 = ffn_w1[layer // 2].astype(BF16)
            wf3 = ffn_w3[layer // 2].astype(BF16)
            wf2 = ffn_w2[layer // 2].astype(BF16)

        def mixer(xs, h2, gate, row, t):
            if moe_layer:
                return _moe(xs, h2, gate, mod, row, we1, we3, we2, fg, t)
            return _ffn(xs, h2, mod, row, wf1, wf3, wf2, fg, min(t, 512))

        z, vt = _inproj(x, mod, lat_row, g1, w_in_p, full_kinds, tables, 512)
        if last:
            zc, vtc = _inproj(xc, mod, ctx_row, g1, w_in_p[:, COL_K:COL_V + QKV_WIDTH], ("k",) * 2 + ("v",) * 2,
                              None, n_ctx)
            ckb = 0
        else:
            zc, vtc = _inproj(xc, mod, ctx_row, g1, w_in_p, full_kinds, None, n_ctx)
            ckb = gk
        attn_y = _attention(lam_p, g_col, z, gq, zc, ckb, vtc, z, gk, vt, lam0, 512, 1)
        pool_y = _pool(z, gp, pw, ps)
        x, h2 = _merge(x, mod, lat_row, g2, z, pool_y, attn_y, wp, wa, wo, None, 512)

        if not last:
            attn_yc = _attention(lam_p, g_col, zc, COL_Q // QKV_WIDTH, zc, COL_K // QKV_WIDTH, vtc, None, 0, None,
                                 lam0, 512, N_HEADS)
            pool_yc = _pool(zc, gp, pw, ps)
            resc = _merge(xc, mod, ctx_row, g2, zc, pool_yc, attn_yc, wp, wa, wo, routers, n_ctx)
            xc, h2c = resc[0], resc[1]
            gatec = resc[2] if moe_layer else None

        if moe_layer:
            x = _routed_moe(x, mod, lat_row, g2, router_w[layer // 2], we1, we3, we2, h2, fg)
        else:
            x = mixer(x, h2, None, lat_row, 512)
        if not last:
            xc = mixer(xc, h2c, gatec, ctx_row, n_ctx)
    return x
```

```python
import functools
import math

import jax
import jax.numpy as jnp
from jax import lax
from jax.experimental import pallas as pl
from jax.experimental.pallas import tpu as pltpu

F32 = jnp.float32
BF16 = jnp.bfloat16

D_MODEL = 1024
EPS = 1e-6
GRID_W = 64
N_HEADS = 8
HEAD_DIM = 64
V_DIM = 2 * HEAD_DIM
ROPE_THETA = 10000.0
AXIS_DIM = HEAD_DIM // 2
ROPE_PAIRS = AXIS_DIM // 2
POOL_WINDOWS = (2, 4, 8, 16)
POOL_WIDTH = 512
POOL_GROUP_DIM = POOL_WIDTH // len(POOL_WINDOWS)
N_EXPERTS = 8
Q_SCALE = HEAD_DIM ** -0.5
LOG2E = math.log2(math.e)
MAX_UNSHIFTED_LOG2 = 80.0
SQ_NORM_SLACK = 1.05

QKV_WIDTH = N_HEADS * V_DIM
COL_G = 0
COL_Q = COL_G + 2 * D_MODEL
COL_K = COL_Q + QKV_WIDTH
COL_V = COL_K + QKV_WIDTH
IN_WIDTH = COL_V + QKV_WIDTH + POOL_WIDTH
COL_P = COL_V
IN_CHUNK = 512

LANES = 128
MOD_ROWS = 40

VMEM_LIMIT = 56 * 1024 * 1024


def _resident(shape):
    nd = len(shape)
    return pl.BlockSpec(shape, lambda *_: (0,) * nd, pipeline_mode=pl.Buffered(1))


def _params(*sem):
    return pltpu.CompilerParams(dimension_semantics=sem, vmem_limit_bytes=VMEM_LIMIT)


def _sigmoid(v):
    return 1.0 / (1.0 + jnp.exp(-v))


def _rms(v):
    return v * lax.rsqrt(jnp.mean(v * v, axis=-1, keepdims=True) + EPS)


def _mod_kernel(s_ref, w_ref, b_ref, o_ref):
    s = s_ref[...]
    s = s * _sigmoid(s)
    w = w_ref[...]
    s_hi = s.astype(BF16)
    s_lo = (s - s_hi.astype(F32)).astype(BF16)
    w_hi = w.astype(BF16)
    w_lo = (w - w_hi.astype(F32)).astype(BF16)
    acc = jnp.dot(s_hi, w_hi, preferred_element_type=F32)
    acc += jnp.dot(s_hi, w_lo, preferred_element_type=F32)
    acc += jnp.dot(s_lo, w_hi, preferred_element_type=F32)
    o_ref[...] = acc + b_ref[...]


def _modulation(s_in, w_mod, b_mod):
    depth, d, n = w_mod.shape
    tn = 1024
    return pl.pallas_call(
        _mod_kernel,
        grid=(depth, n // tn),
        in_specs=[
            pl.BlockSpec((MOD_ROWS, d), lambda l, j: (0, 0)),
            pl.BlockSpec((None, d, tn), lambda l, j: (l, 0, j)),
            pl.BlockSpec((None, 1, tn), lambda l, j: (l, 0, j)),
        ],
        out_specs=pl.BlockSpec((None, MOD_ROWS, tn), lambda l, j: (l, 0, j)),
        out_shape=jax.ShapeDtypeStruct((depth, MOD_ROWS, n), F32),
        compiler_params=_params("parallel", "parallel"),
        name="modulation",
    )(s_in, w_mod, b_mod.reshape(depth, 1, n))


def _rope(z, c, s_up, s_dn):
    up = pltpu.roll(z, LANES - ROPE_PAIRS, 1)
    dn = pltpu.roll(z, ROPE_PAIRS, 1)
    return z * c + up * s_up + dn * s_dn


def _inproj_kernel(*refs, kinds, rope):
    if rope:
        x_ref, mod_ref, g_ref, w_ref, c_ref, su_ref, sd_ref, z_ref, vt_ref = refs
    else:
        x_ref, mod_ref, g_ref, w_ref, z_ref, vt_ref = refs
    y = _rms(x_ref[...]) * g_ref[...]
    h = (y * (1.0 + mod_ref[1:2, :]) + mod_ref[0:1, :]).astype(BF16)
    z_col, head = 0, 0
    for ci, kind in enumerate(kinds):
        lo = ci * IN_CHUNK
        z = jnp.dot(h, w_ref[:, lo:lo + IN_CHUNK], preferred_element_type=F32)
        if kind == "v":
            for j in range(0, IN_CHUNK, V_DIM):
                vt_ref[head * V_DIM:(head + 1) * V_DIM, :] = z[:, j:j + V_DIM].T.astype(BF16)
                head += 1
            continue
        if rope and kind in ("q", "k"):
            c, su, sd = c_ref[...], su_ref[...], sd_ref[...]
            z = jnp.concatenate(
                [_rope(z[:, j:j + LANES], c, su, sd) for j in range(0, IN_CHUNK, LANES)], axis=1)
        if kind == "q":
            z = z * (Q_SCALE * LOG2E)
        z_ref[:, z_col:z_col + IN_CHUNK] = z.astype(BF16)
        z_col += IN_CHUNK


def _inproj(x, mod, mod_row, g, w, kinds, tables, t):
    b, l, d = x.shape
    n_v = sum(k == "v" for k in kinds)
    wz = IN_CHUNK * (len(kinds) - n_v)
    rows_vt = n_v * IN_CHUNK
    rope = tables is not None
    in_specs = [
        pl.BlockSpec((None, t, d), lambda bi, i: (bi, i, 0)),
        pl.BlockSpec((None, 6, d), lambda bi, i: (mod_row(bi), 0, 0)),
        pl.BlockSpec((1, d), lambda bi, i: (0, 0)),
        _resident((d, IN_CHUNK * len(kinds))),
    ]
    args = [x, mod, g, w]
    if rope:
        in_specs += [pl.BlockSpec((t, LANES), lambda bi, i: (i, 0))] * 3
        args += list(tables)
    return pl.pallas_call(
        functools.partial(_inproj_kernel, kinds=kinds, rope=rope),
        grid=(b, l // t),
        in_specs=in_specs,
        out_specs=[pl.BlockSpec((None, t, wz), lambda bi, i: (bi, i, 0)),
                   pl.BlockSpec((None, rows_vt, t), lambda bi, i: (bi, 0, i))],
        out_shape=[jax.ShapeDtypeStruct((b, l, wz), BF16), jax.ShapeDtypeStruct((b, rows_vt, l), BF16)],
        compiler_params=_params("parallel", "parallel"),
        name="inproj",
    )(*args)


def _pool_kernel(u_ref, pw_ref, ps_ref, o_ref):
    l = u_ref.shape[0]
    t = lax.broadcasted_iota(jnp.int32, (l, POOL_GROUP_DIM), 0)

    def shifted(a, k):
        if k > 0:
            return jnp.where(t >= k, pltpu.roll(a, k, 0), 0.0)
        return jnp.where(t < l + k, pltpu.roll(a, l + k, 0), 0.0)

    for gi, w in enumerate(POOL_WINDOWS):
        lo = gi * POOL_GROUP_DIM
        u = u_ref[:, lo:lo + POOL_GROUP_DIM].astype(F32)
        back, fwd, span = u, u, 1
        while span < w // 2:
            back = back + shifted(back, span)
            fwd = fwd + shifted(fwd, -span)
            span *= 2
        win = shifted(back, 1) + fwd
        cnt = (jnp.minimum(t + w // 2, l) - jnp.maximum(t - w // 2, 0)).astype(F32)
        m = (win / cnt - u).astype(BF16)
        y = jnp.dot(m, pw_ref[gi], preferred_element_type=F32)
        o_ref[:, lo:lo + POOL_GROUP_DIM] = (y * ps_ref[:, lo:lo + POOL_GROUP_DIM]).astype(BF16)


def _pool(z, col_block, pool_w, pool_scale):
    b, l, _ = z.shape
    return pl.pallas_call(
        _pool_kernel,
        grid=(b,),
        in_specs=[
            pl.BlockSpec((None, l, POOL_WIDTH), lambda bi: (bi, 0, col_block)),
            _resident(pool_w.shape),
            pl.BlockSpec((1, POOL_WIDTH), lambda bi: (0, 0)),
        ],
        out_specs=pl.BlockSpec((None, l, POOL_WIDTH), lambda bi: (bi, 0, 0)),
        out_shape=jax.ShapeDtypeStruct((b, l, POOL_WIDTH), BF16),
        compiler_params=_params("parallel"),
        name="pool",
    )(z, pool_w, pool_scale)


def _attn_kernel(*refs, n_lat, tq, lam0, heads):
    if n_lat:
        lam_ref, g_ref, q_ref, kc_ref, vtc_ref, kl_ref, vtl_ref, o_ref, p_sc = refs
    else:
        lam_ref, g_ref, q_ref, kc_ref, vtc_ref, o_ref, p_sc = refs
    lp = lam_ref[...]
    lam = (jnp.exp(jnp.sum(lp[0:1] * lp[1:2], axis=1, keepdims=True))
           - jnp.exp(jnp.sum(lp[2:3] * lp[3:4], axis=1, keepdims=True)) + lam0)
    nq = q_ref.shape[0]
    lane = lax.broadcasted_iota(jnp.int32, (tq, V_DIM), 1)
    half = lax.broadcasted_iota(jnp.int32, (8, V_DIM), 1) // HEAD_DIM
    sel = (half == lax.broadcasted_iota(jnp.int32, (8, V_DIM), 0)).astype(BF16)
    nt = (((1,), (1,)), ((), ()))

    def sq_norm_max(a):
        return jnp.max(lax.dot_general(sel, a * a, nt, preferred_element_type=F32), axis=1, keepdims=True)

    def head_groups(hh):
        cs = slice(hh * V_DIM, (hh + 1) * V_DIM)
        groups = [(kc_ref[:, cs], vtc_ref[cs, :])]
        if n_lat:
            groups.append((kl_ref[:, cs], vtl_ref[cs, :]))
        return cs, groups

    bound = None
    for hh in range(heads):
        cs, groups = head_groups(hh)
        k_sq = sq_norm_max(groups[0][0])
        for k, _ in groups[1:]:
            k_sq = jnp.maximum(k_sq, sq_norm_max(k))
        b_h = sq_norm_max(q_ref[:, cs]) * k_sq
        bound = b_h if bound is None else jnp.maximum(bound, b_h)
    small = jnp.max(bound) * SQ_NORM_SLACK < MAX_UNSHIFTED_LOG2 ** 2

    def scores(shift, groups, cs, c0, slot):
        q = q_ref[c0:c0 + tq, cs]
        sums = []
        for mi, first in enumerate((True, False)):
            qm = jnp.where((lane < HEAD_DIM) if first else (lane >= HEAD_DIM), q, jnp.zeros_like(q))
            s_t = [lax.dot_general(k, qm, nt, preferred_element_type=F32) for k, _ in groups]
            if shift:
                m = jnp.max(s_t[0], axis=0, keepdims=True)
                for s_g in s_t[1:]:
                    m = jnp.maximum(m, jnp.max(s_g, axis=0, keepdims=True))
                s_t = [s_g - m for s_g in s_t]
            l_m, row = None, 0
            for s_g in s_t:
                p_g = jnp.exp2(s_g)
                l_g = jnp.sum(p_g, axis=0, keepdims=True)
                l_m = l_g if l_m is None else l_m + l_g
                p_sc[slot, mi, row:row + p_g.shape[0], :] = p_g.astype(BF16)
                row += p_g.shape[0]
            sums.append(l_m)
        return sums

    def values(groups, cs, c0, slot, sums):
        r = (lam * sums[0] * (1.0 / sums[1])).astype(BF16)
        o_t, row = None, 0
        for k, vt in groups:
            rows = slice(row, row + k.shape[0])
            part = jnp.dot(vt, p_sc[slot, 0, rows, :] - r * p_sc[slot, 1, rows, :], preferred_element_type=F32)
            o_t = part if o_t is None else o_t + part
            row += k.shape[0]
        o_t = o_t * (1.0 / sums[0])
        o_t = o_t * lax.rsqrt(jnp.mean(o_t * o_t, axis=0, keepdims=True) + EPS)
        o_t = o_t * (g_ref[...] * (1.0 - lam0))
        o_ref[c0:c0 + tq, cs] = o_t.T.astype(BF16)

    def attend(shift):
        work = [(hh, c0) for hh in range(heads) for c0 in range(0, nq, tq)]
        pending = None
        for i, (hh, c0) in enumerate(work):
            cs, groups = head_groups(hh)
            sums = scores(shift, groups, cs, c0, i % 2)
            if pending is not None:
                values(*pending)
            pending = (groups, cs, c0, i % 2, sums)
        values(*pending)

    lax.cond(small, lambda: attend(False), lambda: attend(True))


def _attention(lam_p, g_col, zq, q_blk, zc, kc_blk, vtc, zl, kl_blk, vtl, lam0, tq, heads):
    b, nq, _ = zq.shape
    n_ctx = zc.shape[1]
    n_lat = 0 if zl is None else zl.shape[1]
    w = heads * V_DIM

    def col(blk, n):
        return pl.BlockSpec((None, n, w), lambda bi, h: (bi, 0, blk + h))

    def vt_rows(n):
        return pl.BlockSpec((None, w, n), lambda bi, h: (bi, h, 0))

    in_specs = [
        pl.BlockSpec(lam_p.shape, lambda bi, h: (0, 0)),
        pl.BlockSpec(g_col.shape, lambda bi, h: (0, 0)),
        col(q_blk, nq), col(kc_blk, n_ctx), vt_rows(n_ctx),
    ]
    args = [lam_p, g_col, zq, zc, vtc]
    if n_lat:
        in_specs += [col(kl_blk, n_lat), vt_rows(n_lat)]
        args += [zl, vtl]
    tq = min(tq, nq)
    return pl.pallas_call(
        functools.partial(_attn_kernel, n_lat=n_lat, tq=tq, lam0=lam0, heads=heads),
        grid=(b, N_HEADS // heads),
        in_specs=in_specs,
        out_specs=pl.BlockSpec((None, nq, w), lambda bi, h: (bi, 0, h)),
        out_shape=jax.ShapeDtypeStruct((b, nq, QKV_WIDTH), BF16),
        scratch_shapes=[pltpu.VMEM((2, 2, n_ctx + n_lat, tq), BF16)],
        compiler_params=_params("parallel", "parallel"),
        name="attention",
    )(*args)


def _merge_kernel(*refs, plan):
    if plan:
        (x_ref, mod_ref, g2_ref, zg_ref, py_ref, ay_ref, wp_ref, wa_ref, wo_ref, rh_ref, rl_ref,
         xo_ref, h2_ref, prow_ref, pcol_ref, cnt_ref) = refs
    else:
        x_ref, mod_ref, g2_ref, zg_ref, py_ref, ay_ref, wp_ref, wa_ref, wo_ref, xo_ref, h2_ref = refs
    d = x_ref.shape[1]
    g_pool = _sigmoid(zg_ref[:, 0:d].astype(F32))
    g_attn = _sigmoid(zg_ref[:, d:2 * d].astype(F32))
    y = (g_pool * jnp.dot(py_ref[...], wp_ref[...], preferred_element_type=F32)
         + g_attn * jnp.dot(ay_ref[...], wa_ref[...], preferred_element_type=F32))
    o = jnp.dot(y.astype(BF16), wo_ref[...], preferred_element_type=F32)
    xn = x_ref[...] + mod_ref[2:3, :] * o
    xo_ref[...] = xn
    h2 = _rms(xn) * g2_ref[...] * (1.0 + mod_ref[4:5, :]) + mod_ref[3:4, :]
    h2_hi = h2.astype(BF16)
    h2_ref[...] = h2_hi
    if plan:
        h2_lo = (h2 - h2_hi.astype(F32)).astype(BF16)
        rows, cnt = _route_plan(h2_hi, h2_lo, rh_ref[...], rl_ref[...])
        prow_ref[...] = rows
        pcol_ref[...] = jnp.concatenate([rows, jnp.zeros((LANES - 8, rows.shape[1]), F32)], axis=0).T
        cnt_ref[...] = jnp.broadcast_to(cnt, (N_EXPERTS, LANES))


def _merge(x, mod, mod_row, g2, z, pool_y, attn_y, wp, wa, wo, router_w, t):
    b, l, d = x.shape
    plan = router_w is not None
    nt = l // t
    tile = lambda w: pl.BlockSpec((None, t, w), lambda bi, i: (bi, i, 0))
    in_specs = [
        tile(d),
        pl.BlockSpec((None, 6, d), lambda bi, i: (mod_row(bi), 0, 0)),
        pl.BlockSpec((1, d), lambda bi, i: (0, 0)),
        tile(2 * d),
        tile(POOL_WIDTH), tile(QKV_WIDTH),
        _resident(wp.shape), _resident(wa.shape), _resident(wo.shape),
    ]
    args = [x, mod, g2, z, pool_y, attn_y, wp, wa, wo]
    out_specs = [tile(d), tile(d)]
    out_shape = [jax.ShapeDtypeStruct((b, l, d), F32), jax.ShapeDtypeStruct((b, l, d), BF16)]
    if plan:
        assert t == MOE_TG
        in_specs += [_resident(router_w[0].shape)] * 2
        args += list(router_w)
        out_specs += [pl.BlockSpec((None, 8, t), lambda bi, i: (bi * nt + i, 0, 0)),
                      tile(LANES),
                      pl.BlockSpec((None, N_EXPERTS, LANES), lambda bi, i: (bi * nt + i, 0, 0))]
        out_shape += [jax.ShapeDtypeStruct((b * nt, 8, t), F32),
                      jax.ShapeDtypeStruct((b, l, LANES), F32),
                      jax.ShapeDtypeStruct((b * nt, N_EXPERTS, LANES), F32)]
    return pl.pallas_call(
        functools.partial(_merge_kernel, plan=plan),
        grid=(b, nt),
        in_specs=in_specs,
        out_specs=out_specs,
        out_shape=out_shape,
        compiler_params=_params("parallel", "parallel"),
        name="merge",
    )(*args)


def _swiglu(h, w1, w3, w2):
    a = jnp.dot(h, w1, preferred_element_type=F32)
    b = jnp.dot(h, w3, preferred_element_type=F32)
    return jnp.dot((a * _sigmoid(a) * b).astype(BF16), w2, preferred_element_type=F32)


def _ffn_kernel(*refs, final):
    if final:
        x_ref, h2_ref, mod_ref, w1_ref, w3_ref, w2_ref, fg_ref, o_ref = refs
    else:
        x_ref, h2_ref, mod_ref, w1_ref, w3_ref, w2_ref, o_ref = refs
    y = _swiglu(h2_ref[...], w1_ref[...], w3_ref[...], w2_ref[...])
    xn = x_ref[...] + mod_ref[5:6, :] * y
    if final:
        xn = _rms(xn) * fg_ref[...]
    o_ref[...] = xn


def _ffn(x, h2, mod, mod_row, w1, w3, w2, final_g, t):
    b, l, d = x.shape
    final = final_g is not None
    tile = pl.BlockSpec((None, t, d), lambda bi, i: (bi, i, 0))
    in_specs = [tile, tile, pl.BlockSpec((None, 6, d), lambda bi, i: (mod_row(bi), 0, 0)),
                _resident(w1.shape), _resident(w3.shape), _resident(w2.shape)]
    args = [x, h2, mod, w1, w3, w2]
    if final:
        in_specs.append(pl.BlockSpec((1, d), lambda bi, i: (0, 0)))
        args.append(final_g)
    return pl.pallas_call(
        functools.partial(_ffn_kernel, final=final),
        grid=(b, l // t),
        in_specs=in_specs,
        out_specs=tile,
        out_shape=jax.ShapeDtypeStruct((b, l, d), F32),
        compiler_params=_params("parallel", "parallel"),
        name="ffn",
    )(*args)


MOE_TG = 512
MOE_TM = 512
PIECE = 16
BIG_PIECE = 64
MOE_SLOTS = 2 * MOE_TG + N_EXPERTS * PIECE
ROUTER_ROWS = 16


def _route_plan(h2_hi, h2_lo, rh, rl):
    tg = h2_hi.shape[0]
    nt = (((1,), (1,)), ((), ()))
    logits = (lax.dot_general(rh, h2_hi, nt, preferred_element_type=F32)
              + lax.dot_general(rl, h2_hi, nt, preferred_element_type=F32)
              + lax.dot_general(rh, h2_lo, nt, preferred_element_type=F32))
    sub = lax.broadcasted_iota(jnp.int32, logits.shape, 0).astype(F32)
    neg = jnp.float32(-jnp.inf)
    logits = jnp.where(sub < N_EXPERTS, logits, neg)
    v1 = jnp.max(logits, axis=0, keepdims=True)
    i1 = jnp.min(jnp.where(logits == v1, sub, float(ROUTER_ROWS)), axis=0, keepdims=True)
    rest = jnp.where(sub == i1, neg, logits)
    v2 = jnp.max(rest, axis=0, keepdims=True)
    i2 = jnp.min(jnp.where(rest == v2, sub, float(ROUTER_ROWS)), axis=0, keepdims=True)
    e2 = jnp.exp(v2 - v1)
    w1 = 1.0 / (1.0 + e2)
    w2 = e2 * w1
    two = w2 != 0.0
    sel = ((sub == i1) | ((sub == i2) & two)).astype(F32)
    before = (lax.broadcasted_iota(jnp.int32, (tg, tg), 0)
              < lax.broadcasted_iota(jnp.int32, (tg, tg), 1)).astype(BF16)
    rank = jnp.dot(sel.astype(BF16), before, preferred_element_type=F32)
    cnt = jnp.sum(sel, axis=1, keepdims=True)
    cpad = jnp.ceil(cnt * (1.0 / PIECE)) * PIECE
    subc = sub[:, 0:1]
    lbase = jnp.zeros_like(cpad)
    for e in range(N_EXPERTS - 1):
        c_e = jnp.sum(jnp.where(subc == e, cpad, 0.0), axis=0, keepdims=True)
        lbase = lbase + jnp.where(subc > e, c_e, 0.0)
    slot = lbase + rank
    e_a = jnp.where(two, jnp.minimum(i1, i2), i1)
    e_b = jnp.where(two, jnp.maximum(i1, i2), i1)
    ls_a = jnp.sum(jnp.where(sub == e_a, slot, 0.0), axis=0, keepdims=True)
    ls_b = jnp.sum(jnp.where(sub == e_b, slot, 0.0), axis=0, keepdims=True)
    w_a = jnp.where(e_a == i1, w1, w2)
    w_b = jnp.where(two, jnp.where(e_b == i1, w1, w2), 0.0)
    rows = jnp.concatenate([ls_a, ls_b, w_a, w_b, jnp.zeros((4, tg), F32)], axis=0)
    return rows, cnt[0:N_EXPERTS]


def _segment_copies(i, seg, local_buf, sorted_hbm, sem, to_sorted, action):
    lb_ref, gb_ref, nb_ref, ns_ref = seg
    for e in range(N_EXPERTS):
        k = i * N_EXPERTS + e

        def copy(off, rows, k=k):
            lo = pl.multiple_of(lb_ref[k] + off, PIECE)
            go = pl.multiple_of(gb_ref[k] + off, PIECE)
            local = local_buf.at[pl.ds(lo, rows), :]
            remote = sorted_hbm.at[pl.ds(go, rows), :]
            c = pltpu.make_async_copy(local, remote, sem) if to_sorted else pltpu.make_async_copy(remote, local, sem)
            getattr(c, action)()

        def big(j, carry, copy=copy):
            copy(j * BIG_PIECE, BIG_PIECE)
            return carry

        def small(j, carry, copy=copy, k=k):
            copy(nb_ref[k] * BIG_PIECE + j * PIECE, PIECE)
            return carry

        lax.fori_loop(0, nb_ref[k], big, 0)
        lax.fori_loop(0, ns_ref[k], small, 0)


def _zero_fill(fill_ref, zbuf, out_ref, sem, action):
    def copy(go, rows):
        c = pltpu.make_async_copy(zbuf.at[pl.ds(0, rows), :], out_ref.at[pl.ds(go, rows), :], sem)
        getattr(c, action)()

    for e in range(N_EXPERTS):
        def tail(j, carry, e=e):
            copy(pl.multiple_of(fill_ref[e] + j * PIECE, PIECE), PIECE)
            return carry

        lax.fori_loop(0, fill_ref[N_EXPERTS + e], tail, 0)

    def rest(j, carry):
        copy(pl.multiple_of(fill_ref[2 * N_EXPERTS] + j * MOE_TM, MOE_TM), MOE_TM)
        return carry

    lax.fori_loop(0, fill_ref[2 * N_EXPERTS + 1], rest, 0)


def _dispatch_kernel(lb_ref, gb_ref, nb_ref, ns_ref, fill_ref, h2_ref, prow_ref, out_ref, cbuf, zbuf, sems):
    seg = (lb_ref, gb_ref, nb_ref, ns_ref)
    i = pl.program_id(0)
    nt = pl.num_programs(0)
    slot = i % 2
    tg = h2_ref.shape[0]

    @pl.when(i >= 2)
    def _():
        _segment_copies(i - 2, seg, cbuf.at[slot], out_ref, sems.at[slot], True, "wait")

    s = lax.broadcasted_iota(jnp.int32, (MOE_SLOTS, tg), 0).astype(F32)
    onehot = jnp.where(s == prow_ref[0:1, :], 1.0, jnp.where(s == prow_ref[1:2, :], 1.0, 0.0)).astype(BF16)
    cbuf[slot] = jnp.dot(onehot, h2_ref[...], preferred_element_type=F32).astype(BF16)
    _segment_copies(i, seg, cbuf.at[slot], out_ref, sems.at[slot], True, "start")

    @pl.when(i == nt - 1)
    def _():
        @pl.when(i >= 1)
        def _():
            _segment_copies(i - 1, seg, cbuf.at[1 - slot], out_ref, sems.at[1 - slot], True, "wait")

        _segment_copies(i, seg, cbuf.at[slot], out_ref, sems.at[slot], True, "wait")
        zbuf[...] = jnp.zeros_like(zbuf)
        _zero_fill(fill_ref, zbuf, out_ref, sems.at[2], "start")
        _zero_fill(fill_ref, zbuf, out_ref, sems.at[2], "wait")


def _dispatch(seg, fill, h2, prow, rows, tg):
    n, d = h2.shape
    grid_spec = pltpu.PrefetchScalarGridSpec(
        num_scalar_prefetch=len(seg) + 1,
        grid=(n // tg,),
        in_specs=[
            pl.BlockSpec((tg, d), lambda i, *_: (i, 0)),
            pl.BlockSpec((None, 8, tg), lambda i, *_: (i, 0, 0)),
        ],
        out_specs=pl.BlockSpec(memory_space=pl.ANY),
        scratch_shapes=[pltpu.VMEM((2, MOE_SLOTS, d), BF16), pltpu.VMEM((MOE_TM, d), BF16),
                        pltpu.SemaphoreType.DMA((3,))],
    )
    return pl.pallas_call(
        _dispatch_kernel,
        grid_spec=grid_spec,
        out_shape=jax.ShapeDtypeStruct((rows, d), BF16),
        compiler_params=_params("arbitrary"),
        name="moe_dispatch",
    )(*seg, fill, h2, prow)


def _group_kernel(te_ref, nu_ref, x_ref, w1_ref, w3_ref, w2_ref, o_ref):
    del te_ref
    live = pl.program_id(0) < nu_ref[0]

    @pl.when(live)
    def _():
        o_ref[...] = _swiglu(x_ref[...], w1_ref[...], w3_ref[...], w2_ref[...]).astype(BF16)

    @pl.when(jnp.logical_not(live))
    def _():
        o_ref[...] = jnp.zeros_like(o_ref)


def _group(tile_expert, n_used, xs, w1, w3, w2, tm):
    rows, d = xs.shape
    f = w1.shape[2]
    used = lambda r, te, nu: (jnp.minimum(r, nu[0] - 1), 0)
    grid_spec = pltpu.PrefetchScalarGridSpec(
        num_scalar_prefetch=2,
        grid=(rows // tm,),
        in_specs=[
            pl.BlockSpec((tm, d), used),
            pl.BlockSpec((None, d, f), lambda r, te, nu: (te[r], 0, 0)),
            pl.BlockSpec((None, d, f), lambda r, te, nu: (te[r], 0, 0)),
            pl.BlockSpec((None, f, d), lambda r, te, nu: (te[r], 0, 0)),
        ],
        out_specs=pl.BlockSpec((tm, d), lambda r, te, nu: (r, 0)),
    )
    return pl.pallas_call(
        _group_kernel,
        grid_spec=grid_spec,
        out_shape=jax.ShapeDtypeStruct((rows, d), BF16),
        compiler_params=_params("arbitrary"),
        name="moe_group",
    )(tile_expert, n_used, xs, w1, w3, w2)


def _combine_kernel(*refs, final):
    if final:
        lb_ref, gb_ref, nb_ref, ns_ref, x_ref, pcol_ref, mod_ref, fg_ref, y_ref, o_ref, ybuf, sems = refs
    else:
        lb_ref, gb_ref, nb_ref, ns_ref, x_ref, pcol_ref, mod_ref, y_ref, o_ref, ybuf, sems = refs
    seg = (lb_ref, gb_ref, nb_ref, ns_ref)
    i = pl.program_id(0)
    nt = pl.num_programs(0)
    slot = i % 2
    tg = x_ref.shape[0]

    @pl.when(i == 0)
    def _():
        ybuf[...] = jnp.zeros_like(ybuf)
        _segment_copies(i, seg, ybuf.at[slot], y_ref, sems.at[slot], False, "start")

    @pl.when(i + 1 < nt)
    def _():
        _segment_copies(i + 1, seg, ybuf.at[1 - slot], y_ref, sems.at[1 - slot], False, "start")

    s = lax.broadcasted_iota(jnp.int32, (tg, MOE_SLOTS), 1).astype(F32)
    pc = pcol_ref[...]
    scatter = jnp.where(s == pc[:, 0:1], pc[:, 2:3], jnp.where(s == pc[:, 1:2], pc[:, 3:4], 0.0)).astype(BF16)
    _segment_copies(i, seg, ybuf.at[slot], y_ref, sems.at[slot], False, "wait")
    y = jnp.dot(scatter, ybuf[slot], preferred_element_type=F32)
    xn = x_ref[...] + mod_ref[5:6, :] * y
    if final:
        xn = _rms(xn) * fg_ref[...]
    o_ref[...] = xn


def _combine(seg, x2, pcol, mod, mod_row, final_g, ys, tg):
    n, d = x2.shape
    final = final_g is not None
    in_specs = [
        pl.BlockSpec((tg, d), lambda i, *_: (i, 0)),
        pl.BlockSpec((tg, LANES), lambda i, *_: (i, 0)),
        pl.BlockSpec((None, 6, d), lambda i, *_: (mod_row(i), 0, 0)),
    ]
    args = [x2, pcol, mod]
    if final:
        in_specs.append(pl.BlockSpec((1, d), lambda i, *_: (0, 0)))
        args.append(final_g)
    in_specs.append(pl.BlockSpec(memory_space=pl.ANY))
    args.append(ys)
    grid_spec = pltpu.PrefetchScalarGridSpec(
        num_scalar_prefetch=len(seg),
        grid=(n // tg,),
        in_specs=in_specs,
        out_specs=pl.BlockSpec((tg, d), lambda i, *_: (i, 0)),
        scratch_shapes=[pltpu.VMEM((2, MOE_SLOTS, d), BF16), pltpu.SemaphoreType.DMA((2,))],
    )
    return pl.pallas_call(
        functools.partial(_combine_kernel, final=final),
        grid_spec=grid_spec,
        out_shape=jax.ShapeDtypeStruct((n, d), F32),
        compiler_params=_params("arbitrary"),
        name="moe_combine",
    )(*seg, *args)


def _routed_moe(x, mod, batch_row, w1, w3, w2, h2, plan, final_g):
    b, l, d = x.shape
    n = b * l
    tg, tm = MOE_TG, MOE_TM
    nt = n // tg
    x2, h22 = x.reshape(n, d), h2.reshape(n, d)
    mod_row = lambda i: batch_row((i * tg) // l)
    prow, pcol, cnt = plan
    pcol = pcol.reshape(n, LANES)

    cnt = cnt[:, :, 0].astype(jnp.int32)
    cpad = (cnt + PIECE - 1) // PIECE * PIECE
    lbase = jnp.cumsum(cpad, axis=1) - cpad
    tot = (jnp.sum(cpad, axis=0) + tm - 1) // tm * tm
    ends = jnp.cumsum(tot)
    gbase = (ends - tot)[None, :] + jnp.cumsum(cpad, axis=0) - cpad
    rows = (2 * n + nt * N_EXPERTS * (PIECE - 1) + N_EXPERTS * (tm - 1) + tm - 1) // tm * tm
    n_used = (ends[-1] // tm).astype(jnp.int32)
    tile_start = jnp.arange(rows // tm, dtype=jnp.int32) * tm
    tile_expert = jnp.sum(jnp.minimum(tile_start, ends[-1] - 1)[:, None] >= ends[None, :], axis=1).astype(jnp.int32)
    seg = tuple(a.reshape(-1).astype(jnp.int32)
                for a in (lbase, gbase, cpad // BIG_PIECE, cpad % BIG_PIECE // PIECE))
    data_end = ends - tot + jnp.sum(cpad, axis=0)
    fill = jnp.concatenate([data_end, (ends - data_end) // PIECE, ends[-1:], (rows - ends[-1:]) // tm]).astype(jnp.int32)

    xs = _dispatch(seg, fill, h22, prow, rows, tg)
    ys = _group(tile_expert, n_used.reshape(1), xs, w1, w3, w2, tm)
    out = _combine(seg, x2, pcol, mod, mod_row, final_g, ys, tg)
    return out.reshape(b, l, d)


def _rope_tables(l):
    rows = l // GRID_W
    row = jnp.repeat(jnp.arange(rows, dtype=F32), GRID_W)
    colp = jnp.tile(jnp.arange(GRID_W, dtype=F32), rows)
    inv = ROPE_THETA ** (-jnp.arange(ROPE_PAIRS, dtype=F32) * 2.0 / AXIS_DIM)
    ang_r, ang_c = row[:, None] * inv, colp[:, None] * inv
    zero = jnp.zeros_like(ang_r)
    cos64 = jnp.concatenate([jnp.cos(ang_r)] * 2 + [jnp.cos(ang_c)] * 2, axis=1)
    up64 = jnp.concatenate([-jnp.sin(ang_r), zero, -jnp.sin(ang_c), zero], axis=1)
    dn64 = jnp.concatenate([zero, jnp.sin(ang_r), zero, jnp.sin(ang_c)], axis=1)
    return tuple(jnp.tile(a, (1, LANES // HEAD_DIM)) for a in (cos64, up64, dn64))


def _lambda_init(layer):
    return 0.8 - 0.6 * math.exp(-0.3 * layer)


def _permute_in(w):
    off_q = POOL_WIDTH
    off_g = off_q + 3 * QKV_WIDTH
    return jnp.concatenate([w[:, off_g:], w[:, off_q:off_g], w[:, :off_q]], axis=1).astype(BF16)


def kernel(x, c, ctx, c_ctx, w_mod, b_mod, norm1_g, norm2_g, w_in, pool_w, pool_scale, lam_q1, lam_k1,
           lam_q2, lam_k2, subln_g, w_pool_proj, w_attn_proj, w_out, ffn_w1, ffn_w3, ffn_w2, router_w,
           moe_w1, moe_w3, moe_w2, final_g):
    b, l, d = x.shape
    n_ctx = ctx.shape[1]
    depth = w_mod.shape[0]
    assert d == D_MODEL and b + 1 <= MOD_ROWS and l % 512 == 0 and n_ctx % 256 == 0

    s_in = jnp.concatenate([c, c_ctx[None, :], jnp.zeros((MOD_ROWS - b - 1, d), F32)], axis=0)
    mod_all = _modulation(s_in, w_mod, b_mod)
    tables = _rope_tables(l)
    lat_row = lambda bi: bi
    ctx_row = lambda bi: b
    gq, gk, gp = COL_Q // V_DIM, COL_K // V_DIM, COL_P // POOL_WIDTH
    full_kinds = ("g",) * 4 + ("q",) * 2 + ("k",) * 2 + ("v",) * 2 + ("p",)

    xc = ctx
    for layer in range(depth):
        last = layer == depth - 1
        lam0 = _lambda_init(layer)
        mod = mod_all[layer].reshape(MOD_ROWS, 6, d)
        g1 = norm1_g[layer][None, :]
        g2 = norm2_g[layer][None, :]
        w_in_p = _permute_in(w_in[layer])
        lam_p = jnp.stack([lam_q1[layer], lam_k1[layer], lam_q2[layer], lam_k2[layer]])
        g_col = subln_g[layer][:, None]
        pw = pool_w[layer].astype(BF16)
        ps = pool_scale[layer][None, :]
        wp = w_pool_proj[layer].astype(BF16)
        wa = w_attn_proj[layer].astype(BF16)
        wo = w_out[layer].astype(BF16)
        fg = final_g[None, :] if last else None
        moe_layer = layer % 2 == 1
        assert last or not moe_layer, "context tokens are only carried through dense layers"
        if moe_layer:
            rw = jnp.pad(router_w[layer // 2].T, ((0, ROUTER_ROWS - N_EXPERTS), (0, 0)))
            rw_hi = rw.astype(BF16)
            routers = (rw_hi, (rw - rw_hi.astype(F32)).astype(BF16))
            we1 = moe_w1[layer // 2].astype(BF16)
            we3 = moe_w3[layer // 2].astype(BF16)
            we2 = moe_w2[layer // 2].astype(BF16)
        else:
            routers = None
            wf1 = ffn_w1[layer // 2].astype(BF16)
            wf3 = ffn_w3[layer // 2].astype(BF16)
            wf2 = ffn_w2[layer // 2].astype(BF16)

        z, vt = _inproj(x, mod, lat_row, g1, w_in_p, full_kinds, tables, 512)
        if last:
            zc, vtc = _inproj(xc, mod, ctx_row, g1, w_in_p[:, COL_K:COL_V + QKV_WIDTH], ("k",) * 2 + ("v",) * 2,
                              None, n_ctx)
            ckb = 0
        else:
            zc, vtc = _inproj(xc, mod, ctx_row, g1, w_in_p, full_kinds, None, n_ctx)
            ckb = gk
        attn_y = _attention(lam_p, g_col, z, gq, zc, ckb, vtc, z, gk, vt, lam0, 512, 1)
        pool_y = _pool(z, gp, pw, ps)
        x, h2, *plan = _merge(x, mod, lat_row, g2, z, pool_y, attn_y, wp, wa, wo, routers, 512)

        if not last:
            attn_yc = _attention(lam_p, g_col, zc, COL_Q // QKV_WIDTH, zc, COL_K // QKV_WIDTH, vtc, None, 0, None,
                                 lam0, 512, N_HEADS)
            pool_yc = _pool(zc, gp, pw, ps)
            xc, h2c = _merge(xc, mod, ctx_row, g2, zc, pool_yc, attn_yc, wp, wa, wo, None, n_ctx)

        if moe_layer:
            x = _routed_moe(x, mod, lat_row, we1, we3, we2, h2, plan, fg)
        else:
            x = _ffn(x, h2, mod, lat_row, wf1, wf3, wf2, fg, 512)
        if not last:
            xc = _ffn(xc, h2c, mod, ctx_row, wf1, wf3, wf2, fg, n_ctx)
    return x
```

```python
import functools
import math

import jax
import jax.numpy as jnp
from jax import lax
from jax.experimental import pallas as pl
from jax.experimental.pallas import tpu as pltpu

F32 = jnp.float32
BF16 = jnp.bfloat16

D_MODEL = 1024
EPS = 1e-6
GRID_W = 64
N_HEADS = 8
HEAD_DIM = 64
V_DIM = 2 * HEAD_DIM
ROPE_THETA = 10000.0
AXIS_DIM = HEAD_DIM // 2
ROPE_PAIRS = AXIS_DIM // 2
POOL_WINDOWS = (2, 4, 8, 16)
POOL_WIDTH = 512
POOL_GROUP_DIM = POOL_WIDTH // len(POOL_WINDOWS)
N_EXPERTS = 8
Q_SCALE = HEAD_DIM ** -0.5
LOG2E = math.log2(math.e)
MAX_UNSHIFTED_LOG2 = 80.0
SQ_NORM_SLACK = 1.05

QKV_WIDTH = N_HEADS * V_DIM
COL_G = 0
COL_Q = COL_G + 2 * D_MODEL
COL_K = COL_Q + QKV_WIDTH
COL_V = COL_K + QKV_WIDTH
IN_WIDTH = COL_V + QKV_WIDTH + POOL_WIDTH
COL_P = COL_V
IN_CHUNK = 512

LANES = 128
MOD_ROWS = 40
ATTN_HEADS = 2

VMEM_LIMIT = 56 * 1024 * 1024


def _resident(shape):
    nd = len(shape)
    return pl.BlockSpec(shape, lambda *_: (0,) * nd, pipeline_mode=pl.Buffered(1))


def _params(*sem):
    return pltpu.CompilerParams(dimension_semantics=sem, vmem_limit_bytes=VMEM_LIMIT)


def _sigmoid(v):
    return 1.0 / (1.0 + jnp.exp(-v))


def _rms(v):
    return v * lax.rsqrt(jnp.mean(v * v, axis=-1, keepdims=True) + EPS)


def _mod_kernel(s_ref, w_ref, b_ref, o_ref):
    s = s_ref[...]
    s = s * _sigmoid(s)
    w = w_ref[...]
    s_hi = s.astype(BF16)
    s_lo = (s - s_hi.astype(F32)).astype(BF16)
    w_hi = w.astype(BF16)
    w_lo = (w - w_hi.astype(F32)).astype(BF16)
    acc = jnp.dot(s_hi, w_hi, preferred_element_type=F32)
    acc += jnp.dot(s_hi, w_lo, preferred_element_type=F32)
    acc += jnp.dot(s_lo, w_hi, preferred_element_type=F32)
    o_ref[...] = acc + b_ref[...]


def _modulation(s_in, w_mod, b_mod):
    depth, d, n = w_mod.shape
    tn = 1024
    return pl.pallas_call(
        _mod_kernel,
        grid=(depth, n // tn),
        in_specs=[
            pl.BlockSpec((MOD_ROWS, d), lambda l, j: (0, 0)),
            pl.BlockSpec((None, d, tn), lambda l, j: (l, 0, j)),
            pl.BlockSpec((None, 1, tn), lambda l, j: (l, 0, j)),
        ],
        out_specs=pl.BlockSpec((None, MOD_ROWS, tn), lambda l, j: (l, 0, j)),
        out_shape=jax.ShapeDtypeStruct((depth, MOD_ROWS, n), F32),
        compiler_params=_params("parallel", "parallel"),
        name="modulation",
    )(s_in, w_mod, b_mod.reshape(depth, 1, n))


def _rope(z, c, s_up, s_dn):
    up = pltpu.roll(z, LANES - ROPE_PAIRS, 1)
    dn = pltpu.roll(z, ROPE_PAIRS, 1)
    return z * c + up * s_up + dn * s_dn


def _inproj_kernel(*refs, kinds, rope):
    if rope:
        x_ref, mod_ref, g_ref, w_ref, c_ref, su_ref, sd_ref, z_ref, vt_ref = refs
    else:
        x_ref, mod_ref, g_ref, w_ref, z_ref, vt_ref = refs
    y = _rms(x_ref[...]) * g_ref[...]
    h = (y * (1.0 + mod_ref[1:2, :]) + mod_ref[0:1, :]).astype(BF16)
    z_col, head = 0, 0
    for ci, kind in enumerate(kinds):
        lo = ci * IN_CHUNK
        z = jnp.dot(h, w_ref[:, lo:lo + IN_CHUNK], preferred_element_type=F32)
        if kind == "v":
            for j in range(0, IN_CHUNK, V_DIM):
                vt_ref[head * V_DIM:(head + 1) * V_DIM, :] = z[:, j:j + V_DIM].T.astype(BF16)
                head += 1
            continue
        if rope and kind in ("q", "k"):
            c, su, sd = c_ref[...], su_ref[...], sd_ref[...]
            z = jnp.concatenate(
                [_rope(z[:, j:j + LANES], c, su, sd) for j in range(0, IN_CHUNK, LANES)], axis=1)
        if kind == "q":
            z = z * (Q_SCALE * LOG2E)
        z_ref[:, z_col:z_col + IN_CHUNK] = z.astype(BF16)
        z_col += IN_CHUNK


def _inproj(x, mod, mod_row, g, w, kinds, tables, t):
    b, l, d = x.shape
    n_v = sum(k == "v" for k in kinds)
    wz = IN_CHUNK * (len(kinds) - n_v)
    rows_vt = n_v * IN_CHUNK
    rope = tables is not None
    in_specs = [
        pl.BlockSpec((None, t, d), lambda bi, i: (bi, i, 0)),
        pl.BlockSpec((None, 6, d), lambda bi, i: (mod_row(bi), 0, 0)),
        pl.BlockSpec((1, d), lambda bi, i: (0, 0)),
        _resident((d, IN_CHUNK * len(kinds))),
    ]
    args = [x, mod, g, w]
    if rope:
        in_specs += [pl.BlockSpec((t, LANES), lambda bi, i: (i, 0))] * 3
        args += list(tables)
    return pl.pallas_call(
        functools.partial(_inproj_kernel, kinds=kinds, rope=rope),
        grid=(b, l // t),
        in_specs=in_specs,
        out_specs=[pl.BlockSpec((None, t, wz), lambda bi, i: (bi, i, 0)),
                   pl.BlockSpec((None, rows_vt, t), lambda bi, i: (bi, 0, i))],
        out_shape=[jax.ShapeDtypeStruct((b, l, wz), BF16), jax.ShapeDtypeStruct((b, rows_vt, l), BF16)],
        compiler_params=_params("parallel", "parallel"),
        name="inproj",
    )(*args)


def _pool_kernel(u_ref, pw_ref, ps_ref, o_ref):
    l = u_ref.shape[0]
    t = lax.broadcasted_iota(jnp.int32, (l, POOL_GROUP_DIM), 0)

    def shifted(a, k):
        if k > 0:
            return jnp.where(t >= k, pltpu.roll(a, k, 0), 0.0)
        return jnp.where(t < l + k, pltpu.roll(a, l + k, 0), 0.0)

    for gi, w in enumerate(POOL_WINDOWS):
        lo = gi * POOL_GROUP_DIM
        u = u_ref[:, lo:lo + POOL_GROUP_DIM].astype(F32)
        back, fwd, span = u, u, 1
        while span < w // 2:
            back = back + shifted(back, span)
            fwd = fwd + shifted(fwd, -span)
            span *= 2
        win = shifted(back, 1) + fwd
        cnt = (jnp.minimum(t + w // 2, l) - jnp.maximum(t - w // 2, 0)).astype(F32)
        m = (win / cnt - u).astype(BF16)
        y = jnp.dot(m, pw_ref[gi], preferred_element_type=F32)
        o_ref[:, lo:lo + POOL_GROUP_DIM] = (y * ps_ref[:, lo:lo + POOL_GROUP_DIM]).astype(BF16)


def _pool(z, col_block, pool_w, pool_scale):
    b, l, _ = z.shape
    return pl.pallas_call(
        _pool_kernel,
        grid=(b,),
        in_specs=[
            pl.BlockSpec((None, l, POOL_WIDTH), lambda bi: (bi, 0, col_block)),
            _resident(pool_w.shape),
            pl.BlockSpec((1, POOL_WIDTH), lambda bi: (0, 0)),
        ],
        out_specs=pl.BlockSpec((None, l, POOL_WIDTH), lambda bi: (bi, 0, 0)),
        out_shape=jax.ShapeDtypeStruct((b, l, POOL_WIDTH), BF16),
        compiler_params=_params("parallel"),
        name="pool",
    )(z, pool_w, pool_scale)


def _attn_kernel(*refs, n_lat, tq, lam0, heads):
    if n_lat:
        lam_ref, g_ref, q_ref, kc_ref, vtc_ref, kl_ref, vtl_ref, o_ref, p_sc = refs
    else:
        lam_ref, g_ref, q_ref, kc_ref, vtc_ref, o_ref, p_sc = refs
    lp = lam_ref[...]
    lam = (jnp.exp(jnp.sum(lp[0:1] * lp[1:2], axis=1, keepdims=True))
           - jnp.exp(jnp.sum(lp[2:3] * lp[3:4], axis=1, keepdims=True)) + lam0)
    nq = q_ref.shape[0]
    lane = lax.broadcasted_iota(jnp.int32, (tq, V_DIM), 1)
    half = lax.broadcasted_iota(jnp.int32, (8, V_DIM), 1) // HEAD_DIM
    sel = (half == lax.broadcasted_iota(jnp.int32, (8, V_DIM), 0)).astype(BF16)
    nt = (((1,), (1,)), ((), ()))

    def sq_norm_max(a):
        return jnp.max(lax.dot_general(sel, a * a, nt, preferred_element_type=F32), axis=1, keepdims=True)

    def head_groups(hh):
        cs = slice(hh * V_DIM, (hh + 1) * V_DIM)
        groups = [(kc_ref[:, cs], vtc_ref[cs, :])]
        if n_lat:
            groups.append((kl_ref[:, cs], vtl_ref[cs, :]))
        return cs, groups

    bound = None
    for hh in range(heads):
        cs, groups = head_groups(hh)
        k_sq = sq_norm_max(groups[0][0])
        for k, _ in groups[1:]:
            k_sq = jnp.maximum(k_sq, sq_norm_max(k))
        b_h = sq_norm_max(q_ref[:, cs]) * k_sq
        bound = b_h if bound is None else jnp.maximum(bound, b_h)
    small = jnp.max(bound) * SQ_NORM_SLACK < MAX_UNSHIFTED_LOG2 ** 2

    def scores(shift, groups, cs, c0, slot):
        q = q_ref[c0:c0 + tq, cs]
        sums = []
        for mi, first in enumerate((True, False)):
            qm = jnp.where((lane < HEAD_DIM) if first else (lane >= HEAD_DIM), q, jnp.zeros_like(q))
            s_t = [lax.dot_general(k, qm, nt, preferred_element_type=F32) for k, _ in groups]
            if shift:
                m = jnp.max(s_t[0], axis=0, keepdims=True)
                for s_g in s_t[1:]:
                    m = jnp.maximum(m, jnp.max(s_g, axis=0, keepdims=True))
                s_t = [s_g - m for s_g in s_t]
            l_m, row = None, 0
            for s_g in s_t:
                p_g = jnp.exp2(s_g)
                l_g = jnp.sum(p_g, axis=0, keepdims=True)
                l_m = l_g if l_m is None else l_m + l_g
                p_sc[slot, mi, row:row + p_g.shape[0], :] = p_g.astype(BF16)
                row += p_g.shape[0]
            sums.append(l_m)
        return sums

    def values(groups, cs, c0, slot, sums):
        r = (lam * sums[0] * (1.0 / sums[1])).astype(BF16)
        o_t, row = None, 0
        for k, vt in groups:
            rows = slice(row, row + k.shape[0])
            part = jnp.dot(vt, p_sc[slot, 0, rows, :] - r * p_sc[slot, 1, rows, :], preferred_element_type=F32)
            o_t = part if o_t is None else o_t + part
            row += k.shape[0]
        o_t = o_t * (1.0 / sums[0])
        o_t = o_t * lax.rsqrt(jnp.mean(o_t * o_t, axis=0, keepdims=True) + EPS)
        o_t = o_t * (g_ref[...] * (1.0 - lam0))
        o_ref[c0:c0 + tq, cs] = o_t.T.astype(BF16)

    def attend(shift):
        work = [(hh, c0) for hh in range(heads) for c0 in range(0, nq, tq)]
        pending = None
        for i, (hh, c0) in enumerate(work):
            cs, groups = head_groups(hh)
            sums = scores(shift, groups, cs, c0, i % 2)
            if pending is not None:
                values(*pending)
            pending = (groups, cs, c0, i % 2, sums)
        values(*pending)

    lax.cond(small, lambda: attend(False), lambda: attend(True))


def _attention(lam_p, g_col, zq, q_blk, zc, kc_blk, vtc, zl, kl_blk, vtl, lam0, tq, heads):
    b, nq, _ = zq.shape
    n_ctx = zc.shape[1]
    n_lat = 0 if zl is None else zl.shape[1]
    w = heads * V_DIM

    def col(blk, n):
        return pl.BlockSpec((None, n, w), lambda bi, h: (bi, 0, blk + h))

    def vt_rows(n):
        return pl.BlockSpec((None, w, n), lambda bi, h: (bi, h, 0))

    in_specs = [
        pl.BlockSpec(lam_p.shape, lambda bi, h: (0, 0)),
        pl.BlockSpec(g_col.shape, lambda bi, h: (0, 0)),
        col(q_blk, nq), col(kc_blk, n_ctx), vt_rows(n_ctx),
    ]
    args = [lam_p, g_col, zq, zc, vtc]
    if n_lat:
        in_specs += [col(kl_blk, n_lat), vt_rows(n_lat)]
        args += [zl, vtl]
    tq = min(tq, nq)
    return pl.pallas_call(
        functools.partial(_attn_kernel, n_lat=n_lat, tq=tq, lam0=lam0, heads=heads),
        grid=(b, N_HEADS // heads),
        in_specs=in_specs,
        out_specs=pl.BlockSpec((None, nq, w), lambda bi, h: (bi, 0, h)),
        out_shape=jax.ShapeDtypeStruct((b, nq, QKV_WIDTH), BF16),
        scratch_shapes=[pltpu.VMEM((2, 2, n_ctx + n_lat, tq), BF16)],
        compiler_params=_params("parallel", "parallel"),
        name="attention",
    )(*args)


def _merge_kernel(*refs, plan):
    if plan:
        (x_ref, mod_ref, g2_ref, zg_ref, py_ref, ay_ref, wp_ref, wa_ref, wo_ref, rh_ref, rl_ref,
         xo_ref, h2_ref, prow_ref, pcol_ref, cnt_ref) = refs
    else:
        x_ref, mod_ref, g2_ref, zg_ref, py_ref, ay_ref, wp_ref, wa_ref, wo_ref, xo_ref, h2_ref = refs
    d = x_ref.shape[1]
    g_pool = _sigmoid(zg_ref[:, 0:d].astype(F32))
    g_attn = _sigmoid(zg_ref[:, d:2 * d].astype(F32))
    y = (g_pool * jnp.dot(py_ref[...], wp_ref[...], preferred_element_type=F32)
         + g_attn * jnp.dot(ay_ref[...], wa_ref[...], preferred_element_type=F32))
    o = jnp.dot(y.astype(BF16), wo_ref[...], preferred_element_type=F32)
    xn = x_ref[...] + mod_ref[2:3, :] * o
    xo_ref[...] = xn
    h2 = _rms(xn) * g2_ref[...] * (1.0 + mod_ref[4:5, :]) + mod_ref[3:4, :]
    h2_hi = h2.astype(BF16)
    h2_ref[...] = h2_hi
    if plan:
        h2_lo = (h2 - h2_hi.astype(F32)).astype(BF16)
        rows, cnt = _route_plan(h2_hi, h2_lo, rh_ref[...], rl_ref[...])
        prow_ref[...] = rows
        pcol_ref[...] = jnp.concatenate([rows, jnp.zeros((LANES - 8, rows.shape[1]), F32)], axis=0).T
        cnt_ref[...] = jnp.broadcast_to(cnt, (N_EXPERTS, LANES))


def _merge(x, mod, mod_row, g2, z, pool_y, attn_y, wp, wa, wo, router_w, t):
    b, l, d = x.shape
    plan = router_w is not None
    nt = l // t
    tile = lambda w: pl.BlockSpec((None, t, w), lambda bi, i: (bi, i, 0))
    in_specs = [
        tile(d),
        pl.BlockSpec((None, 6, d), lambda bi, i: (mod_row(bi), 0, 0)),
        pl.BlockSpec((1, d), lambda bi, i: (0, 0)),
        tile(2 * d),
        tile(POOL_WIDTH), tile(QKV_WIDTH),
        _resident(wp.shape), _resident(wa.shape), _resident(wo.shape),
    ]
    args = [x, mod, g2, z, pool_y, attn_y, wp, wa, wo]
    out_specs = [tile(d), tile(d)]
    out_shape = [jax.ShapeDtypeStruct((b, l, d), F32), jax.ShapeDtypeStruct((b, l, d), BF16)]
    if plan:
        assert t == MOE_TG
        in_specs += [_resident(router_w[0].shape)] * 2
        args += list(router_w)
        out_specs += [pl.BlockSpec((None, 8, t), lambda bi, i: (bi * nt + i, 0, 0)),
                      tile(LANES),
                      pl.BlockSpec((None, N_EXPERTS, LANES), lambda bi, i: (bi * nt + i, 0, 0))]
        out_shape += [jax.ShapeDtypeStruct((b * nt, 8, t), F32),
                      jax.ShapeDtypeStruct((b, l, LANES), F32),
                      jax.ShapeDtypeStruct((b * nt, N_EXPERTS, LANES), F32)]
    return pl.pallas_call(
        functools.partial(_merge_kernel, plan=plan),
        grid=(b, nt),
        in_specs=in_specs,
        out_specs=out_specs,
        out_shape=out_shape,
        compiler_params=_params("parallel", "parallel"),
        name="merge",
    )(*args)


def _swiglu(h, w1, w3, w2):
    a = jnp.dot(h, w1, preferred_element_type=F32)
    b = jnp.dot(h, w3, preferred_element_type=F32)
    return jnp.dot((a * _sigmoid(a) * b).astype(BF16), w2, preferred_element_type=F32)


def _ffn_kernel(*refs, final):
    if final:
        x_ref, h2_ref, mod_ref, w1_ref, w3_ref, w2_ref, fg_ref, o_ref = refs
    else:
        x_ref, h2_ref, mod_ref, w1_ref, w3_ref, w2_ref, o_ref = refs
    y = _swiglu(h2_ref[...], w1_ref[...], w3_ref[...], w2_ref[...])
    xn = x_ref[...] + mod_ref[5:6, :] * y
    if final:
        xn = _rms(xn) * fg_ref[...]
    o_ref[...] = xn


def _ffn(x, h2, mod, mod_row, w1, w3, w2, final_g, t):
    b, l, d = x.shape
    final = final_g is not None
    tile = pl.BlockSpec((None, t, d), lambda bi, i: (bi, i, 0))
    in_specs = [tile, tile, pl.BlockSpec((None, 6, d), lambda bi, i: (mod_row(bi), 0, 0)),
                _resident(w1.shape), _resident(w3.shape), _resident(w2.shape)]
    args = [x, h2, mod, w1, w3, w2]
    if final:
        in_specs.append(pl.BlockSpec((1, d), lambda bi, i: (0, 0)))
        args.append(final_g)
    return pl.pallas_call(
        functools.partial(_ffn_kernel, final=final),
        grid=(b, l // t),
        in_specs=in_specs,
        out_specs=tile,
        out_shape=jax.ShapeDtypeStruct((b, l, d), F32),
        compiler_params=_params("parallel", "parallel"),
        name="ffn",
    )(*args)


MOE_TG = 512
MOE_TM = 512
PIECE = 16
BIG_PIECE = 64
MOE_SLOTS = 2 * MOE_TG + N_EXPERTS * PIECE
ROUTER_ROWS = 16


def _route_plan(h2_hi, h2_lo, rh, rl):
    tg = h2_hi.shape[0]
    nt = (((1,), (1,)), ((), ()))
    logits = (lax.dot_general(rh, h2_hi, nt, preferred_element_type=F32)
              + lax.dot_general(rl, h2_hi, nt, preferred_element_type=F32)
              + lax.dot_general(rh, h2_lo, nt, preferred_element_type=F32))
    sub = lax.broadcasted_iota(jnp.int32, logits.shape, 0).astype(F32)
    neg = jnp.float32(-jnp.inf)
    logits = jnp.where(sub < N_EXPERTS, logits, neg)
    v1 = jnp.max(logits, axis=0, keepdims=True)
    i1 = jnp.min(jnp.where(logits == v1, sub, float(ROUTER_ROWS)), axis=0, keepdims=True)
    rest = jnp.where(sub == i1, neg, logits)
    v2 = jnp.max(rest, axis=0, keepdims=True)
    i2 = jnp.min(jnp.where(rest == v2, sub, float(ROUTER_ROWS)), axis=0, keepdims=True)
    e2 = jnp.exp(v2 - v1)
    w1 = 1.0 / (1.0 + e2)
    w2 = e2 * w1
    two = w2 != 0.0
    sel = ((sub == i1) | ((sub == i2) & two)).astype(F32)
    before = (lax.broadcasted_iota(jnp.int32, (tg, tg), 0)
              < lax.broadcasted_iota(jnp.int32, (tg, tg), 1)).astype(BF16)
    rank = jnp.dot(sel.astype(BF16), before, preferred_element_type=F32)
    cnt = jnp.sum(sel, axis=1, keepdims=True)
    cpad = jnp.ceil(cnt * (1.0 / PIECE)) * PIECE
    subc = sub[:, 0:1]
    lbase = jnp.zeros_like(cpad)
    for e in range(N_EXPERTS - 1):
        c_e = jnp.sum(jnp.where(subc == e, cpad, 0.0), axis=0, keepdims=True)
        lbase = lbase + jnp.where(subc > e, c_e, 0.0)
    slot = lbase + rank
    e_a = jnp.where(two, jnp.minimum(i1, i2), i1)
    e_b = jnp.where(two, jnp.maximum(i1, i2), i1)
    ls_a = jnp.sum(jnp.where(sub == e_a, slot, 0.0), axis=0, keepdims=True)
    ls_b = jnp.sum(jnp.where(sub == e_b, slot, 0.0), axis=0, keepdims=True)
    w_a = jnp.where(e_a == i1, w1, w2)
    w_b = jnp.where(two, jnp.where(e_b == i1, w1, w2), 0.0)
    rows = jnp.concatenate([ls_a, ls_b, w_a, w_b, jnp.zeros((4, tg), F32)], axis=0)
    return rows, cnt[0:N_EXPERTS]


def _segment_copies(i, seg, local_buf, sorted_hbm, sem, to_sorted, action):
    lb_ref, gb_ref, nb_ref, ns_ref = seg
    for e in range(N_EXPERTS):
        k = i * N_EXPERTS + e

        def copy(off, rows, k=k):
            lo = pl.multiple_of(lb_ref[k] + off, PIECE)
            go = pl.multiple_of(gb_ref[k] + off, PIECE)
            local = local_buf.at[pl.ds(lo, rows), :]
            remote = sorted_hbm.at[pl.ds(go, rows), :]
            c = pltpu.make_async_copy(local, remote, sem) if to_sorted else pltpu.make_async_copy(remote, local, sem)
            getattr(c, action)()

        def big(j, carry, copy=copy):
            copy(j * BIG_PIECE, BIG_PIECE)
            return carry

        def small(j, carry, copy=copy, k=k):
            copy(nb_ref[k] * BIG_PIECE + j * PIECE, PIECE)
            return carry

        lax.fori_loop(0, nb_ref[k], big, 0)
        lax.fori_loop(0, ns_ref[k], small, 0)


def _zero_fill(fill_ref, zbuf, out_ref, sem, action):
    def copy(go, rows):
        c = pltpu.make_async_copy(zbuf.at[pl.ds(0, rows), :], out_ref.at[pl.ds(go, rows), :], sem)
        getattr(c, action)()

    for e in range(N_EXPERTS):
        def tail(j, carry, e=e):
            copy(pl.multiple_of(fill_ref[e] + j * PIECE, PIECE), PIECE)
            return carry

        lax.fori_loop(0, fill_ref[N_EXPERTS + e], tail, 0)

    def rest(j, carry):
        copy(pl.multiple_of(fill_ref[2 * N_EXPERTS] + j * MOE_TM, MOE_TM), MOE_TM)
        return carry

    lax.fori_loop(0, fill_ref[2 * N_EXPERTS + 1], rest, 0)


def _dispatch_kernel(lb_ref, gb_ref, nb_ref, ns_ref, fill_ref, h2_ref, prow_ref, out_ref, cbuf, zbuf, sems):
    seg = (lb_ref, gb_ref, nb_ref, ns_ref)
    i = pl.program_id(0)
    nt = pl.num_programs(0)
    slot = i % 2
    tg = h2_ref.shape[0]

    @pl.when(i >= 2)
    def _():
        _segment_copies(i - 2, seg, cbuf.at[slot], out_ref, sems.at[slot], True, "wait")

    s = lax.broadcasted_iota(jnp.int32, (MOE_SLOTS, tg), 0).astype(F32)
    onehot = jnp.where(s == prow_ref[0:1, :], 1.0, jnp.where(s == prow_ref[1:2, :], 1.0, 0.0)).astype(BF16)
    cbuf[slot] = jnp.dot(onehot, h2_ref[...], preferred_element_type=F32).astype(BF16)
    _segment_copies(i, seg, cbuf.at[slot], out_ref, sems.at[slot], True, "start")

    @pl.when(i == nt - 1)
    def _():
        @pl.when(i >= 1)
        def _():
            _segment_copies(i - 1, seg, cbuf.at[1 - slot], out_ref, sems.at[1 - slot], True, "wait")

        _segment_copies(i, seg, cbuf.at[slot], out_ref, sems.at[slot], True, "wait")
        zbuf[...] = jnp.zeros_like(zbuf)
        _zero_fill(fill_ref, zbuf, out_ref, sems.at[2], "start")
        _zero_fill(fill_ref, zbuf, out_ref, sems.at[2], "wait")


def _dispatch(seg, fill, h2, prow, rows, tg):
    n, d = h2.shape
    grid_spec = pltpu.PrefetchScalarGridSpec(
        num_scalar_prefetch=len(seg) + 1,
        grid=(n // tg,),
        in_specs=[
            pl.BlockSpec((tg, d), lambda i, *_: (i, 0)),
            pl.BlockSpec((None, 8, tg), lambda i, *_: (i, 0, 0)),
        ],
        out_specs=pl.BlockSpec(memory_space=pl.ANY),
        scratch_shapes=[pltpu.VMEM((2, MOE_SLOTS, d), BF16), pltpu.VMEM((MOE_TM, d), BF16),
                        pltpu.SemaphoreType.DMA((3,))],
    )
    return pl.pallas_call(
        _dispatch_kernel,
        grid_spec=grid_spec,
        out_shape=jax.ShapeDtypeStruct((rows, d), BF16),
        compiler_params=_params("arbitrary"),
        name="moe_dispatch",
    )(*seg, fill, h2, prow)


def _group_kernel(te_ref, nu_ref, x_ref, w1_ref, w3_ref, w2_ref, o_ref):
    del te_ref
    live = pl.program_id(0) < nu_ref[0]

    @pl.when(live)
    def _():
        o_ref[...] = _swiglu(x_ref[...], w1_ref[...], w3_ref[...], w2_ref[...]).astype(BF16)

    @pl.when(jnp.logical_not(live))
    def _():
        o_ref[...] = jnp.zeros_like(o_ref)


def _group(tile_expert, n_used, xs, w1, w3, w2, tm):
    rows, d = xs.shape
    f = w1.shape[2]
    used = lambda r, te, nu: (jnp.minimum(r, nu[0] - 1), 0)
    grid_spec = pltpu.PrefetchScalarGridSpec(
        num_scalar_prefetch=2,
        grid=(rows // tm,),
        in_specs=[
            pl.BlockSpec((tm, d), used),
            pl.BlockSpec((None, d, f), lambda r, te, nu: (te[r], 0, 0)),
            pl.BlockSpec((None, d, f), lambda r, te, nu: (te[r], 0, 0)),
            pl.BlockSpec((None, f, d), lambda r, te, nu: (te[r], 0, 0)),
        ],
        out_specs=pl.BlockSpec((tm, d), lambda r, te, nu: (r, 0)),
    )
    return pl.pallas_call(
        _group_kernel,
        grid_spec=grid_spec,
        out_shape=jax.ShapeDtypeStruct((rows, d), BF16),
        compiler_params=_params("arbitrary"),
        name="moe_group",
    )(tile_expert, n_used, xs, w1, w3, w2)


def _combine_kernel(*refs, final):
    if final:
        lb_ref, gb_ref, nb_ref, ns_ref, x_ref, pcol_ref, mod_ref, fg_ref, y_ref, o_ref, ybuf, sems = refs
    else:
        lb_ref, gb_ref, nb_ref, ns_ref, x_ref, pcol_ref, mod_ref, y_ref, o_ref, ybuf, sems = refs
    seg = (lb_ref, gb_ref, nb_ref, ns_ref)
    i = pl.program_id(0)
    nt = pl.num_programs(0)
    slot = i % 2
    tg = x_ref.shape[0]

    @pl.when(i == 0)
    def _():
        ybuf[...] = jnp.zeros_like(ybuf)
        _segment_copies(i, seg, ybuf.at[slot], y_ref, sems.at[slot], False, "start")

    @pl.when(i + 1 < nt)
    def _():
        _segment_copies(i + 1, seg, ybuf.at[1 - slot], y_ref, sems.at[1 - slot], False, "start")

    s = lax.broadcasted_iota(jnp.int32, (tg, MOE_SLOTS), 1).astype(F32)
    pc = pcol_ref[...]
    scatter = jnp.where(s == pc[:, 0:1], pc[:, 2:3], jnp.where(s == pc[:, 1:2], pc[:, 3:4], 0.0)).astype(BF16)
    _segment_copies(i, seg, ybuf.at[slot], y_ref, sems.at[slot], False, "wait")
    y = jnp.dot(scatter, ybuf[slot], preferred_element_type=F32)
    xn = x_ref[...] + mod_ref[5:6, :] * y
    if final:
        xn = _rms(xn) * fg_ref[...]
    o_ref[...] = xn


def _combine(seg, x2, pcol, mod, mod_row, final_g, ys, tg):
    n, d = x2.shape
    final = final_g is not None
    in_specs = [
        pl.BlockSpec((tg, d), lambda i, *_: (i, 0)),
        pl.BlockSpec((tg, LANES), lambda i, *_: (i, 0)),
        pl.BlockSpec((None, 6, d), lambda i, *_: (mod_row(i), 0, 0)),
    ]
    args = [x2, pcol, mod]
    if final:
        in_specs.append(pl.BlockSpec((1, d), lambda i, *_: (0, 0)))
        args.append(final_g)
    in_specs.append(pl.BlockSpec(memory_space=pl.ANY))
    args.append(ys)
    grid_spec = pltpu.PrefetchScalarGridSpec(
        num_scalar_prefetch=len(seg),
        grid=(n // tg,),
        in_specs=in_specs,
        out_specs=pl.BlockSpec((tg, d), lambda i, *_: (i, 0)),
        scratch_shapes=[pltpu.VMEM((2, MOE_SLOTS, d), BF16), pltpu.SemaphoreType.DMA((2,))],
    )
    return pl.pallas_call(
        functools.partial(_combine_kernel, final=final),
        grid_spec=grid_spec,
        out_shape=jax.ShapeDtypeStruct((n, d), F32),
        compiler_params=_params("arbitrary"),
        name="moe_combine",
    )(*seg, *args)


def _routed_moe(x, mod, batch_row, w1, w3, w2, h2, plan, final_g):
    b, l, d = x.shape
    n = b * l
    tg, tm = MOE_TG, MOE_TM
    nt = n // tg
    x2, h22 = x.reshape(n, d), h2.reshape(n, d)
    mod_row = lambda i: batch_row((i * tg) // l)
    prow, pcol, cnt = plan
    pcol = pcol.reshape(n, LANES)

    cnt = cnt[:, :, 0].astype(jnp.int32)
    cpad = (cnt + PIECE - 1) // PIECE * PIECE
    lbase = jnp.cumsum(cpad, axis=1) - cpad
    tot = (jnp.sum(cpad, axis=0) + tm - 1) // tm * tm
    ends = jnp.cumsum(tot)
    gbase = (ends - tot)[None, :] + jnp.cumsum(cpad, axis=0) - cpad
    rows = (2 * n + nt * N_EXPERTS * (PIECE - 1) + N_EXPERTS * (tm - 1) + tm - 1) // tm * tm
    n_used = (ends[-1] // tm).astype(jnp.int32)
    tile_start = jnp.arange(rows // tm, dtype=jnp.int32) * tm
    tile_expert = jnp.sum(jnp.minimum(tile_start, ends[-1] - 1)[:, None] >= ends[None, :], axis=1).astype(jnp.int32)
    seg = tuple(a.reshape(-1).astype(jnp.int32)
                for a in (lbase, gbase, cpad // BIG_PIECE, cpad % BIG_PIECE // PIECE))
    data_end = ends - tot + jnp.sum(cpad, axis=0)
    fill = jnp.concatenate([data_end, (ends - data_end) // PIECE, ends[-1:], (rows - ends[-1:]) // tm]).astype(jnp.int32)

    xs = _dispatch(seg, fill, h22, prow, rows, tg)
    ys = _group(tile_expert, n_used.reshape(1), xs, w1, w3, w2, tm)
    out = _combine(seg, x2, pcol, mod, mod_row, final_g, ys, tg)
    return out.reshape(b, l, d)


def _rope_tables(l):
    rows = l // GRID_W
    row = jnp.repeat(jnp.arange(rows, dtype=F32), GRID_W)
    colp = jnp.tile(jnp.arange(GRID_W, dtype=F32), rows)
    inv = ROPE_THETA ** (-jnp.arange(ROPE_PAIRS, dtype=F32) * 2.0 / AXIS_DIM)
    ang_r, ang_c = row[:, None] * inv, colp[:, None] * inv
    zero = jnp.zeros_like(ang_r)
    cos64 = jnp.concatenate([jnp.cos(ang_r)] * 2 + [jnp.cos(ang_c)] * 2, axis=1)
    up64 = jnp.concatenate([-jnp.sin(ang_r), zero, -jnp.sin(ang_c), zero], axis=1)
    dn64 = jnp.concatenate([zero, jnp.sin(ang_r), zero, jnp.sin(ang_c)], axis=1)
    return tuple(jnp.tile(a, (1, LANES // HEAD_DIM)) for a in (cos64, up64, dn64))


def _lambda_init(layer):
    return 0.8 - 0.6 * math.exp(-0.3 * layer)


def _permute_in(w):
    off_q = POOL_WIDTH
    off_g = off_q + 3 * QKV_WIDTH
    return jnp.concatenate([w[:, off_g:], w[:, off_q:off_g], w[:, :off_q]], axis=1).astype(BF16)


def kernel(x, c, ctx, c_ctx, w_mod, b_mod, norm1_g, norm2_g, w_in, pool_w, pool_scale, lam_q1, lam_k1,
           lam_q2, lam_k2, subln_g, w_pool_proj, w_attn_proj, w_out, ffn_w1, ffn_w3, ffn_w2, router_w,
           moe_w1, moe_w3, moe_w2, final_g):
    b, l, d = x.shape
    n_ctx = ctx.shape[1]
    depth = w_mod.shape[0]
    assert d == D_MODEL and b + 1 <= MOD_ROWS and l % 512 == 0 and n_ctx % 256 == 0

    s_in = jnp.concatenate([c, c_ctx[None, :], jnp.zeros((MOD_ROWS - b - 1, d), F32)], axis=0)
    mod_all = _modulation(s_in, w_mod, b_mod)
    tables = _rope_tables(l)
    lat_row = lambda bi: bi
    ctx_row = lambda bi: b
    gq, gk, gp = COL_Q // V_DIM, COL_K // V_DIM, COL_P // POOL_WIDTH
    full_kinds = ("g",) * 4 + ("q",) * 2 + ("k",) * 2 + ("v",) * 2 + ("p",)

    xc = ctx
    for layer in range(depth):
        last = layer == depth - 1
        lam0 = _lambda_init(layer)
        mod = mod_all[layer].reshape(MOD_ROWS, 6, d)
        g1 = norm1_g[layer][None, :]
        g2 = norm2_g[layer][None, :]
        w_in_p = _permute_in(w_in[layer])
        lam_p = jnp.stack([lam_q1[layer], lam_k1[layer], lam_q2[layer], lam_k2[layer]])
        g_col = subln_g[layer][:, None]
        pw = pool_w[layer].astype(BF16)
        ps = pool_scale[layer][None, :]
        wp = w_pool_proj[layer].astype(BF16)
        wa = w_attn_proj[layer].astype(BF16)
        wo = w_out[layer].astype(BF16)
        fg = final_g[None, :] if last else None
        moe_layer = layer % 2 == 1
        assert last or not moe_layer, "context tokens are only carried through dense layers"
        if moe_layer:
            rw = jnp.pad(router_w[layer // 2].T, ((0, ROUTER_ROWS - N_EXPERTS), (0, 0)))
            rw_hi = rw.astype(BF16)
            routers = (rw_hi, (rw - rw_hi.astype(F32)).astype(BF16))
            we1 = moe_w1[layer // 2].astype(BF16)
            we3 = moe_w3[layer // 2].astype(BF16)
            we2 = moe_w2[layer // 2].astype(BF16)
        else:
            routers = None
            wf1 = ffn_w1[layer // 2].astype(BF16)
            wf3 = ffn_w3[layer // 2].astype(BF16)
            wf2 = ffn_w2[layer // 2].astype(BF16)

        z, vt = _inproj(x, mod, lat_row, g1, w_in_p, full_kinds, tables, 512)
        if last:
            zc, vtc = _inproj(xc, mod, ctx_row, g1, w_in_p[:, COL_K:COL_V + QKV_WIDTH], ("k",) * 2 + ("v",) * 2,
                              None, n_ctx)
            ckb = 0
        else:
            zc, vtc = _inproj(xc, mod, ctx_row, g1, w_in_p, full_kinds, None, n_ctx)
            ckb = gk
        attn_y = _attention(lam_p, g_col, z, gq // ATTN_HEADS, zc, ckb // ATTN_HEADS, vtc, z, gk // ATTN_HEADS, vt,
                            lam0, 512, ATTN_HEADS)
        pool_y = _pool(z, gp, pw, ps)
        x, h2, *plan = _merge(x, mod, lat_row, g2, z, pool_y, attn_y, wp, wa, wo, routers, 512)

        if not last:
            attn_yc = _attention(lam_p, g_col, zc, COL_Q // QKV_WIDTH, zc, COL_K // QKV_WIDTH, vtc, None, 0, None,
                                 lam0, 512, N_HEADS)
            pool_yc = _pool(zc, gp, pw, ps)
            xc, h2c = _merge(xc, mod, ctx_row, g2, zc, pool_yc, attn_yc, wp, wa, wo, None, n_ctx)

        if moe_layer:
            x = _routed_moe(x, mod, lat_row, we1, we3, we2, h2, plan, fg)
        else:
            x = _ffn(x, h2, mod, lat_row, wf1, wf3, wf2, fg, 512)
        if not last:
            xc = _ffn(xc, h2c, mod, ctx_row, wf1, wf3, wf2, fg, n_ctx)
    return x
```

```python
import functools
import math

import jax
import jax.numpy as jnp
from jax import lax
from jax.experimental import pallas as pl
from jax.experimental.pallas import tpu as pltpu

F32 = jnp.float32
BF16 = jnp.bfloat16

D_MODEL = 1024
EPS = 1e-6
GRID_W = 64
N_HEADS = 8
HEAD_DIM = 64
V_DIM = 2 * HEAD_DIM
ROPE_THETA = 10000.0
AXIS_DIM = HEAD_DIM // 2
ROPE_PAIRS = AXIS_DIM // 2
POOL_WINDOWS = (2, 4, 8, 16)
POOL_WIDTH = 512
POOL_GROUP_DIM = POOL_WIDTH // len(POOL_WINDOWS)
N_EXPERTS = 8
Q_SCALE = HEAD_DIM ** -0.5
LOG2E = math.log2(math.e)
MAX_UNSHIFTED_LOG2 = 80.0
SQ_NORM_SLACK = 1.05

QKV_WIDTH = N_HEADS * V_DIM
COL_G = 0
COL_Q = COL_G + 2 * D_MODEL
COL_K = COL_Q + QKV_WIDTH
COL_V = COL_K + QKV_WIDTH
IN_WIDTH = COL_V + QKV_WIDTH + POOL_WIDTH
COL_P = COL_V
IN_CHUNK = 512

LANES = 128
MOD_ROWS = 40

VMEM_LIMIT = 56 * 1024 * 1024


def _resident(shape):
    nd = len(shape)
    return pl.BlockSpec(shape, lambda *_: (0,) * nd, pipeline_mode=pl.Buffered(1))


def _params(*sem):
    return pltpu.CompilerParams(dimension_semantics=sem, vmem_limit_bytes=VMEM_LIMIT)


def _sigmoid(v):
    return 1.0 / (1.0 + jnp.exp(-v))


def _rms(v):
    return v * lax.rsqrt(jnp.mean(v * v, axis=-1, keepdims=True) + EPS)


def _mod_kernel(s_ref, w_ref, b_ref, o_ref):
    s = s_ref[...]
    s = s * _sigmoid(s)
    w = w_ref[...]
    s_hi = s.astype(BF16)
    s_lo = (s - s_hi.astype(F32)).astype(BF16)
    w_hi = w.astype(BF16)
    w_lo = (w - w_hi.astype(F32)).astype(BF16)
    acc = jnp.dot(s_hi, w_hi, preferred_element_type=F32)
    acc += jnp.dot(s_hi, w_lo, preferred_element_type=F32)
    acc += jnp.dot(s_lo, w_hi, preferred_element_type=F32)
    o_ref[...] = acc + b_ref[...]


def _modulation(s_in, w_mod, b_mod):
    depth, d, n = w_mod.shape
    tn = 1024
    return pl.pallas_call(
        _mod_kernel,
        grid=(depth, n // tn),
        in_specs=[
            pl.BlockSpec((MOD_ROWS, d), lambda l, j: (0, 0)),
            pl.BlockSpec((None, d, tn), lambda l, j: (l, 0, j)),
            pl.BlockSpec((None, 1, tn), lambda l, j: (l, 0, j)),
        ],
        out_specs=pl.BlockSpec((None, MOD_ROWS, tn), lambda l, j: (l, 0, j)),
        out_shape=jax.ShapeDtypeStruct((depth, MOD_ROWS, n), F32),
        compiler_params=_params("parallel", "parallel"),
        name="modulation",
    )(s_in, w_mod, b_mod.reshape(depth, 1, n))


def _rope(z, c, s_up, s_dn):
    up = pltpu.roll(z, LANES - ROPE_PAIRS, 1)
    dn = pltpu.roll(z, ROPE_PAIRS, 1)
    return z * c + up * s_up + dn * s_dn


def _inproj_kernel(*refs, kinds, rope):
    if rope:
        x_ref, mod_ref, g_ref, w_ref, c_ref, su_ref, sd_ref, z_ref, vt_ref = refs
    else:
        x_ref, mod_ref, g_ref, w_ref, z_ref, vt_ref = refs
    y = _rms(x_ref[...]) * g_ref[...]
    h = (y * (1.0 + mod_ref[1:2, :]) + mod_ref[0:1, :]).astype(BF16)
    z_col, head = 0, 0
    for ci, kind in enumerate(kinds):
        lo = ci * IN_CHUNK
        z = jnp.dot(h, w_ref[:, lo:lo + IN_CHUNK], preferred_element_type=F32)
        if kind == "v":
            for j in range(0, IN_CHUNK, V_DIM):
                vt_ref[head * V_DIM:(head + 1) * V_DIM, :] = z[:, j:j + V_DIM].T.astype(BF16)
                head += 1
            continue
        if rope and kind in ("q", "k"):
            c, su, sd = c_ref[...], su_ref[...], sd_ref[...]
            z = jnp.concatenate(
                [_rope(z[:, j:j + LANES], c, su, sd) for j in range(0, IN_CHUNK, LANES)], axis=1)
        if kind == "q":
            z = z * (Q_SCALE * LOG2E)
        z_ref[:, z_col:z_col + IN_CHUNK] = z.astype(BF16)
        z_col += IN_CHUNK


def _inproj(x, mod, mod_row, g, w, kinds, tables, t):
    b, l, d = x.shape
    n_v = sum(k == "v" for k in kinds)
    wz = IN_CHUNK * (len(kinds) - n_v)
    rows_vt = n_v * IN_CHUNK
    rope = tables is not None
    in_specs = [
        pl.BlockSpec((None, t, d), lambda bi, i: (bi, i, 0)),
        pl.BlockSpec((None, 6, d), lambda bi, i: (mod_row(bi), 0, 0)),
        pl.BlockSpec((1, d), lambda bi, i: (0, 0)),
        _resident((d, IN_CHUNK * len(kinds))),
    ]
    args = [x, mod, g, w]
    if rope:
        in_specs += [pl.BlockSpec((t, LANES), lambda bi, i: (i, 0))] * 3
        args += list(tables)
    return pl.pallas_call(
        functools.partial(_inproj_kernel, kinds=kinds, rope=rope),
        grid=(b, l // t),
        in_specs=in_specs,
        out_specs=[pl.BlockSpec((None, t, wz), lambda bi, i: (bi, i, 0)),
                   pl.BlockSpec((None, rows_vt, t), lambda bi, i: (bi, 0, i))],
        out_shape=[jax.ShapeDtypeStruct((b, l, wz), BF16), jax.ShapeDtypeStruct((b, rows_vt, l), BF16)],
        compiler_params=_params("parallel", "parallel"),
        name="inproj",
    )(*args)


def _pool_kernel(u_ref, pw_ref, ps_ref, o_ref):
    l = u_ref.shape[0]
    t = lax.broadcasted_iota(jnp.int32, (l, POOL_GROUP_DIM), 0)

    def shifted(a, k):
        if k > 0:
            return jnp.where(t >= k, pltpu.roll(a, k, 0), 0.0)
        return jnp.where(t < l + k, pltpu.roll(a, l + k, 0), 0.0)

    for gi, w in enumerate(POOL_WINDOWS):
        lo = gi * POOL_GROUP_DIM
        u = u_ref[:, lo:lo + POOL_GROUP_DIM].astype(F32)
        back, fwd, span = u, u, 1
        while span < w // 2:
            back = back + shifted(back, span)
            fwd = fwd + shifted(fwd, -span)
            span *= 2
        win = shifted(back, 1) + fwd
        cnt = (jnp.minimum(t + w // 2, l) - jnp.maximum(t - w // 2, 0)).astype(F32)
        m = (win / cnt - u).astype(BF16)
        y = jnp.dot(m, pw_ref[gi], preferred_element_type=F32)
        o_ref[:, lo:lo + POOL_GROUP_DIM] = (y * ps_ref[:, lo:lo + POOL_GROUP_DIM]).astype(BF16)


def _pool(z, col_block, pool_w, pool_scale):
    b, l, _ = z.shape
    return pl.pallas_call(
        _pool_kernel,
        grid=(b,),
        in_specs=[
            pl.BlockSpec((None, l, POOL_WIDTH), lambda bi: (bi, 0, col_block)),
            _resident(pool_w.shape),
            pl.BlockSpec((1, POOL_WIDTH), lambda bi: (0, 0)),
        ],
        out_specs=pl.BlockSpec((None, l, POOL_WIDTH), lambda bi: (bi, 0, 0)),
        out_shape=jax.ShapeDtypeStruct((b, l, POOL_WIDTH), BF16),
        compiler_params=_params("parallel"),
        name="pool",
    )(z, pool_w, pool_scale)


def _query_chunks(nq, tq):
    return [(c0, min(tq, nq - c0)) for c0 in range(0, nq, tq)]


def _attn_kernel(*refs, n_lat, tq, lam0, heads):
    if n_lat:
        lam_ref, g_ref, q_ref, kc_ref, vtc_ref, kl_ref, vtl_ref, o_ref, p_sc = refs
    else:
        lam_ref, g_ref, q_ref, kc_ref, vtc_ref, o_ref, p_sc = refs
    lp = lam_ref[...]
    lam = (jnp.exp(jnp.sum(lp[0:1] * lp[1:2], axis=1, keepdims=True))
           - jnp.exp(jnp.sum(lp[2:3] * lp[3:4], axis=1, keepdims=True)) + lam0)
    nq = q_ref.shape[0]
    half = lax.broadcasted_iota(jnp.int32, (8, V_DIM), 1) // HEAD_DIM
    sel = (half == lax.broadcasted_iota(jnp.int32, (8, V_DIM), 0)).astype(BF16)
    nt = (((1,), (1,)), ((), ()))

    def sq_norm_max(a):
        return jnp.max(lax.dot_general(sel, a * a, nt, preferred_element_type=F32), axis=1, keepdims=True)

    def head_groups(hh):
        cs = slice(hh * V_DIM, (hh + 1) * V_DIM)
        groups = [(kc_ref[:, cs], vtc_ref[cs, :])]
        if n_lat:
            groups.append((kl_ref[:, cs], vtl_ref[cs, :]))
        return cs, groups

    bound = None
    for hh in range(heads):
        cs, groups = head_groups(hh)
        k_sq = sq_norm_max(groups[0][0])
        for k, _ in groups[1:]:
            k_sq = jnp.maximum(k_sq, sq_norm_max(k))
        b_h = sq_norm_max(q_ref[:, cs]) * k_sq
        bound = b_h if bound is None else jnp.maximum(bound, b_h)
    small = jnp.max(bound) * SQ_NORM_SLACK < MAX_UNSHIFTED_LOG2 ** 2

    def scores(shift, groups, cs, c0, n, slot):
        q = q_ref[c0:c0 + n, cs]
        lane = lax.broadcasted_iota(jnp.int32, q.shape, 1)
        sums = []
        for mi, first in enumerate((True, False)):
            qm = jnp.where((lane < HEAD_DIM) if first else (lane >= HEAD_DIM), q, jnp.zeros_like(q))
            s_t = [lax.dot_general(k, qm, nt, preferred_element_type=F32) for k, _ in groups]
            if shift:
                m = jnp.max(s_t[0], axis=0, keepdims=True)
                for s_g in s_t[1:]:
                    m = jnp.maximum(m, jnp.max(s_g, axis=0, keepdims=True))
                s_t = [s_g - m for s_g in s_t]
            l_m, row = None, 0
            for s_g in s_t:
                p_g = jnp.exp2(s_g)
                l_g = jnp.sum(p_g, axis=0, keepdims=True)
                l_m = l_g if l_m is None else l_m + l_g
                p_sc[slot, mi, row:row + p_g.shape[0], 0:n] = p_g.astype(BF16)
                row += p_g.shape[0]
            sums.append(l_m)
        return sums

    def values(groups, cs, c0, n, slot, sums):
        r = (lam * sums[0] * (1.0 / sums[1])).astype(BF16)
        o_t, row = None, 0
        for k, vt in groups:
            rows = slice(row, row + k.shape[0])
            part = jnp.dot(vt, p_sc[slot, 0, rows, 0:n] - r * p_sc[slot, 1, rows, 0:n], preferred_element_type=F32)
            o_t = part if o_t is None else o_t + part
            row += k.shape[0]
        o_t = o_t * (1.0 / sums[0])
        o_t = o_t * lax.rsqrt(jnp.mean(o_t * o_t, axis=0, keepdims=True) + EPS)
        o_t = o_t * (g_ref[...] * (1.0 - lam0))
        o_ref[c0:c0 + n, cs] = o_t.T.astype(BF16)

    def attend(shift):
        work = [(hh, c0, n) for hh in range(heads) for c0, n in _query_chunks(nq, tq)]
        pending = None
        for i, (hh, c0, n) in enumerate(work):
            cs, groups = head_groups(hh)
            sums = scores(shift, groups, cs, c0, n, i % 2)
            if pending is not None:
                values(*pending)
            pending = (groups, cs, c0, n, i % 2, sums)
        values(*pending)

    lax.cond(small, lambda: attend(False), lambda: attend(True))


def _attention(lam_p, g_col, zq, q_blk, zc, kc_blk, vtc, zl, kl_blk, vtl, lam0, tq, heads):
    b, nq, _ = zq.shape
    n_ctx = zc.shape[1]
    n_lat = 0 if zl is None else zl.shape[1]
    w = heads * V_DIM

    def col(blk, n):
        return pl.BlockSpec((None, n, w), lambda bi, h: (bi, 0, blk + h))

    def vt_rows(n):
        return pl.BlockSpec((None, w, n), lambda bi, h: (bi, h, 0))

    in_specs = [
        pl.BlockSpec(lam_p.shape, lambda bi, h: (0, 0)),
        pl.BlockSpec(g_col.shape, lambda bi, h: (0, 0)),
        col(q_blk, nq), col(kc_blk, n_ctx), vt_rows(n_ctx),
    ]
    args = [lam_p, g_col, zq, zc, vtc]
    if n_lat:
        in_specs += [col(kl_blk, n_lat), vt_rows(n_lat)]
        args += [zl, vtl]
    tq = min(tq, nq)
    return pl.pallas_call(
        functools.partial(_attn_kernel, n_lat=n_lat, tq=tq, lam0=lam0, heads=heads),
        grid=(b, N_HEADS // heads),
        in_specs=in_specs,
        out_specs=pl.BlockSpec((None, nq, w), lambda bi, h: (bi, 0, h)),
        out_shape=jax.ShapeDtypeStruct((b, nq, QKV_WIDTH), BF16),
        scratch_shapes=[pltpu.VMEM((2, 2, n_ctx + n_lat, tq), BF16)],
        compiler_params=_params("parallel", "parallel"),
        name="attention",
    )(*args)


def _merge_kernel(*refs, plan):
    if plan:
        (x_ref, mod_ref, g2_ref, zg_ref, py_ref, ay_ref, wp_ref, wa_ref, wo_ref, rh_ref, rl_ref,
         xo_ref, h2_ref, prow_ref, pcol_ref, cnt_ref) = refs
    else:
        x_ref, mod_ref, g2_ref, zg_ref, py_ref, ay_ref, wp_ref, wa_ref, wo_ref, xo_ref, h2_ref = refs
    d = x_ref.shape[1]
    g_pool = _sigmoid(zg_ref[:, 0:d].astype(F32))
    g_attn = _sigmoid(zg_ref[:, d:2 * d].astype(F32))
    y = (g_pool * jnp.dot(py_ref[...], wp_ref[...], preferred_element_type=F32)
         + g_attn * jnp.dot(ay_ref[...], wa_ref[...], preferred_element_type=F32))
    o = jnp.dot(y.astype(BF16), wo_ref[...], preferred_element_type=F32)
    xn = x_ref[...] + mod_ref[2:3, :] * o
    xo_ref[...] = xn
    h2 = _rms(xn) * g2_ref[...] * (1.0 + mod_ref[4:5, :]) + mod_ref[3:4, :]
    h2_hi = h2.astype(BF16)
    h2_ref[...] = h2_hi
    if plan:
        h2_lo = (h2 - h2_hi.astype(F32)).astype(BF16)
        rows, cnt = _route_plan(h2_hi, h2_lo, rh_ref[...], rl_ref[...])
        prow_ref[...] = rows
        pcol_ref[...] = jnp.concatenate([rows, jnp.zeros((LANES - 8, rows.shape[1]), F32)], axis=0).T
        cnt_ref[...] = jnp.broadcast_to(cnt, (N_EXPERTS, LANES))


def _merge(x, mod, mod_row, g2, z, pool_y, attn_y, wp, wa, wo, router_w, t):
    b, l, d = x.shape
    plan = router_w is not None
    nt = l // t
    tile = lambda w: pl.BlockSpec((None, t, w), lambda bi, i: (bi, i, 0))
    in_specs = [
        tile(d),
        pl.BlockSpec((None, 6, d), lambda bi, i: (mod_row(bi), 0, 0)),
        pl.BlockSpec((1, d), lambda bi, i: (0, 0)),
        tile(2 * d),
        tile(POOL_WIDTH), tile(QKV_WIDTH),
        _resident(wp.shape), _resident(wa.shape), _resident(wo.shape),
    ]
    args = [x, mod, g2, z, pool_y, attn_y, wp, wa, wo]
    out_specs = [tile(d), tile(d)]
    out_shape = [jax.ShapeDtypeStruct((b, l, d), F32), jax.ShapeDtypeStruct((b, l, d), BF16)]
    if plan:
        assert t == MOE_TG
        in_specs += [_resident(router_w[0].shape)] * 2
        args += list(router_w)
        out_specs += [pl.BlockSpec((None, 8, t), lambda bi, i: (bi * nt + i, 0, 0)),
                      tile(LANES),
                      pl.BlockSpec((None, N_EXPERTS, LANES), lambda bi, i: (bi * nt + i, 0, 0))]
        out_shape += [jax.ShapeDtypeStruct((b * nt, 8, t), F32),
                      jax.ShapeDtypeStruct((b, l, LANES), F32),
                      jax.ShapeDtypeStruct((b * nt, N_EXPERTS, LANES), F32)]
    return pl.pallas_call(
        functools.partial(_merge_kernel, plan=plan),
        grid=(b, nt),
        in_specs=in_specs,
        out_specs=out_specs,
        out_shape=out_shape,
        compiler_params=_params("parallel", "parallel"),
        name="merge",
    )(*args)


def _swiglu(h, w1, w3, w2):
    a = jnp.dot(h, w1, preferred_element_type=F32)
    b = jnp.dot(h, w3, preferred_element_type=F32)
    return jnp.dot((a * _sigmoid(a) * b).astype(BF16), w2, preferred_element_type=F32)


def _ffn_kernel(*refs, final):
    if final:
        x_ref, h2_ref, mod_ref, w1_ref, w3_ref, w2_ref, fg_ref, o_ref = refs
    else:
        x_ref, h2_ref, mod_ref, w1_ref, w3_ref, w2_ref, o_ref = refs
    y = _swiglu(h2_ref[...], w1_ref[...], w3_ref[...], w2_ref[...])
    xn = x_ref[...] + mod_ref[5:6, :] * y
    if final:
        xn = _rms(xn) * fg_ref[...]
    o_ref[...] = xn


def _ffn(x, h2, mod, mod_row, w1, w3, w2, final_g, t):
    b, l, d = x.shape
    final = final_g is not None
    tile = pl.BlockSpec((None, t, d), lambda bi, i: (bi, i, 0))
    in_specs = [tile, tile, pl.BlockSpec((None, 6, d), lambda bi, i: (mod_row(bi), 0, 0)),
                _resident(w1.shape), _resident(w3.shape), _resident(w2.shape)]
    args = [x, h2, mod, w1, w3, w2]
    if final:
        in_specs.append(pl.BlockSpec((1, d), lambda bi, i: (0, 0)))
        args.append(final_g)
    return pl.pallas_call(
        functools.partial(_ffn_kernel, final=final),
        grid=(b, l // t),
        in_specs=in_specs,
        out_specs=tile,
        out_shape=jax.ShapeDtypeStruct((b, l, d), F32),
        compiler_params=_params("parallel", "parallel"),
        name="ffn",
    )(*args)


MOE_TG = 512
MOE_TM = 1024
PIECE = 16
BIG_PIECE = 64
MOE_SLOTS = 2 * MOE_TG + N_EXPERTS * PIECE
ROUTER_ROWS = 16


def _route_plan(h2_hi, h2_lo, rh, rl):
    tg = h2_hi.shape[0]
    nt = (((1,), (1,)), ((), ()))
    logits = (lax.dot_general(rh, h2_hi, nt, preferred_element_type=F32)
              + lax.dot_general(rl, h2_hi, nt, preferred_element_type=F32)
              + lax.dot_general(rh, h2_lo, nt, preferred_element_type=F32))
    sub = lax.broadcasted_iota(jnp.int32, logits.shape, 0).astype(F32)
    neg = jnp.float32(-jnp.inf)
    logits = jnp.where(sub < N_EXPERTS, logits, neg)
    v1 = jnp.max(logits, axis=0, keepdims=True)
    i1 = jnp.min(jnp.where(logits == v1, sub, float(ROUTER_ROWS)), axis=0, keepdims=True)
    rest = jnp.where(sub == i1, neg, logits)
    v2 = jnp.max(rest, axis=0, keepdims=True)
    i2 = jnp.min(jnp.where(rest == v2, sub, float(ROUTER_ROWS)), axis=0, keepdims=True)
    e2 = jnp.exp(v2 - v1)
    w1 = 1.0 / (1.0 + e2)
    w2 = e2 * w1
    two = w2 != 0.0
    sel = ((sub == i1) | ((sub == i2) & two)).astype(F32)
    before = (lax.broadcasted_iota(jnp.int32, (tg, tg), 0)
              < lax.broadcasted_iota(jnp.int32, (tg, tg), 1)).astype(BF16)
    rank = jnp.dot(sel.astype(BF16), before, preferred_element_type=F32)
    cnt = jnp.sum(sel, axis=1, keepdims=True)
    cpad = jnp.ceil(cnt * (1.0 / PIECE)) * PIECE
    subc = sub[:, 0:1]
    lbase = jnp.zeros_like(cpad)
    for e in range(N_EXPERTS - 1):
        c_e = jnp.sum(jnp.where(subc == e, cpad, 0.0), axis=0, keepdims=True)
        lbase = lbase + jnp.where(subc > e, c_e, 0.0)
    slot = lbase + rank
    e_a = jnp.where(two, jnp.minimum(i1, i2), i1)
    e_b = jnp.where(two, jnp.maximum(i1, i2), i1)
    ls_a = jnp.sum(jnp.where(sub == e_a, slot, 0.0), axis=0, keepdims=True)
    ls_b = jnp.sum(jnp.where(sub == e_b, slot, 0.0), axis=0, keepdims=True)
    w_a = jnp.where(e_a == i1, w1, w2)
    w_b = jnp.where(two, jnp.where(e_b == i1, w1, w2), 0.0)
    rows = jnp.concatenate([ls_a, ls_b, w_a, w_b, jnp.zeros((4, tg), F32)], axis=0)
    return rows, cnt[0:N_EXPERTS]


def _segment_copies(i, seg, local_buf, sorted_hbm, sem, to_sorted, action):
    lb_ref, gb_ref, nb_ref, ns_ref = seg
    for e in range(N_EXPERTS):
        k = i * N_EXPERTS + e

        def copy(off, rows, k=k):
            lo = pl.multiple_of(lb_ref[k] + off, PIECE)
            go = pl.multiple_of(gb_ref[k] + off, PIECE)
            local = local_buf.at[pl.ds(lo, rows), :]
            remote = sorted_hbm.at[pl.ds(go, rows), :]
            c = pltpu.make_async_copy(local, remote, sem) if to_sorted else pltpu.make_async_copy(remote, local, sem)
            getattr(c, action)()

        def big(j, carry, copy=copy):
            copy(j * BIG_PIECE, BIG_PIECE)
            return carry

        def small(j, carry, copy=copy, k=k):
            copy(nb_ref[k] * BIG_PIECE + j * PIECE, PIECE)
            return carry

        lax.fori_loop(0, nb_ref[k], big, 0)
        lax.fori_loop(0, ns_ref[k], small, 0)


def _zero_fill(fill_ref, zbuf, out_ref, sem, action):
    def copy(go, rows):
        c = pltpu.make_async_copy(zbuf.at[pl.ds(0, rows), :], out_ref.at[pl.ds(go, rows), :], sem)
        getattr(c, action)()

    for e in range(N_EXPERTS):
        def tail(j, carry, e=e):
            copy(pl.multiple_of(fill_ref[e] + j * PIECE, PIECE), PIECE)
            return carry

        lax.fori_loop(0, fill_ref[N_EXPERTS + e], tail, 0)

    def rest(j, carry):
        copy(pl.multiple_of(fill_ref[2 * N_EXPERTS] + j * MOE_TM, MOE_TM), MOE_TM)
        return carry

    lax.fori_loop(0, fill_ref[2 * N_EXPERTS + 1], rest, 0)


def _dispatch_kernel(lb_ref, gb_ref, nb_ref, ns_ref, fill_ref, h2_ref, prow_ref, out_ref, cbuf, zbuf, sems):
    seg = (lb_ref, gb_ref, nb_ref, ns_ref)
    i = pl.program_id(0)
    nt = pl.num_programs(0)
    slot = i % 2
    tg = h2_ref.shape[0]

    @pl.when(i >= 2)
    def _():
        _segment_copies(i - 2, seg, cbuf.at[slot], out_ref, sems.at[slot], True, "wait")

    s = lax.broadcasted_iota(jnp.int32, (MOE_SLOTS, tg), 0).astype(F32)
    onehot = jnp.where(s == prow_ref[0:1, :], 1.0, jnp.where(s == prow_ref[1:2, :], 1.0, 0.0)).astype(BF16)
    cbuf[slot] = jnp.dot(onehot, h2_ref[...], preferred_element_type=F32).astype(BF16)
    _segment_copies(i, seg, cbuf.at[slot], out_ref, sems.at[slot], True, "start")

    @pl.when(i == nt - 1)
    def _():
        @pl.when(i >= 1)
        def _():
            _segment_copies(i - 1, seg, cbuf.at[1 - slot], out_ref, sems.at[1 - slot], True, "wait")

        _segment_copies(i, seg, cbuf.at[slot], out_ref, sems.at[slot], True, "wait")
        zbuf[...] = jnp.zeros_like(zbuf)
        _zero_fill(fill_ref, zbuf, out_ref, sems.at[2], "start")
        _zero_fill(fill_ref, zbuf, out_ref, sems.at[2], "wait")


def _dispatch(seg, fill, h2, prow, rows, tg):
    n, d = h2.shape
    grid_spec = pltpu.PrefetchScalarGridSpec(
        num_scalar_prefetch=len(seg) + 1,
        grid=(n // tg,),
        in_specs=[
            pl.BlockSpec((tg, d), lambda i, *_: (i, 0)),
            pl.BlockSpec((None, 8, tg), lambda i, *_: (i, 0, 0)),
        ],
        out_specs=pl.BlockSpec(memory_space=pl.ANY),
        scratch_shapes=[pltpu.VMEM((2, MOE_SLOTS, d), BF16), pltpu.VMEM((MOE_TM, d), BF16),
                        pltpu.SemaphoreType.DMA((3,))],
    )
    return pl.pallas_call(
        _dispatch_kernel,
        grid_spec=grid_spec,
        out_shape=jax.ShapeDtypeStruct((rows, d), BF16),
        compiler_params=_params("arbitrary"),
        name="moe_dispatch",
    )(*seg, fill, h2, prow)


def _group_kernel(te_ref, nu_ref, x_ref, w1_ref, w3_ref, w2_ref, o_ref):
    del te_ref
    live = pl.program_id(0) < nu_ref[0]

    @pl.when(live)
    def _():
        o_ref[...] = _swiglu(x_ref[...], w1_ref[...], w3_ref[...], w2_ref[...]).astype(BF16)

    @pl.when(jnp.logical_not(live))
    def _():
        o_ref[...] = jnp.zeros_like(o_ref)


def _group(tile_expert, n_used, xs, w1, w3, w2, tm):
    rows, d = xs.shape
    f = w1.shape[2]
    used = lambda r, te, nu: (jnp.minimum(r, nu[0] - 1), 0)
    grid_spec = pltpu.PrefetchScalarGridSpec(
        num_scalar_prefetch=2,
        grid=(rows // tm,),
        in_specs=[
            pl.BlockSpec((tm, d), used),
            pl.BlockSpec((None, d, f), lambda r, te, nu: (te[r], 0, 0)),
            pl.BlockSpec((None, d, f), lambda r, te, nu: (te[r], 0, 0)),
            pl.BlockSpec((None, f, d), lambda r, te, nu: (te[r], 0, 0)),
        ],
        out_specs=pl.BlockSpec((tm, d), lambda r, te, nu: (r, 0)),
    )
    return pl.pallas_call(
        _group_kernel,
        grid_spec=grid_spec,
        out_shape=jax.ShapeDtypeStruct((rows, d), BF16),
        compiler_params=_params("arbitrary"),
        name="moe_group",
    )(tile_expert, n_used, xs, w1, w3, w2)


def _combine_kernel(*refs, final):
    if final:
        lb_ref, gb_ref, nb_ref, ns_ref, x_ref, pcol_ref, mod_ref, fg_ref, y_ref, o_ref, ybuf, sems = refs
    else:
        lb_ref, gb_ref, nb_ref, ns_ref, x_ref, pcol_ref, mod_ref, y_ref, o_ref, ybuf, sems = refs
    seg = (lb_ref, gb_ref, nb_ref, ns_ref)
    i = pl.program_id(0)
    nt = pl.num_programs(0)
    slot = i % 2
    tg = x_ref.shape[0]

    @pl.when(i == 0)
    def _():
        ybuf[...] = jnp.zeros_like(ybuf)
        _segment_copies(i, seg, ybuf.at[slot], y_ref, sems.at[slot], False, "start")

    @pl.when(i + 1 < nt)
    def _():
        _segment_copies(i + 1, seg, ybuf.at[1 - slot], y_ref, sems.at[1 - slot], False, "start")

    s = lax.broadcasted_iota(jnp.int32, (tg, MOE_SLOTS), 1).astype(F32)
    pc = pcol_ref[...]
    scatter = jnp.where(s == pc[:, 0:1], pc[:, 2:3], jnp.where(s == pc[:, 1:2], pc[:, 3:4], 0.0)).astype(BF16)
    _segment_copies(i, seg, ybuf.at[slot], y_ref, sems.at[slot], False, "wait")
    y = jnp.dot(scatter, ybuf[slot], preferred_element_type=F32)
    xn = x_ref[...] + mod_ref[5:6, :] * y
    if final:
        xn = _rms(xn) * fg_ref[...]
    o_ref[...] = xn


def _combine(seg, x2, pcol, mod, mod_row, final_g, ys, tg):
    n, d = x2.shape
    final = final_g is not None
    in_specs = [
        pl.BlockSpec((tg, d), lambda i, *_: (i, 0)),
        pl.BlockSpec((tg, LANES), lambda i, *_: (i, 0)),
        pl.BlockSpec((None, 6, d), lambda i, *_: (mod_row(i), 0, 0)),
    ]
    args = [x2, pcol, mod]
    if final:
        in_specs.append(pl.BlockSpec((1, d), lambda i, *_: (0, 0)))
        args.append(final_g)
    in_specs.append(pl.BlockSpec(memory_space=pl.ANY))
    args.append(ys)
    grid_spec = pltpu.PrefetchScalarGridSpec(
        num_scalar_prefetch=len(seg),
        grid=(n // tg,),
        in_specs=in_specs,
        out_specs=pl.BlockSpec((tg, d), lambda i, *_: (i, 0)),
        scratch_shapes=[pltpu.VMEM((2, MOE_SLOTS, d), BF16), pltpu.SemaphoreType.DMA((2,))],
    )
    return pl.pallas_call(
        functools.partial(_combine_kernel, final=final),
        grid_spec=grid_spec,
        out_shape=jax.ShapeDtypeStruct((n, d), F32),
        compiler_params=_params("arbitrary"),
        name="moe_combine",
    )(*seg, *args)


def _routed_moe(x, mod, batch_row, w1, w3, w2, h2, plan, final_g):
    b, l, d = x.shape
    n = b * l
    tg, tm = MOE_TG, MOE_TM
    nt = n // tg
    x2, h22 = x.reshape(n, d), h2.reshape(n, d)
    mod_row = lambda i: batch_row((i * tg) // l)
    prow, pcol, cnt = plan
    pcol = pcol.reshape(n, LANES)

    cnt = cnt[:, :, 0].astype(jnp.int32)
    cpad = (cnt + PIECE - 1) // PIECE * PIECE
    lbase = jnp.cumsum(cpad, axis=1) - cpad
    tot = (jnp.sum(cpad, axis=0) + tm - 1) // tm * tm
    ends = jnp.cumsum(tot)
    gbase = (ends - tot)[None, :] + jnp.cumsum(cpad, axis=0) - cpad
    rows = (2 * n + nt * N_EXPERTS * (PIECE - 1) + N_EXPERTS * (tm - 1) + tm - 1) // tm * tm
    n_used = (ends[-1] // tm).astype(jnp.int32)
    tile_start = jnp.arange(rows // tm, dtype=jnp.int32) * tm
    tile_expert = jnp.sum(jnp.minimum(tile_start, ends[-1] - 1)[:, None] >= ends[None, :], axis=1).astype(jnp.int32)
    seg = tuple(a.reshape(-1).astype(jnp.int32)
                for a in (lbase, gbase, cpad // BIG_PIECE, cpad % BIG_PIECE // PIECE))
    data_end = ends - tot + jnp.sum(cpad, axis=0)
    fill = jnp.concatenate([data_end, (ends - data_end) // PIECE, ends[-1:], (rows - ends[-1:]) // tm]).astype(jnp.int32)

    xs = _dispatch(seg, fill, h22, prow, rows, tg)
    ys = _group(tile_expert, n_used.reshape(1), xs, w1, w3, w2, tm)
    out = _combine(seg, x2, pcol, mod, mod_row, final_g, ys, tg)
    return out.reshape(b, l, d)


def _rope_tables(l):
    rows = l // GRID_W
    row = jnp.repeat(jnp.arange(rows, dtype=F32), GRID_W)
    colp = jnp.tile(jnp.arange(GRID_W, dtype=F32), rows)
    inv = ROPE_THETA ** (-jnp.arange(ROPE_PAIRS, dtype=F32) * 2.0 / AXIS_DIM)
    ang_r, ang_c = row[:, None] * inv, colp[:, None] * inv
    zero = jnp.zeros_like(ang_r)
    cos64 = jnp.concatenate([jnp.cos(ang_r)] * 2 + [jnp.cos(ang_c)] * 2, axis=1)
    up64 = jnp.concatenate([-jnp.sin(ang_r), zero, -jnp.sin(ang_c), zero], axis=1)
    dn64 = jnp.concatenate([zero, jnp.sin(ang_r), zero, jnp.sin(ang_c)], axis=1)
    return tuple(jnp.tile(a, (1, LANES // HEAD_DIM)) for a in (cos64, up64, dn64))


def _lambda_init(layer):
    return 0.8 - 0.6 * math.exp(-0.3 * layer)


def _permute_in(w):
    off_q = POOL_WIDTH
    off_g = off_q + 3 * QKV_WIDTH
    return jnp.concatenate([w[:, off_g:], w[:, off_q:off_g], w[:, :off_q]], axis=1).astype(BF16)


def kernel(x, c, ctx, c_ctx, w_mod, b_mod, norm1_g, norm2_g, w_in, pool_w, pool_scale, lam_q1, lam_k1,
           lam_q2, lam_k2, subln_g, w_pool_proj, w_attn_proj, w_out, ffn_w1, ffn_w3, ffn_w2, router_w,
           moe_w1, moe_w3, moe_w2, final_g):
    b, l, d = x.shape
    n_ctx = ctx.shape[1]
    depth = w_mod.shape[0]
    assert d == D_MODEL and b + 1 <= MOD_ROWS and l % 512 == 0 and n_ctx % 256 == 0

    s_in = jnp.concatenate([c, c_ctx[None, :], jnp.zeros((MOD_ROWS - b - 1, d), F32)], axis=0)
    mod_all = _modulation(s_in, w_mod, b_mod)
    tables = _rope_tables(l)
    lat_row = lambda bi: bi
    ctx_row = lambda bi: b
    gq, gk, gp = COL_Q // V_DIM, COL_K // V_DIM, COL_P // POOL_WIDTH
    full_kinds = ("g",) * 4 + ("q",) * 2 + ("k",) * 2 + ("v",) * 2 + ("p",)

    xc = ctx
    for layer in range(depth):
        last = layer == depth - 1
        lam0 = _lambda_init(layer)
        mod = mod_all[layer].reshape(MOD_ROWS, 6, d)
        g1 = norm1_g[layer][None, :]
        g2 = norm2_g[layer][None, :]
        w_in_p = _permute_in(w_in[layer])
        lam_p = jnp.stack([lam_q1[layer], lam_k1[layer], lam_q2[layer], lam_k2[layer]])
        g_col = subln_g[layer][:, None]
        pw = pool_w[layer].astype(BF16)
        ps = pool_scale[layer][None, :]
        wp = w_pool_proj[layer].astype(BF16)
        wa = w_attn_proj[layer].astype(BF16)
        wo = w_out[layer].astype(BF16)
        fg = final_g[None, :] if last else None
        moe_layer = layer % 2 == 1
        assert last or not moe_layer, "context tokens are only carried through dense layers"
        if moe_layer:
            rw = jnp.pad(router_w[layer // 2].T, ((0, ROUTER_ROWS - N_EXPERTS), (0, 0)))
            rw_hi = rw.astype(BF16)
            routers = (rw_hi, (rw - rw_hi.astype(F32)).astype(BF16))
            we1 = moe_w1[layer // 2].astype(BF16)
            we3 = moe_w3[layer // 2].astype(BF16)
            we2 = moe_w2[layer // 2].astype(BF16)
        else:
            routers = None
            wf1 = ffn_w1[layer // 2].astype(BF16)
            wf3 = ffn_w3[layer // 2].astype(BF16)
            wf2 = ffn_w2[layer // 2].astype(BF16)

        z, vt = _inproj(x, mod, lat_row, g1, w_in_p, full_kinds, tables, 1024)
        if last:
            zc, vtc = _inproj(xc, mod, ctx_row, g1, w_in_p[:, COL_K:COL_V + QKV_WIDTH], ("k",) * 2 + ("v",) * 2,
                              None, n_ctx)
            ckb = 0
        else:
            zc, vtc = _inproj(xc, mod, ctx_row, g1, w_in_p, full_kinds, None, n_ctx)
            ckb = gk
        attn_y = _attention(lam_p, g_col, z, gq, zc, ckb, vtc, z, gk, vt, lam0, 512, 1)
        pool_y = _pool(z, gp, pw, ps)
        x, h2, *plan = _merge(x, mod, lat_row, g2, z, pool_y, attn_y, wp, wa, wo, routers,
                              MOE_TG if moe_layer else 1024)

        if not last:
            attn_yc = _attention(lam_p, g_col, zc, COL_Q // QKV_WIDTH, zc, COL_K // QKV_WIDTH, vtc, None, 0, None,
                                 lam0, 512, N_HEADS)
            pool_yc = _pool(zc, gp, pw, ps)
            xc, h2c = _merge(xc, mod, ctx_row, g2, zc, pool_yc, attn_yc, wp, wa, wo, None, n_ctx)

        if moe_layer:
            x = _routed_moe(x, mod, lat_row, we1, we3, we2, h2, plan, fg)
        else:
            x = _ffn(x, h2, mod, lat_row, wf1, wf3, wf2, fg, 512)
        if not last:
            xc = _ffn(xc, h2c, mod, ctx_row, wf1, wf3, wf2, fg, n_ctx)
    return x
```

```python
import functools
import math

import jax
import jax.numpy as jnp
from jax import lax
from jax.experimental import pallas as pl
from jax.experimental.pallas import tpu as pltpu

F32 = jnp.float32
BF16 = jnp.bfloat16

D_MODEL = 1024
EPS = 1e-6
GRID_W = 64
N_HEADS = 8
HEAD_DIM = 64
V_DIM = 2 * HEAD_DIM
ROPE_THETA = 10000.0
AXIS_DIM = HEAD_DIM // 2
ROPE_PAIRS = AXIS_DIM // 2
POOL_WINDOWS = (2, 4, 8, 16)
POOL_WIDTH = 512
POOL_GROUP_DIM = POOL_WIDTH // len(POOL_WINDOWS)
POOL_EDGE = max(POOL_WINDOWS) // 2
N_EXPERTS = 8
Q_SCALE = HEAD_DIM ** -0.5
LOG2E = math.log2(math.e)
MAX_UNSHIFTED_LOG2 = 80.0
SQ_NORM_SLACK = 1.05

QKV_WIDTH = N_HEADS * V_DIM
COL_G = 0
COL_Q = COL_G + 2 * D_MODEL
COL_K = COL_Q + QKV_WIDTH
COL_V = COL_K + QKV_WIDTH
IN_WIDTH = COL_V + QKV_WIDTH + POOL_WIDTH
COL_P = COL_V
IN_CHUNK = 512

LANES = 128
MOD_ROWS = 40

VMEM_LIMIT = 56 * 1024 * 1024


def _resident(shape):
    nd = len(shape)
    return pl.BlockSpec(shape, lambda *_: (0,) * nd, pipeline_mode=pl.Buffered(1))


def _params(*sem):
    return pltpu.CompilerParams(dimension_semantics=sem, vmem_limit_bytes=VMEM_LIMIT)


def _sigmoid(v):
    return 1.0 / (1.0 + jnp.exp(-v))


def _rms(v):
    return v * lax.rsqrt(jnp.mean(v * v, axis=-1, keepdims=True) + EPS)


def _mod_kernel(s_ref, w_ref, b_ref, o_ref):
    s = s_ref[...]
    s = s * _sigmoid(s)
    w = w_ref[...]
    s_hi = s.astype(BF16)
    s_lo = (s - s_hi.astype(F32)).astype(BF16)
    w_hi = w.astype(BF16)
    w_lo = (w - w_hi.astype(F32)).astype(BF16)
    acc = jnp.dot(s_hi, w_hi, preferred_element_type=F32)
    acc += jnp.dot(s_hi, w_lo, preferred_element_type=F32)
    acc += jnp.dot(s_lo, w_hi, preferred_element_type=F32)
    o_ref[...] = acc + b_ref[...]


def _modulation(s_in, w_mod, b_mod):
    depth, d, n = w_mod.shape
    tn = 1024
    return pl.pallas_call(
        _mod_kernel,
        grid=(depth, n // tn),
        in_specs=[
            pl.BlockSpec((MOD_ROWS, d), lambda l, j: (0, 0)),
            pl.BlockSpec((None, d, tn), lambda l, j: (l, 0, j)),
            pl.BlockSpec((None, 1, tn), lambda l, j: (l, 0, j)),
        ],
        out_specs=pl.BlockSpec((None, MOD_ROWS, tn), lambda l, j: (l, 0, j)),
        out_shape=jax.ShapeDtypeStruct((depth, MOD_ROWS, n), F32),
        compiler_params=_params("parallel", "parallel"),
        name="modulation",
    )(s_in, w_mod, b_mod.reshape(depth, 1, n))


def _rope(z, c, s_up, s_dn):
    up = pltpu.roll(z, LANES - ROPE_PAIRS, 1)
    dn = pltpu.roll(z, ROPE_PAIRS, 1)
    return z * c + up * s_up + dn * s_dn


def _inproj_kernel(*refs, kinds, rope):
    if rope:
        x_ref, mod_ref, g_ref, w_ref, c_ref, su_ref, sd_ref, z_ref, vt_ref = refs
    else:
        x_ref, mod_ref, g_ref, w_ref, z_ref, vt_ref = refs
    y = _rms(x_ref[...]) * g_ref[...]
    h = (y * (1.0 + mod_ref[1:2, :]) + mod_ref[0:1, :]).astype(BF16)
    z_col, head = 0, 0
    for ci, kind in enumerate(kinds):
        lo = ci * IN_CHUNK
        z = jnp.dot(h, w_ref[:, lo:lo + IN_CHUNK], preferred_element_type=F32)
        if kind == "v":
            for j in range(0, IN_CHUNK, V_DIM):
                vt_ref[head * V_DIM:(head + 1) * V_DIM, :] = z[:, j:j + V_DIM].T.astype(BF16)
                head += 1
            continue
        if rope and kind in ("q", "k"):
            c, su, sd = c_ref[...], su_ref[...], sd_ref[...]
            z = jnp.concatenate(
                [_rope(z[:, j:j + LANES], c, su, sd) for j in range(0, IN_CHUNK, LANES)], axis=1)
        if kind == "q":
            z = z * (Q_SCALE * LOG2E)
        z_ref[:, z_col:z_col + IN_CHUNK] = z.astype(BF16)
        z_col += IN_CHUNK


def _inproj(x, mod, mod_row, g, w, kinds, tables, t):
    b, l, d = x.shape
    n_v = sum(k == "v" for k in kinds)
    wz = IN_CHUNK * (len(kinds) - n_v)
    rows_vt = n_v * IN_CHUNK
    rope = tables is not None
    in_specs = [
        pl.BlockSpec((None, t, d), lambda bi, i: (bi, i, 0)),
        pl.BlockSpec((None, 6, d), lambda bi, i: (mod_row(bi), 0, 0)),
        pl.BlockSpec((1, d), lambda bi, i: (0, 0)),
        _resident((d, IN_CHUNK * len(kinds))),
    ]
    args = [x, mod, g, w]
    if rope:
        in_specs += [pl.BlockSpec((t, LANES), lambda bi, i: (i, 0))] * 3
        args += list(tables)
    return pl.pallas_call(
        functools.partial(_inproj_kernel, kinds=kinds, rope=rope),
        grid=(b, l // t),
        in_specs=in_specs,
        out_specs=[pl.BlockSpec((None, t, wz), lambda bi, i: (bi, i, 0)),
                   pl.BlockSpec((None, rows_vt, t), lambda bi, i: (bi, 0, i))],
        out_shape=[jax.ShapeDtypeStruct((b, l, wz), BF16), jax.ShapeDtypeStruct((b, rows_vt, l), BF16)],
        compiler_params=_params("parallel", "parallel"),
        name="inproj",
    )(*args)


def _pool_kernel(u_ref, pw_ref, ps_ref, o_ref):
    l = u_ref.shape[0]
    e = POOL_EDGE
    t_head = lax.broadcasted_iota(jnp.int32, (e, POOL_GROUP_DIM), 0)
    t_tail = t_head + (l - e)

    def shifted(a, k):
        r = pltpu.roll(a, k % l, 0)
        if k > 0:
            return jnp.concatenate([jnp.where(t_head >= k, r[0:e], 0.0), r[e:]], axis=0)
        return jnp.concatenate([r[0:l - e], jnp.where(t_tail < l + k, r[l - e:], 0.0)], axis=0)

    def count(t, w):
        return (jnp.minimum(t + w // 2, l) - jnp.maximum(t - w // 2, 0)).astype(F32)

    for gi, w in enumerate(POOL_WINDOWS):
        lo = gi * POOL_GROUP_DIM
        u = u_ref[:, lo:lo + POOL_GROUP_DIM].astype(F32)
        back, fwd, span = u, u, 1
        while span < w // 2:
            back = back + shifted(back, span)
            fwd = fwd + shifted(fwd, -span)
            span *= 2
        win = shifted(back, 1) + fwd
        mean = jnp.concatenate([win[0:e] / count(t_head, w), win[e:l - e] * (1.0 / w),
                                win[l - e:] / count(t_tail, w)], axis=0)
        m = (mean - u).astype(BF16)
        y = jnp.dot(m, pw_ref[gi], preferred_element_type=F32)
        o_ref[:, lo:lo + POOL_GROUP_DIM] = (y * ps_ref[:, lo:lo + POOL_GROUP_DIM]).astype(BF16)


def _pool(z, col_block, pool_w, pool_scale):
    b, l, _ = z.shape
    return pl.pallas_call(
        _pool_kernel,
        grid=(b,),
        in_specs=[
            pl.BlockSpec((None, l, POOL_WIDTH), lambda bi: (bi, 0, col_block)),
            _resident(pool_w.shape),
            pl.BlockSpec((1, POOL_WIDTH), lambda bi: (0, 0)),
        ],
        out_specs=pl.BlockSpec((None, l, POOL_WIDTH), lambda bi: (bi, 0, 0)),
        out_shape=jax.ShapeDtypeStruct((b, l, POOL_WIDTH), BF16),
        compiler_params=_params("parallel"),
        name="pool",
    )(z, pool_w, pool_scale)


def _query_chunks(nq, tq):
    return [(c0, min(tq, nq - c0)) for c0 in range(0, nq, tq)]


def _attn_kernel(*refs, n_lat, tq, lam0, heads):
    if n_lat:
        lam_ref, g_ref, q_ref, kc_ref, vtc_ref, kl_ref, vtl_ref, o_ref, p_sc = refs
    else:
        lam_ref, g_ref, q_ref, kc_ref, vtc_ref, o_ref, p_sc = refs
    lp = lam_ref[...]
    lam = (jnp.exp(jnp.sum(lp[0:1] * lp[1:2], axis=1, keepdims=True))
           - jnp.exp(jnp.sum(lp[2:3] * lp[3:4], axis=1, keepdims=True)) + lam0)
    nq = q_ref.shape[0]
    half = lax.broadcasted_iota(jnp.int32, (8, V_DIM), 1) // HEAD_DIM
    sel = (half == lax.broadcasted_iota(jnp.int32, (8, V_DIM), 0)).astype(BF16)
    nt = (((1,), (1,)), ((), ()))

    def sq_norm_max(a):
        return jnp.max(lax.dot_general(sel, a * a, nt, preferred_element_type=F32), axis=1, keepdims=True)

    def head_groups(hh):
        cs = slice(hh * V_DIM, (hh + 1) * V_DIM)
        groups = [(kc_ref[:, cs], vtc_ref[cs, :])]
        if n_lat:
            groups.append((kl_ref[:, cs], vtl_ref[cs, :]))
        return cs, groups

    bound = None
    for hh in range(heads):
        cs, groups = head_groups(hh)
        k_sq = sq_norm_max(groups[0][0])
        for k, _ in groups[1:]:
            k_sq = jnp.maximum(k_sq, sq_norm_max(k))
        b_h = sq_norm_max(q_ref[:, cs]) * k_sq
        bound = b_h if bound is None else jnp.maximum(bound, b_h)
    small = jnp.max(bound) * SQ_NORM_SLACK < MAX_UNSHIFTED_LOG2 ** 2

    def scores(shift, groups, cs, c0, n, slot):
        q = q_ref[c0:c0 + n, cs]
        lane = lax.broadcasted_iota(jnp.int32, q.shape, 1)
        sums = []
        for mi, first in enumerate((True, False)):
            qm = jnp.where((lane < HEAD_DIM) if first else (lane >= HEAD_DIM), q, jnp.zeros_like(q))
            s_t = [lax.dot_general(k, qm, nt, preferred_element_type=F32) for k, _ in groups]
            if shift:
                m = jnp.max(s_t[0], axis=0, keepdims=True)
                for s_g in s_t[1:]:
                    m = jnp.maximum(m, jnp.max(s_g, axis=0, keepdims=True))
                s_t = [s_g - m for s_g in s_t]
            l_m, row = None, 0
            for s_g in s_t:
                p_g = jnp.exp2(s_g)
                l_g = jnp.sum(p_g, axis=0, keepdims=True)
                l_m = l_g if l_m is None else l_m + l_g
                p_sc[slot, mi, row:row + p_g.shape[0], 0:n] = p_g.astype(BF16)
                row += p_g.shape[0]
            sums.append(l_m)
        return sums

    def values(groups, cs, c0, n, slot, sums):
        r = (lam * sums[0] * (1.0 / sums[1])).astype(BF16)
        o_t, row = None, 0
        for k, vt in groups:
            rows = slice(row, row + k.shape[0])
            part = jnp.dot(vt, p_sc[slot, 0, rows, 0:n] - r * p_sc[slot, 1, rows, 0:n], preferred_element_type=F32)
            o_t = part if o_t is None else o_t + part
            row += k.shape[0]
        o_t = o_t * (1.0 / sums[0])
        o_t = o_t * lax.rsqrt(jnp.mean(o_t * o_t, axis=0, keepdims=True) + EPS)
        o_t = o_t * (g_ref[...] * (1.0 - lam0))
        o_ref[c0:c0 + n, cs] = o_t.T.astype(BF16)

    def attend(shift):
        work = [(hh, c0, n) for hh in range(heads) for c0, n in _query_chunks(nq, tq)]
        pending = None
        for i, (hh, c0, n) in enumerate(work):
            cs, groups = head_groups(hh)
            sums = scores(shift, groups, cs, c0, n, i % 2)
            if pending is not None:
                values(*pending)
            pending = (groups, cs, c0, n, i % 2, sums)
        values(*pending)

    lax.cond(small, lambda: attend(False), lambda: attend(True))


def _attention(lam_p, g_col, zq, q_blk, zc, kc_blk, vtc, zl, kl_blk, vtl, lam0, tq, heads):
    b, nq, _ = zq.shape
    n_ctx = zc.shape[1]
    n_lat = 0 if zl is None else zl.shape[1]
    w = heads * V_DIM

    def col(blk, n):
        return pl.BlockSpec((None, n, w), lambda bi, h: (bi, 0, blk + h))

    def vt_rows(n):
        return pl.BlockSpec((None, w, n), lambda bi, h: (bi, h, 0))

    in_specs = [
        pl.BlockSpec(lam_p.shape, lambda bi, h: (0, 0)),
        pl.BlockSpec(g_col.shape, lambda bi, h: (0, 0)),
        col(q_blk, nq), col(kc_blk, n_ctx), vt_rows(n_ctx),
    ]
    args = [lam_p, g_col, zq, zc, vtc]
    if n_lat:
        in_specs += [col(kl_blk, n_lat), vt_rows(n_lat)]
        args += [zl, vtl]
    tq = min(tq, nq)
    return pl.pallas_call(
        functools.partial(_attn_kernel, n_lat=n_lat, tq=tq, lam0=lam0, heads=heads),
        grid=(b, N_HEADS // heads),
        in_specs=in_specs,
        out_specs=pl.BlockSpec((None, nq, w), lambda bi, h: (bi, 0, h)),
        out_shape=jax.ShapeDtypeStruct((b, nq, QKV_WIDTH), BF16),
        scratch_shapes=[pltpu.VMEM((2, 2, n_ctx + n_lat, tq), BF16)],
        compiler_params=_params("parallel", "parallel"),
        name="attention",
    )(*args)


def _merge_kernel(*refs, plan):
    if plan:
        (x_ref, mod_ref, g2_ref, zg_ref, py_ref, ay_ref, wp_ref, wa_ref, wo_ref, rh_ref, rl_ref,
         xo_ref, h2_ref, prow_ref, pcol_ref, cnt_ref) = refs
    else:
        x_ref, mod_ref, g2_ref, zg_ref, py_ref, ay_ref, wp_ref, wa_ref, wo_ref, xo_ref, h2_ref = refs
    d = x_ref.shape[1]
    g_pool = _sigmoid(zg_ref[:, 0:d].astype(F32))
    g_attn = _sigmoid(zg_ref[:, d:2 * d].astype(F32))
    y = (g_pool * jnp.dot(py_ref[...], wp_ref[...], preferred_element_type=F32)
         + g_attn * jnp.dot(ay_ref[...], wa_ref[...], preferred_element_type=F32))
    o = jnp.dot(y.astype(BF16), wo_ref[...], preferred_element_type=F32)
    xn = x_ref[...] + mod_ref[2:3, :] * o
    xo_ref[...] = xn
    h2 = _rms(xn) * g2_ref[...] * (1.0 + mod_ref[4:5, :]) + mod_ref[3:4, :]
    h2_hi = h2.astype(BF16)
    h2_ref[...] = h2_hi
    if plan:
        h2_lo = (h2 - h2_hi.astype(F32)).astype(BF16)
        rows, cnt = _route_plan(h2_hi, h2_lo, rh_ref[...], rl_ref[...])
        prow_ref[...] = rows
        pcol_ref[...] = jnp.concatenate([rows, jnp.zeros((LANES - 8, rows.shape[1]), F32)], axis=0).T
        cnt_ref[...] = jnp.broadcast_to(cnt, (N_EXPERTS, LANES))


def _merge(x, mod, mod_row, g2, z, pool_y, attn_y, wp, wa, wo, router_w, t):
    b, l, d = x.shape
    plan = router_w is not None
    nt = l // t
    tile = lambda w: pl.BlockSpec((None, t, w), lambda bi, i: (bi, i, 0))
    in_specs = [
        tile(d),
        pl.BlockSpec((None, 6, d), lambda bi, i: (mod_row(bi), 0, 0)),
        pl.BlockSpec((1, d), lambda bi, i: (0, 0)),
        tile(2 * d),
        tile(POOL_WIDTH), tile(QKV_WIDTH),
        _resident(wp.shape), _resident(wa.shape), _resident(wo.shape),
    ]
    args = [x, mod, g2, z, pool_y, attn_y, wp, wa, wo]
    out_specs = [tile(d), tile(d)]
    out_shape = [jax.ShapeDtypeStruct((b, l, d), F32), jax.ShapeDtypeStruct((b, l, d), BF16)]
    if plan:
        assert t == MOE_TG
        in_specs += [_resident(router_w[0].shape)] * 2
        args += list(router_w)
        out_specs += [pl.BlockSpec((None, 8, t), lambda bi, i: (bi * nt + i, 0, 0)),
                      tile(LANES),
                      pl.BlockSpec((None, N_EXPERTS, LANES), lambda bi, i: (bi * nt + i, 0, 0))]
        out_shape += [jax.ShapeDtypeStruct((b * nt, 8, t), F32),
                      jax.ShapeDtypeStruct((b, l, LANES), F32),
                      jax.ShapeDtypeStruct((b * nt, N_EXPERTS, LANES), F32)]
    return pl.pallas_call(
        functools.partial(_merge_kernel, plan=plan),
        grid=(b, nt),
        in_specs=in_specs,
        out_specs=out_specs,
        out_shape=out_shape,
        compiler_params=_params("parallel", "parallel"),
        name="merge",
    )(*args)


def _swiglu(h, w1, w3, w2):
    a = jnp.dot(h, w1, preferred_element_type=F32)
    b = jnp.dot(h, w3, preferred_element_type=F32)
    return jnp.dot((a * _sigmoid(a) * b).astype(BF16), w2, preferred_element_type=F32)


def _ffn_kernel(*refs, final):
    if final:
        x_ref, h2_ref, mod_ref, w1_ref, w3_ref, w2_ref, fg_ref, o_ref = refs
    else:
        x_ref, h2_ref, mod_ref, w1_ref, w3_ref, w2_ref, o_ref = refs
    y = _swiglu(h2_ref[...], w1_ref[...], w3_ref[...], w2_ref[...])
    xn = x_ref[...] + mod_ref[5:6, :] * y
    if final:
        xn = _rms(xn) * fg_ref[...]
    o_ref[...] = xn


def _ffn(x, h2, mod, mod_row, w1, w3, w2, final_g, t):
    b, l, d = x.shape
    final = final_g is not None
    tile = pl.BlockSpec((None, t, d), lambda bi, i: (bi, i, 0))
    in_specs = [tile, tile, pl.BlockSpec((None, 6, d), lambda bi, i: (mod_row(bi), 0, 0)),
                _resident(w1.shape), _resident(w3.shape), _resident(w2.shape)]
    args = [x, h2, mod, w1, w3, w2]
    if final:
        in_specs.append(pl.BlockSpec((1, d), lambda bi, i: (0, 0)))
        args.append(final_g)
    return pl.pallas_call(
        functools.partial(_ffn_kernel, final=final),
        grid=(b, l // t),
        in_specs=in_specs,
        out_specs=tile,
        out_shape=jax.ShapeDtypeStruct((b, l, d), F32),
        compiler_params=_params("parallel", "parallel"),
        name="ffn",
    )(*args)


MOE_TG = 512
MOE_TM = 1024
PIECE = 16
BIG_PIECE = 64
MOE_SLOTS = 2 * MOE_TG + N_EXPERTS * PIECE
ROUTER_ROWS = 16


def _route_plan(h2_hi, h2_lo, rh, rl):
    tg = h2_hi.shape[0]
    nt = (((1,), (1,)), ((), ()))
    logits = (lax.dot_general(rh, h2_hi, nt, preferred_element_type=F32)
              + lax.dot_general(rl, h2_hi, nt, preferred_element_type=F32)
              + lax.dot_general(rh, h2_lo, nt, preferred_element_type=F32))
    sub = lax.broadcasted_iota(jnp.int32, logits.shape, 0).astype(F32)
    neg = jnp.float32(-jnp.inf)
    logits = jnp.where(sub < N_EXPERTS, logits, neg)
    v1 = jnp.max(logits, axis=0, keepdims=True)
    i1 = jnp.min(jnp.where(logits == v1, sub, float(ROUTER_ROWS)), axis=0, keepdims=True)
    rest = jnp.where(sub == i1, neg, logits)
    v2 = jnp.max(rest, axis=0, keepdims=True)
    i2 = jnp.min(jnp.where(rest == v2, sub, float(ROUTER_ROWS)), axis=0, keepdims=True)
    e2 = jnp.exp(v2 - v1)
    w1 = 1.0 / (1.0 + e2)
    w2 = e2 * w1
    two = w2 != 0.0
    sel = ((sub == i1) | ((sub == i2) & two)).astype(F32)
    before = (lax.broadcasted_iota(jnp.int32, (tg, tg), 0)
              < lax.broadcasted_iota(jnp.int32, (tg, tg), 1)).astype(BF16)
    rank = jnp.dot(sel.astype(BF16), before, preferred_element_type=F32)
    cnt = jnp.sum(sel, axis=1, keepdims=True)
    cpad = jnp.ceil(cnt * (1.0 / PIECE)) * PIECE
    subc = sub[:, 0:1]
    lbase = jnp.zeros_like(cpad)
    for e in range(N_EXPERTS - 1):
        c_e = jnp.sum(jnp.where(subc == e, cpad, 0.0), axis=0, keepdims=True)
        lbase = lbase + jnp.where(subc > e, c_e, 0.0)
    slot = lbase + rank
    e_a = jnp.where(two, jnp.minimum(i1, i2), i1)
    e_b = jnp.where(two, jnp.maximum(i1, i2), i1)
    ls_a = jnp.sum(jnp.where(sub == e_a, slot, 0.0), axis=0, keepdims=True)
    ls_b = jnp.sum(jnp.where(sub == e_b, slot, 0.0), axis=0, keepdims=True)
    w_a = jnp.where(e_a == i1, w1, w2)
    w_b = jnp.where(two, jnp.where(e_b == i1, w1, w2), 0.0)
    rows = jnp.concatenate([ls_a, ls_b, w_a, w_b, jnp.zeros((4, tg), F32)], axis=0)
    return rows, cnt[0:N_EXPERTS]


def _segment_copies(i, seg, local_buf, sorted_hbm, sem, to_sorted, action):
    lb_ref, gb_ref, nb_ref, ns_ref = seg
    for e in range(N_EXPERTS):
        k = i * N_EXPERTS + e

        def copy(off, rows, k=k):
            lo = pl.multiple_of(lb_ref[k] + off, PIECE)
            go = pl.multiple_of(gb_ref[k] + off, PIECE)
            local = local_buf.at[pl.ds(lo, rows), :]
            remote = sorted_hbm.at[pl.ds(go, rows), :]
            c = pltpu.make_async_copy(local, remote, sem) if to_sorted else pltpu.make_async_copy(remote, local, sem)
            getattr(c, action)()

        def big(j, carry, copy=copy):
            copy(j * BIG_PIECE, BIG_PIECE)
            return carry

        def small(j, carry, copy=copy, k=k):
            copy(nb_ref[k] * BIG_PIECE + j * PIECE, PIECE)
            return carry

        lax.fori_loop(0, nb_ref[k], big, 0)
        lax.fori_loop(0, ns_ref[k], small, 0)


def _zero_fill(fill_ref, zbuf, out_ref, sem, action):
    def copy(go, rows):
        c = pltpu.make_async_copy(zbuf.at[pl.ds(0, rows), :], out_ref.at[pl.ds(go, rows), :], sem)
        getattr(c, action)()

    for e in range(N_EXPERTS):
        def tail(j, carry, e=e):
            copy(pl.multiple_of(fill_ref[e] + j * PIECE, PIECE), PIECE)
            return carry

        lax.fori_loop(0, fill_ref[N_EXPERTS + e], tail, 0)

    def rest(j, carry):
        copy(pl.multiple_of(fill_ref[2 * N_EXPERTS] + j * MOE_TM, MOE_TM), MOE_TM)
        return carry

    lax.fori_loop(0, fill_ref[2 * N_EXPERTS + 1], rest, 0)


def _dispatch_kernel(lb_ref, gb_ref, nb_ref, ns_ref, fill_ref, h2_ref, prow_ref, out_ref, cbuf, zbuf, sems):
    seg = (lb_ref, gb_ref, nb_ref, ns_ref)
    i = pl.program_id(0)
    nt = pl.num_programs(0)
    slot = i % 2
    tg = h2_ref.shape[0]

    @pl.when(i >= 2)
    def _():
        _segment_copies(i - 2, seg, cbuf.at[slot], out_ref, sems.at[slot], True, "wait")

    s = lax.broadcasted_iota(jnp.int32, (MOE_SLOTS, tg), 0).astype(F32)
    onehot = jnp.where(s == prow_ref[0:1, :], 1.0, jnp.where(s == prow_ref[1:2, :], 1.0, 0.0)).astype(BF16)
    cbuf[slot] = jnp.dot(onehot, h2_ref[...], preferred_element_type=F32).astype(BF16)
    _segment_copies(i, seg, cbuf.at[slot], out_ref, sems.at[slot], True, "start")

    @pl.when(i == nt - 1)
    def _():
        @pl.when(i >= 1)
        def _():
            _segment_copies(i - 1, seg, cbuf.at[1 - slot], out_ref, sems.at[1 - slot], True, "wait")

        _segment_copies(i, seg, cbuf.at[slot], out_ref, sems.at[slot], True, "wait")
        zbuf[...] = jnp.zeros_like(zbuf)
        _zero_fill(fill_ref, zbuf, out_ref, sems.at[2], "start")
        _zero_fill(fill_ref, zbuf, out_ref, sems.at[2], "wait")


def _dispatch(seg, fill, h2, prow, rows, tg):
    n, d = h2.shape
    grid_spec = pltpu.PrefetchScalarGridSpec(
        num_scalar_prefetch=len(seg) + 1,
        grid=(n // tg,),
        in_specs=[
            pl.BlockSpec((tg, d), lambda i, *_: (i, 0)),
            pl.BlockSpec((None, 8, tg), lambda i, *_: (i, 0, 0)),
        ],
        out_specs=pl.BlockSpec(memory_space=pl.ANY),
        scratch_shapes=[pltpu.VMEM((2, MOE_SLOTS, d), BF16), pltpu.VMEM((MOE_TM, d), BF16),
                        pltpu.SemaphoreType.DMA((3,))],
    )
    return pl.pallas_call(
        _dispatch_kernel,
        grid_spec=grid_spec,
        out_shape=jax.ShapeDtypeStruct((rows, d), BF16),
        compiler_params=_params("arbitrary"),
        name="moe_dispatch",
    )(*seg, fill, h2, prow)


def _group_kernel(te_ref, nu_ref, x_ref, w13_ref, w2_ref, o_ref):
    del te_ref
    live = pl.program_id(0) < nu_ref[0]

    @pl.when(live)
    def _():
        f = w2_ref.shape[0]
        ab = jnp.dot(x_ref[...], w13_ref[...], preferred_element_type=F32)
        a, b = ab[:, 0:f], ab[:, f:2 * f]
        o_ref[...] = jnp.dot((a * _sigmoid(a) * b).astype(BF16), w2_ref[...],
                             preferred_element_type=F32).astype(BF16)

    @pl.when(jnp.logical_not(live))
    def _():
        o_ref[...] = jnp.zeros_like(o_ref)


def _group(tile_expert, n_used, xs, w13, w2, tm):
    rows, d = xs.shape
    f = w2.shape[1]
    used = lambda r, te, nu: (jnp.minimum(r, nu[0] - 1), 0)
    grid_spec = pltpu.PrefetchScalarGridSpec(
        num_scalar_prefetch=2,
        grid=(rows // tm,),
        in_specs=[
            pl.BlockSpec((tm, d), used),
            pl.BlockSpec((None, d, 2 * f), lambda r, te, nu: (te[r], 0, 0)),
            pl.BlockSpec((None, f, d), lambda r, te, nu: (te[r], 0, 0)),
        ],
        out_specs=pl.BlockSpec((tm, d), lambda r, te, nu: (r, 0)),
    )
    return pl.pallas_call(
        _group_kernel,
        grid_spec=grid_spec,
        out_shape=jax.ShapeDtypeStruct((rows, d), BF16),
        compiler_params=_params("arbitrary"),
        name="moe_group",
    )(tile_expert, n_used, xs, w13, w2)


def _combine_kernel(*refs, final):
    if final:
        lb_ref, gb_ref, nb_ref, ns_ref, x_ref, pcol_ref, mod_ref, fg_ref, y_ref, o_ref, ybuf, sems = refs
    else:
        lb_ref, gb_ref, nb_ref, ns_ref, x_ref, pcol_ref, mod_ref, y_ref, o_ref, ybuf, sems = refs
    seg = (lb_ref, gb_ref, nb_ref, ns_ref)
    i = pl.program_id(0)
    nt = pl.num_programs(0)
    slot = i % 2
    tg = x_ref.shape[0]

    @pl.when(i == 0)
    def _():
        ybuf[...] = jnp.zeros_like(ybuf)
        _segment_copies(i, seg, ybuf.at[slot], y_ref, sems.at[slot], False, "start")

    @pl.when(i + 1 < nt)
    def _():
        _segment_copies(i + 1, seg, ybuf.at[1 - slot], y_ref, sems.at[1 - slot], False, "start")

    s = lax.broadcasted_iota(jnp.int32, (tg, MOE_SLOTS), 1).astype(F32)
    pc = pcol_ref[...]
    scatter = jnp.where(s == pc[:, 0:1], pc[:, 2:3], jnp.where(s == pc[:, 1:2], pc[:, 3:4], 0.0)).astype(BF16)
    _segment_copies(i, seg, ybuf.at[slot], y_ref, sems.at[slot], False, "wait")
    y = jnp.dot(scatter, ybuf[slot], preferred_element_type=F32)
    xn = x_ref[...] + mod_ref[5:6, :] * y
    if final:
        xn = _rms(xn) * fg_ref[...]
    o_ref[...] = xn


def _combine(seg, x2, pcol, mod, mod_row, final_g, ys, tg):
    n, d = x2.shape
    final = final_g is not None
    in_specs = [
        pl.BlockSpec((tg, d), lambda i, *_: (i, 0)),
        pl.BlockSpec((tg, LANES), lambda i, *_: (i, 0)),
        pl.BlockSpec((None, 6, d), lambda i, *_: (mod_row(i), 0, 0)),
    ]
    args = [x2, pcol, mod]
    if final:
        in_specs.append(pl.BlockSpec((1, d), lambda i, *_: (0, 0)))
        args.append(final_g)
    in_specs.append(pl.BlockSpec(memory_space=pl.ANY))
    args.append(ys)
    grid_spec = pltpu.PrefetchScalarGridSpec(
        num_scalar_prefetch=len(seg),
        grid=(n // tg,),
        in_specs=in_specs,
        out_specs=pl.BlockSpec((tg, d), lambda i, *_: (i, 0)),
        scratch_shapes=[pltpu.VMEM((2, MOE_SLOTS, d), BF16), pltpu.SemaphoreType.DMA((2,))],
    )
    return pl.pallas_call(
        functools.partial(_combine_kernel, final=final),
        grid_spec=grid_spec,
        out_shape=jax.ShapeDtypeStruct((n, d), F32),
        compiler_params=_params("arbitrary"),
        name="moe_combine",
    )(*seg, *args)


def _routed_moe(x, mod, batch_row, w13, w2, h2, plan, final_g):
    b, l, d = x.shape
    n = b * l
    tg, tm = MOE_TG, MOE_TM
    nt = n // tg
    x2, h22 = x.reshape(n, d), h2.reshape(n, d)
    mod_row = lambda i: batch_row((i * tg) // l)
    prow, pcol, cnt = plan
    pcol = pcol.reshape(n, LANES)

    cnt = cnt[:, :, 0].astype(jnp.int32)
    cpad = (cnt + PIECE - 1) // PIECE * PIECE
    lbase = jnp.cumsum(cpad, axis=1) - cpad
    tot = (jnp.sum(cpad, axis=0) + tm - 1) // tm * tm
    ends = jnp.cumsum(tot)
    gbase = (ends - tot)[None, :] + jnp.cumsum(cpad, axis=0) - cpad
    rows = (2 * n + nt * N_EXPERTS * (PIECE - 1) + N_EXPERTS * (tm - 1) + tm - 1) // tm * tm
    n_used = (ends[-1] // tm).astype(jnp.int32)
    tile_start = jnp.arange(rows // tm, dtype=jnp.int32) * tm
    tile_expert = jnp.sum(jnp.minimum(tile_start, ends[-1] - 1)[:, None] >= ends[None, :], axis=1).astype(jnp.int32)
    seg = tuple(a.reshape(-1).astype(jnp.int32)
                for a in (lbase, gbase, cpad // BIG_PIECE, cpad % BIG_PIECE // PIECE))
    data_end = ends - tot + jnp.sum(cpad, axis=0)
    fill = jnp.concatenate([data_end, (ends - data_end) // PIECE, ends[-1:], (rows - ends[-1:]) // tm]).astype(jnp.int32)

    xs = _dispatch(seg, fill, h22, prow, rows, tg)
    ys = _group(tile_expert, n_used.reshape(1), xs, w13, w2, tm)
    out = _combine(seg, x2, pcol, mod, mod_row, final_g, ys, tg)
    return out.reshape(b, l, d)


def _rope_tables(l):
    rows = l // GRID_W
    row = jnp.repeat(jnp.arange(rows, dtype=F32), GRID_W)
    colp = jnp.tile(jnp.arange(GRID_W, dtype=F32), rows)
    inv = ROPE_THETA ** (-jnp.arange(ROPE_PAIRS, dtype=F32) * 2.0 / AXIS_DIM)
    ang_r, ang_c = row[:, None] * inv, colp[:, None] * inv
    zero = jnp.zeros_like(ang_r)
    cos64 = jnp.concatenate([jnp.cos(ang_r)] * 2 + [jnp.cos(ang_c)] * 2, axis=1)
    up64 = jnp.concatenate([-jnp.sin(ang_r), zero, -jnp.sin(ang_c), zero], axis=1)
    dn64 = jnp.concatenate([zero, jnp.sin(ang_r), zero, jnp.sin(ang_c)], axis=1)
    return tuple(jnp.tile(a, (1, LANES // HEAD_DIM)) for a in (cos64, up64, dn64))


def _lambda_init(layer):
    return 0.8 - 0.6 * math.exp(-0.3 * layer)


def _permute_in(w):
    off_q = POOL_WIDTH
    off_g = off_q + 3 * QKV_WIDTH
    return jnp.concatenate([w[:, off_g:], w[:, off_q:off_g], w[:, :off_q]], axis=1).astype(BF16)


def kernel(x, c, ctx, c_ctx, w_mod, b_mod, norm1_g, norm2_g, w_in, pool_w, pool_scale, lam_q1, lam_k1,
           lam_q2, lam_k2, subln_g, w_pool_proj, w_attn_proj, w_out, ffn_w1, ffn_w3, ffn_w2, router_w,
           moe_w1, moe_w3, moe_w2, final_g):
    b, l, d = x.shape
    n_ctx = ctx.shape[1]
    depth = w_mod.shape[0]
    assert d == D_MODEL and b + 1 <= MOD_ROWS and l % 512 == 0 and n_ctx % 256 == 0

    s_in = jnp.concatenate([c, c_ctx[None, :], jnp.zeros((MOD_ROWS - b - 1, d), F32)], axis=0)
    mod_all = _modulation(s_in, w_mod, b_mod)
    tables = _rope_tables(l)
    big_tile = 1024 if l % 1024 == 0 else 512
    lat_row = lambda bi: bi
    ctx_row = lambda bi: b
    gq, gk, gp = COL_Q // V_DIM, COL_K // V_DIM, COL_P // POOL_WIDTH
    full_kinds = ("g",) * 4 + ("q",) * 2 + ("k",) * 2 + ("v",) * 2 + ("p",)

    xc = ctx
    for layer in range(depth):
        last = layer == depth - 1
        lam0 = _lambda_init(layer)
        mod = mod_all[layer].reshape(MOD_ROWS, 6, d)
        g1 = norm1_g[layer][None, :]
        g2 = norm2_g[layer][None, :]
        w_in_p = _permute_in(w_in[layer])
        lam_p = jnp.stack([lam_q1[layer], lam_k1[layer], lam_q2[layer], lam_k2[layer]])
        g_col = subln_g[layer][:, None]
        pw = pool_w[layer].astype(BF16)
        ps = pool_scale[layer][None, :]
        wp = w_pool_proj[layer].astype(BF16)
        wa = w_attn_proj[layer].astype(BF16)
        wo = w_out[layer].astype(BF16)
        fg = final_g[None, :] if last else None
        moe_layer = layer % 2 == 1
        assert last or not moe_layer, "context tokens are only carried through dense layers"
        if moe_layer:
            rw = jnp.pad(router_w[layer // 2].T, ((0, ROUTER_ROWS - N_EXPERTS), (0, 0)))
            rw_hi = rw.astype(BF16)
            routers = (rw_hi, (rw - rw_hi.astype(F32)).astype(BF16))
            we13 = jnp.concatenate([moe_w1[layer // 2], moe_w3[layer // 2]], axis=2).astype(BF16)
            we2 = moe_w2[layer // 2].astype(BF16)
        else:
            routers = None
            wf1 = ffn_w1[layer // 2].astype(BF16)
            wf3 = ffn_w3[layer // 2].astype(BF16)
            wf2 = ffn_w2[layer // 2].astype(BF16)

        z, vt = _inproj(x, mod, lat_row, g1, w_in_p, full_kinds, tables, big_tile)
        if last:
            zc, vtc = _inproj(xc, mod, ctx_row, g1, w_in_p[:, COL_K:COL_V + QKV_WIDTH], ("k",) * 2 + ("v",) * 2,
                              None, n_ctx)
            ckb = 0
        else:
            zc, vtc = _inproj(xc, mod, ctx_row, g1, w_in_p, full_kinds, None, n_ctx)
            ckb = gk
        attn_y = _attention(lam_p, g_col, z, gq, zc, ckb, vtc, z, gk, vt, lam0, 512, 1)
        pool_y = _pool(z, gp, pw, ps)
        x, h2, *plan = _merge(x, mod, lat_row, g2, z, pool_y, attn_y, wp, wa, wo, routers,
                              MOE_TG if moe_layer else big_tile)

        if not last:
            attn_yc = _attention(lam_p, g_col, zc, COL_Q // QKV_WIDTH, zc, COL_K // QKV_WIDTH, vtc, None, 0, None,
                                 lam0, 512, N_HEADS)
            pool_yc = _pool(zc, gp, pw, ps)
            xc, h2c = _merge(xc, mod, ctx_row, g2, zc, pool_yc, attn_yc, wp, wa, wo, None, n_ctx)

        if moe_layer:
            x = _routed_moe(x, mod, lat_row, we13, we2, h2, plan, fg)
        else:
            x = _ffn(x, h2, mod, lat_row, wf1, wf3, wf2, fg, 512)
        if not last:
            xc = _ffn(xc, h2c, mod, ctx_row, wf1, wf3, wf2, fg, n_ctx)
    return x
```

```python
import functools
import math

import jax
import jax.numpy as jnp
from jax import lax
from jax.experimental import pallas as pl
from jax.experimental.pallas import tpu as pltpu

F32 = jnp.float32
BF16 = jnp.bfloat16

D_MODEL = 1024
EPS = 1e-6
GRID_W = 64
N_HEADS = 8
HEAD_DIM = 64
V_DIM = 2 * HEAD_DIM
ROPE_THETA = 10000.0
AXIS_DIM = HEAD_DIM // 2
ROPE_PAIRS = AXIS_DIM // 2
POOL_WINDOWS = (2, 4, 8, 16)
POOL_WIDTH = 512
POOL_GROUP_DIM = POOL_WIDTH // len(POOL_WINDOWS)
POOL_EDGE = max(POOL_WINDOWS) // 2
N_EXPERTS = 8
Q_SCALE = HEAD_DIM ** -0.5
LOG2E = math.log2(math.e)
MAX_UNSHIFTED_LOG2 = 80.0
SQ_NORM_SLACK = 1.05

QKV_WIDTH = N_HEADS * V_DIM
COL_G = 0
COL_Q = COL_G + 2 * D_MODEL
COL_K = COL_Q + QKV_WIDTH
COL_V = COL_K + QKV_WIDTH
IN_WIDTH = COL_V + QKV_WIDTH + POOL_WIDTH
COL_P = COL_V
IN_CHUNK = 512

LANES = 128
MOD_ROWS = 40

VMEM_LIMIT = 56 * 1024 * 1024


def _resident(shape):
    nd = len(shape)
    return pl.BlockSpec(shape, lambda *_: (0,) * nd, pipeline_mode=pl.Buffered(1))


def _params(*sem):
    return pltpu.CompilerParams(dimension_semantics=sem, vmem_limit_bytes=VMEM_LIMIT)


def _sigmoid(v):
    return 1.0 / (1.0 + jnp.exp(-v))


def _rms(v):
    return v * lax.rsqrt(jnp.mean(v * v, axis=-1, keepdims=True) + EPS)


def _mod_kernel(s_ref, w_ref, b_ref, o_ref):
    s = s_ref[...]
    s = s * _sigmoid(s)
    w = w_ref[...]
    s_hi = s.astype(BF16)
    s_lo = (s - s_hi.astype(F32)).astype(BF16)
    w_hi = w.astype(BF16)
    w_lo = (w - w_hi.astype(F32)).astype(BF16)
    acc = jnp.dot(s_hi, w_hi, preferred_element_type=F32)
    acc += jnp.dot(s_hi, w_lo, preferred_element_type=F32)
    acc += jnp.dot(s_lo, w_hi, preferred_element_type=F32)
    o_ref[...] = acc + b_ref[...]


def _modulation(s_in, w_mod, b_mod):
    depth, d, n = w_mod.shape
    tn = 1024
    return pl.pallas_call(
        _mod_kernel,
        grid=(depth, n // tn),
        in_specs=[
            pl.BlockSpec((MOD_ROWS, d), lambda l, j: (0, 0)),
            pl.BlockSpec((None, d, tn), lambda l, j: (l, 0, j)),
            pl.BlockSpec((None, 1, tn), lambda l, j: (l, 0, j)),
        ],
        out_specs=pl.BlockSpec((None, MOD_ROWS, tn), lambda l, j: (l, 0, j)),
        out_shape=jax.ShapeDtypeStruct((depth, MOD_ROWS, n), F32),
        compiler_params=_params("parallel", "parallel"),
        name="modulation",
    )(s_in, w_mod, b_mod.reshape(depth, 1, n))


def _rope(z, c, s_up, s_dn):
    up = pltpu.roll(z, LANES - ROPE_PAIRS, 1)
    dn = pltpu.roll(z, ROPE_PAIRS, 1)
    return z * c + up * s_up + dn * s_dn


def _inproj_kernel(*refs, kinds, rope):
    if rope:
        x_ref, mod_ref, g_ref, w_ref, c_ref, su_ref, sd_ref, z_ref, vt_ref = refs
    else:
        x_ref, mod_ref, g_ref, w_ref, z_ref, vt_ref = refs
    y = _rms(x_ref[...]) * g_ref[...]
    h = (y * (1.0 + mod_ref[1:2, :]) + mod_ref[0:1, :]).astype(BF16)
    z_col, head = 0, 0
    for ci, kind in enumerate(kinds):
        lo = ci * IN_CHUNK
        z = jnp.dot(h, w_ref[:, lo:lo + IN_CHUNK], preferred_element_type=F32)
        if kind == "v":
            for j in range(0, IN_CHUNK, V_DIM):
                vt_ref[head * V_DIM:(head + 1) * V_DIM, :] = z[:, j:j + V_DIM].T.astype(BF16)
                head += 1
            continue
        if rope and kind in ("q", "k"):
            c, su, sd = c_ref[...], su_ref[...], sd_ref[...]
            z = jnp.concatenate(
                [_rope(z[:, j:j + LANES], c, su, sd) for j in range(0, IN_CHUNK, LANES)], axis=1)
        if kind == "q":
            z = z * (Q_SCALE * LOG2E)
        z_ref[:, z_col:z_col + IN_CHUNK] = z.astype(BF16)
        z_col += IN_CHUNK


def _inproj(x, mod, mod_row, g, w, kinds, tables, t):
    b, l, d = x.shape
    n_v = sum(k == "v" for k in kinds)
    wz = IN_CHUNK * (len(kinds) - n_v)
    rows_vt = n_v * IN_CHUNK
    rope = tables is not None
    in_specs = [
        pl.BlockSpec((None, t, d), lambda bi, i: (bi, i, 0)),
        pl.BlockSpec((None, 6, d), lambda bi, i: (mod_row(bi), 0, 0)),
        pl.BlockSpec((1, d), lambda bi, i: (0, 0)),
        _resident((d, IN_CHUNK * len(kinds))),
    ]
    args = [x, mod, g, w]
    if rope:
        in_specs += [pl.BlockSpec((t, LANES), lambda bi, i: (i, 0))] * 3
        args += list(tables)
    return pl.pallas_call(
        functools.partial(_inproj_kernel, kinds=kinds, rope=rope),
        grid=(b, l // t),
        in_specs=in_specs,
        out_specs=[pl.BlockSpec((None, t, wz), lambda bi, i: (bi, i, 0)),
                   pl.BlockSpec((None, rows_vt, t), lambda bi, i: (bi, 0, i))],
        out_shape=[jax.ShapeDtypeStruct((b, l, wz), BF16), jax.ShapeDtypeStruct((b, rows_vt, l), BF16)],
        compiler_params=_params("parallel", "parallel"),
        name="inproj",
    )(*args)


def _pool_kernel(u_ref, pw_ref, ps_ref, o_ref):
    l = u_ref.shape[0]
    e = POOL_EDGE
    t_head = lax.broadcasted_iota(jnp.int32, (e, POOL_GROUP_DIM), 0)
    t_tail = t_head + (l - e)

    def shifted(a, k):
        r = pltpu.roll(a, k % l, 0)
        if k > 0:
            return jnp.concatenate([jnp.where(t_head >= k, r[0:e], 0.0), r[e:]], axis=0)
        return jnp.concatenate([r[0:l - e], jnp.where(t_tail < l + k, r[l - e:], 0.0)], axis=0)

    def count(t, w):
        return (jnp.minimum(t + w // 2, l) - jnp.maximum(t - w // 2, 0)).astype(F32)

    for gi, w in enumerate(POOL_WINDOWS):
        lo = gi * POOL_GROUP_DIM
        u = u_ref[:, lo:lo + POOL_GROUP_DIM].astype(F32)
        back, fwd, span = u, u, 1
        while span < w // 2:
            back = back + shifted(back, span)
            fwd = fwd + shifted(fwd, -span)
            span *= 2
        win = shifted(back, 1) + fwd
        mean = jnp.concatenate([win[0:e] / count(t_head, w), win[e:l - e] * (1.0 / w),
                                win[l - e:] / count(t_tail, w)], axis=0)
        m = (mean - u).astype(BF16)
        y = jnp.dot(m, pw_ref[gi], preferred_element_type=F32)
        o_ref[:, lo:lo + POOL_GROUP_DIM] = (y * ps_ref[:, lo:lo + POOL_GROUP_DIM]).astype(BF16)


def _pool(z, col_block, pool_w, pool_scale):
    b, l, _ = z.shape
    return pl.pallas_call(
        _pool_kernel,
        grid=(b,),
        in_specs=[
            pl.BlockSpec((None, l, POOL_WIDTH), lambda bi: (bi, 0, col_block)),
            _resident(pool_w.shape),
            pl.BlockSpec((1, POOL_WIDTH), lambda bi: (0, 0)),
        ],
        out_specs=pl.BlockSpec((None, l, POOL_WIDTH), lambda bi: (bi, 0, 0)),
        out_shape=jax.ShapeDtypeStruct((b, l, POOL_WIDTH), BF16),
        compiler_params=_params("parallel"),
        name="pool",
    )(z, pool_w, pool_scale)


def _query_chunks(nq, tq):
    return [(c0, min(tq, nq - c0)) for c0 in range(0, nq, tq)]


def _attn_kernel(*refs, n_lat, tq, lam0, heads):
    if n_lat:
        lam_ref, g_ref, q_ref, kc_ref, vtc_ref, kl_ref, vtl_ref, o_ref, p_sc = refs
    else:
        lam_ref, g_ref, q_ref, kc_ref, vtc_ref, o_ref, p_sc = refs
    lp = lam_ref[...]
    lam = (jnp.exp(jnp.sum(lp[0:1] * lp[1:2], axis=1, keepdims=True))
           - jnp.exp(jnp.sum(lp[2:3] * lp[3:4], axis=1, keepdims=True)) + lam0)
    nq = q_ref.shape[0]
    half = lax.broadcasted_iota(jnp.int32, (8, V_DIM), 1) // HEAD_DIM
    sel = (half == lax.broadcasted_iota(jnp.int32, (8, V_DIM), 0)).astype(BF16)
    nt = (((1,), (1,)), ((), ()))

    def sq_norm_max(a):
        return jnp.max(lax.dot_general(sel, a * a, nt, preferred_element_type=F32), axis=1, keepdims=True)

    def head_groups(hh):
        cs = slice(hh * V_DIM, (hh + 1) * V_DIM)
        groups = [(kc_ref[:, cs], vtc_ref[cs, :])]
        if n_lat:
            groups.append((kl_ref[:, cs], vtl_ref[cs, :]))
        return cs, groups

    def scores(shift, groups, cs, c0, n, slot):
        q = q_ref[c0:c0 + n, cs]
        lane = lax.broadcasted_iota(jnp.int32, q.shape, 1)
        sums = []
        for mi, first in enumerate((True, False)):
            qm = jnp.where((lane < HEAD_DIM) if first else (lane >= HEAD_DIM), q, jnp.zeros_like(q))
            s_t = [lax.dot_general(k, qm, nt, preferred_element_type=F32) for k, _ in groups]
            if shift:
                m = jnp.max(s_t[0], axis=0, keepdims=True)
                for s_g in s_t[1:]:
                    m = jnp.maximum(m, jnp.max(s_g, axis=0, keepdims=True))
                s_t = [s_g - m for s_g in s_t]
            l_m, row = None, 0
            for s_g in s_t:
                p_g = jnp.exp2(s_g)
                l_g = jnp.sum(p_g, axis=0, keepdims=True)
                l_m = l_g if l_m is None else l_m + l_g
                p_sc[slot, mi, row:row + p_g.shape[0], 0:n] = p_g.astype(BF16)
                row += p_g.shape[0]
            sums.append(l_m)
        return sums

    def values(groups, cs, c0, n, slot, sums):
        r = (lam * sums[0] * (1.0 / sums[1])).astype(BF16)
        o_t, row = None, 0
        for k, vt in groups:
            rows = slice(row, row + k.shape[0])
            part = jnp.dot(vt, p_sc[slot, 0, rows, 0:n] - r * p_sc[slot, 1, rows, 0:n], preferred_element_type=F32)
            o_t = part if o_t is None else o_t + part
            row += k.shape[0]
        o_t = o_t * (1.0 / sums[0])
        o_t = o_t * lax.rsqrt(jnp.mean(o_t * o_t, axis=0, keepdims=True) + EPS)
        o_t = o_t * (g_ref[...] * (1.0 - lam0))
        o_ref[c0:c0 + n, cs] = o_t.T.astype(BF16)

    def attend(shift):
        work = [(hh, c0, n) for hh in range(heads) for c0, n in _query_chunks(nq, tq)]
        pending = None
        for i, (hh, c0, n) in enumerate(work):
            cs, groups = head_groups(hh)
            sums = scores(shift, groups, cs, c0, n, i % 2)
            if pending is not None:
                values(*pending)
            pending = (groups, cs, c0, n, i % 2, sums)
        values(*pending)

    attend(False)

    bound = None
    for hh in range(heads):
        cs, groups = head_groups(hh)
        k_sq = sq_norm_max(groups[0][0])
        for k, _ in groups[1:]:
            k_sq = jnp.maximum(k_sq, sq_norm_max(k))
        b_h = sq_norm_max(q_ref[:, cs]) * k_sq
        bound = b_h if bound is None else jnp.maximum(bound, b_h)
    small = jnp.max(bound) * SQ_NORM_SLACK < MAX_UNSHIFTED_LOG2 ** 2

    @pl.when(jnp.logical_not(small))
    def _():
        attend(True)


def _attention(lam_p, g_col, zq, q_blk, zc, kc_blk, vtc, zl, kl_blk, vtl, lam0, tq, heads):
    b, nq, _ = zq.shape
    n_ctx = zc.shape[1]
    n_lat = 0 if zl is None else zl.shape[1]
    w = heads * V_DIM

    def col(blk, n):
        return pl.BlockSpec((None, n, w), lambda bi, h: (bi, 0, blk + h))

    def vt_rows(n):
        return pl.BlockSpec((None, w, n), lambda bi, h: (bi, h, 0))

    in_specs = [
        pl.BlockSpec(lam_p.shape, lambda bi, h: (0, 0)),
        pl.BlockSpec(g_col.shape, lambda bi, h: (0, 0)),
        col(q_blk, nq), col(kc_blk, n_ctx), vt_rows(n_ctx),
    ]
    args = [lam_p, g_col, zq, zc, vtc]
    if n_lat:
        in_specs += [col(kl_blk, n_lat), vt_rows(n_lat)]
        args += [zl, vtl]
    tq = min(tq, nq)
    return pl.pallas_call(
        functools.partial(_attn_kernel, n_lat=n_lat, tq=tq, lam0=lam0, heads=heads),
        grid=(b, N_HEADS // heads),
        in_specs=in_specs,
        out_specs=pl.BlockSpec((None, nq, w), lambda bi, h: (bi, 0, h)),
        out_shape=jax.ShapeDtypeStruct((b, nq, QKV_WIDTH), BF16),
        scratch_shapes=[pltpu.VMEM((2, 2, n_ctx + n_lat, tq), BF16)],
        compiler_params=_params("parallel", "parallel"),
        name="attention",
    )(*args)


def _merge_kernel(*refs, plan):
    if plan:
        (x_ref, mod_ref, g2_ref, zg_ref, py_ref, ay_ref, wp_ref, wa_ref, wo_ref, rh_ref, rl_ref,
         xo_ref, h2_ref, prow_ref, pcol_ref, cnt_ref) = refs
    else:
        x_ref, mod_ref, g2_ref, zg_ref, py_ref, ay_ref, wp_ref, wa_ref, wo_ref, xo_ref, h2_ref = refs
    d = x_ref.shape[1]
    g_pool = _sigmoid(zg_ref[:, 0:d].astype(F32))
    g_attn = _sigmoid(zg_ref[:, d:2 * d].astype(F32))
    y = (g_pool * jnp.dot(py_ref[...], wp_ref[...], preferred_element_type=F32)
         + g_attn * jnp.dot(ay_ref[...], wa_ref[...], preferred_element_type=F32))
    o = jnp.dot(y.astype(BF16), wo_ref[...], preferred_element_type=F32)
    xn = x_ref[...] + mod_ref[2:3, :] * o
    xo_ref[...] = xn
    h2 = _rms(xn) * g2_ref[...] * (1.0 + mod_ref[4:5, :]) + mod_ref[3:4, :]
    h2_hi = h2.astype(BF16)
    h2_ref[...] = h2_hi
    if plan:
        h2_lo = (h2 - h2_hi.astype(F32)).astype(BF16)
        rows, cnt = _route_plan(h2_hi, h2_lo, rh_ref[...], rl_ref[...])
        prow_ref[...] = rows
        pcol_ref[...] = jnp.concatenate([rows, jnp.zeros((LANES - 8, rows.shape[1]), F32)], axis=0).T
        cnt_ref[...] = jnp.broadcast_to(cnt, (N_EXPERTS, LANES))


def _merge(x, mod, mod_row, g2, z, pool_y, attn_y, wp, wa, wo, router_w, t):
    b, l, d = x.shape
    plan = router_w is not None
    nt = l // t
    tile = lambda w: pl.BlockSpec((None, t, w), lambda bi, i: (bi, i, 0))
    in_specs = [
        tile(d),
        pl.BlockSpec((None, 6, d), lambda bi, i: (mod_row(bi), 0, 0)),
        pl.BlockSpec((1, d), lambda bi, i: (0, 0)),
        tile(2 * d),
        tile(POOL_WIDTH), tile(QKV_WIDTH),
        _resident(wp.shape), _resident(wa.shape), _resident(wo.shape),
    ]
    args = [x, mod, g2, z, pool_y, attn_y, wp, wa, wo]
    out_specs = [tile(d), tile(d)]
    out_shape = [jax.ShapeDtypeStruct((b, l, d), F32), jax.ShapeDtypeStruct((b, l, d), BF16)]
    if plan:
        assert t == MOE_TG
        in_specs += [_resident(router_w[0].shape)] * 2
        args += list(router_w)
        out_specs += [pl.BlockSpec((None, 8, t), lambda bi, i: (bi * nt + i, 0, 0)),
                      tile(LANES),
                      pl.BlockSpec((None, N_EXPERTS, LANES), lambda bi, i: (bi * nt + i, 0, 0))]
        out_shape += [jax.ShapeDtypeStruct((b * nt, 8, t), F32),
                      jax.ShapeDtypeStruct((b, l, LANES), F32),
                      jax.ShapeDtypeStruct((b * nt, N_EXPERTS, LANES), F32)]
    return pl.pallas_call(
        functools.partial(_merge_kernel, plan=plan),
        grid=(b, nt),
        in_specs=in_specs,
        out_specs=out_specs,
        out_shape=out_shape,
        compiler_params=_params("parallel", "parallel"),
        name="merge",
    )(*args)


def _swiglu(h, w1, w3, w2):
    a = jnp.dot(h, w1, preferred_element_type=F32)
    b = jnp.dot(h, w3, preferred_element_type=F32)
    return jnp.dot((a * _sigmoid(a) * b).astype(BF16), w2, preferred_element_type=F32)


def _ffn_kernel(*refs, final):
    if final:
        x_ref, h2_ref, mod_ref, w1_ref, w3_ref, w2_ref, fg_ref, o_ref = refs
    else:
        x_ref, h2_ref, mod_ref, w1_ref, w3_ref, w2_ref, o_ref = refs
    y = _swiglu(h2_ref[...], w1_ref[...], w3_ref[...], w2_ref[...])
    xn = x_ref[...] + mod_ref[5:6, :] * y
    if final:
        xn = _rms(xn) * fg_ref[...]
    o_ref[...] = xn


def _ffn(x, h2, mod, mod_row, w1, w3, w2, final_g, t):
    b, l, d = x.shape
    final = final_g is not None
    tile = pl.BlockSpec((None, t, d), lambda bi, i: (bi, i, 0))
    in_specs = [tile, tile, pl.BlockSpec((None, 6, d), lambda bi, i: (mod_row(bi), 0, 0)),
                _resident(w1.shape), _resident(w3.shape), _resident(w2.shape)]
    args = [x, h2, mod, w1, w3, w2]
    if final:
        in_specs.append(pl.BlockSpec((1, d), lambda bi, i: (0, 0)))
        args.append(final_g)
    return pl.pallas_call(
        functools.partial(_ffn_kernel, final=final),
        grid=(b, l // t),
        in_specs=in_specs,
        out_specs=tile,
        out_shape=jax.ShapeDtypeStruct((b, l, d), F32),
        compiler_params=_params("parallel", "parallel"),
        name="ffn",
    )(*args)


MOE_TG = 512
MOE_TM = 1024
PIECE = 16
BIG_PIECE = 64
MOE_SLOTS = 2 * MOE_TG + N_EXPERTS * PIECE
ROUTER_ROWS = 16


def _route_plan(h2_hi, h2_lo, rh, rl):
    tg = h2_hi.shape[0]
    nt = (((1,), (1,)), ((), ()))
    both = lax.dot_general(jnp.concatenate([rh, rl], axis=0), h2_hi, nt, preferred_element_type=F32)
    logits = (both[0:ROUTER_ROWS] + both[ROUTER_ROWS:2 * ROUTER_ROWS]
              + lax.dot_general(rh, h2_lo, nt, preferred_element_type=F32))
    sub = lax.broadcasted_iota(jnp.int32, logits.shape, 0).astype(F32)
    neg = jnp.float32(-jnp.inf)
    logits = jnp.where(sub < N_EXPERTS, logits, neg)
    v1 = jnp.max(logits, axis=0, keepdims=True)
    i1 = jnp.min(jnp.where(logits == v1, sub, float(ROUTER_ROWS)), axis=0, keepdims=True)
    rest = jnp.where(sub == i1, neg, logits)
    v2 = jnp.max(rest, axis=0, keepdims=True)
    i2 = jnp.min(jnp.where(rest == v2, sub, float(ROUTER_ROWS)), axis=0, keepdims=True)
    e2 = jnp.exp(v2 - v1)
    w1 = 1.0 / (1.0 + e2)
    w2 = e2 * w1
    two = w2 != 0.0
    sel = ((sub == i1) | ((sub == i2) & two)).astype(F32)
    before = (lax.broadcasted_iota(jnp.int32, (tg, tg), 0)
              < lax.broadcasted_iota(jnp.int32, (tg, tg), 1)).astype(BF16)
    rank = jnp.dot(sel.astype(BF16), before, preferred_element_type=F32)
    cnt = jnp.sum(sel, axis=1, keepdims=True)
    cpad = jnp.ceil(cnt * (1.0 / PIECE)) * PIECE
    subc = sub[:, 0:1]
    lbase = jnp.zeros_like(cpad)
    for e in range(N_EXPERTS - 1):
        c_e = jnp.sum(jnp.where(subc == e, cpad, 0.0), axis=0, keepdims=True)
        lbase = lbase + jnp.where(subc > e, c_e, 0.0)
    slot = lbase + rank
    e_a = jnp.where(two, jnp.minimum(i1, i2), i1)
    e_b = jnp.where(two, jnp.maximum(i1, i2), i1)
    ls_a = jnp.sum(jnp.where(sub == e_a, slot, 0.0), axis=0, keepdims=True)
    ls_b = jnp.sum(jnp.where(sub == e_b, slot, 0.0), axis=0, keepdims=True)
    w_a = jnp.where(e_a == i1, w1, w2)
    w_b = jnp.where(two, jnp.where(e_b == i1, w1, w2), 0.0)
    rows = jnp.concatenate([ls_a, ls_b, w_a, w_b, jnp.zeros((4, tg), F32)], axis=0)
    return rows, cnt[0:N_EXPERTS]


def _segment_copies(i, seg, local_buf, sorted_hbm, sem, to_sorted, action):
    lb_ref, gb_ref, nb_ref, ns_ref = seg
    for e in range(N_EXPERTS):
        k = i * N_EXPERTS + e

        def copy(off, rows, k=k):
            lo = pl.multiple_of(lb_ref[k] + off, PIECE)
            go = pl.multiple_of(gb_ref[k] + off, PIECE)
            local = local_buf.at[pl.ds(lo, rows), :]
            remote = sorted_hbm.at[pl.ds(go, rows), :]
            c = pltpu.make_async_copy(local, remote, sem) if to_sorted else pltpu.make_async_copy(remote, local, sem)
            getattr(c, action)()

        def big(j, carry, copy=copy):
            copy(j * BIG_PIECE, BIG_PIECE)
            return carry

        def small(j, carry, copy=copy, k=k):
            copy(nb_ref[k] * BIG_PIECE + j * PIECE, PIECE)
            return carry

        lax.fori_loop(0, nb_ref[k], big, 0)
        lax.fori_loop(0, ns_ref[k], small, 0)


def _zero_fill(fill_ref, zbuf, out_ref, sem, action):
    def copy(go, rows):
        c = pltpu.make_async_copy(zbuf.at[pl.ds(0, rows), :], out_ref.at[pl.ds(go, rows), :], sem)
        getattr(c, action)()

    for e in range(N_EXPERTS):
        def tail(j, carry, e=e):
            copy(pl.multiple_of(fill_ref[e] + j * PIECE, PIECE), PIECE)
            return carry

        lax.fori_loop(0, fill_ref[N_EXPERTS + e], tail, 0)

    def rest(j, carry):
        copy(pl.multiple_of(fill_ref[2 * N_EXPERTS] + j * MOE_TM, MOE_TM), MOE_TM)
        return carry

    lax.fori_loop(0, fill_ref[2 * N_EXPERTS + 1], rest, 0)


def _dispatch_kernel(lb_ref, gb_ref, nb_ref, ns_ref, fill_ref, h2_ref, prow_ref, out_ref, cbuf, zbuf, sems):
    seg = (lb_ref, gb_ref, nb_ref, ns_ref)
    i = pl.program_id(0)
    nt = pl.num_programs(0)
    slot = i % 2
    tg = h2_ref.shape[0]

    @pl.when(i >= 2)
    def _():
        _segment_copies(i - 2, seg, cbuf.at[slot], out_ref, sems.at[slot], True, "wait")

    s = lax.broadcasted_iota(jnp.int32, (MOE_SLOTS, tg), 0).astype(F32)
    onehot = jnp.where(s == prow_ref[0:1, :], 1.0, jnp.where(s == prow_ref[1:2, :], 1.0, 0.0)).astype(BF16)
    cbuf[slot] = jnp.dot(onehot, h2_ref[...], preferred_element_type=F32).astype(BF16)
    _segment_copies(i, seg, cbuf.at[slot], out_ref, sems.at[slot], True, "start")

    @pl.when(i == nt - 1)
    def _():
        @pl.when(i >= 1)
        def _():
            _segment_copies(i - 1, seg, cbuf.at[1 - slot], out_ref, sems.at[1 - slot], True, "wait")

        _segment_copies(i, seg, cbuf.at[slot], out_ref, sems.at[slot], True, "wait")
        zbuf[...] = jnp.zeros_like(zbuf)
        _zero_fill(fill_ref, zbuf, out_ref, sems.at[2], "start")
        _zero_fill(fill_ref, zbuf, out_ref, sems.at[2], "wait")


def _dispatch(seg, fill, h2, prow, rows, tg):
    n, d = h2.shape
    grid_spec = pltpu.PrefetchScalarGridSpec(
        num_scalar_prefetch=len(seg) + 1,
        grid=(n // tg,),
        in_specs=[
            pl.BlockSpec((tg, d), lambda i, *_: (i, 0)),
            pl.BlockSpec((None, 8, tg), lambda i, *_: (i, 0, 0)),
        ],
        out_specs=pl.BlockSpec(memory_space=pl.ANY),
        scratch_shapes=[pltpu.VMEM((2, MOE_SLOTS, d), BF16), pltpu.VMEM((MOE_TM, d), BF16),
                        pltpu.SemaphoreType.DMA((3,))],
    )
    return pl.pallas_call(
        _dispatch_kernel,
        grid_spec=grid_spec,
        out_shape=jax.ShapeDtypeStruct((rows, d), BF16),
        compiler_params=_params("arbitrary"),
        name="moe_dispatch",
    )(*seg, fill, h2, prow)


def _group_kernel(te_ref, nu_ref, x_ref, w13_ref, w2_ref, o_ref):
    del te_ref
    live = pl.program_id(0) < nu_ref[0]

    @pl.when(live)
    def _():
        f = w2_ref.shape[0]
        ab = jnp.dot(x_ref[...], w13_ref[...], preferred_element_type=F32)
        a, b = ab[:, 0:f], ab[:, f:2 * f]
        o_ref[...] = jnp.dot((a * _sigmoid(a) * b).astype(BF16), w2_ref[...],
                             preferred_element_type=F32).astype(BF16)

    @pl.when(jnp.logical_not(live))
    def _():
        o_ref[...] = jnp.zeros_like(o_ref)


def _group(tile_expert, n_used, xs, w13, w2, tm):
    rows, d = xs.shape
    f = w2.shape[1]
    used = lambda r, te, nu: (jnp.minimum(r, nu[0] - 1), 0)
    grid_spec = pltpu.PrefetchScalarGridSpec(
        num_scalar_prefetch=2,
        grid=(rows // tm,),
        in_specs=[
            pl.BlockSpec((tm, d), used),
            pl.BlockSpec((None, d, 2 * f), lambda r, te, nu: (te[r], 0, 0)),
            pl.BlockSpec((None, f, d), lambda r, te, nu: (te[r], 0, 0)),
        ],
        out_specs=pl.BlockSpec((tm, d), lambda r, te, nu: (r, 0)),
    )
    return pl.pallas_call(
        _group_kernel,
        grid_spec=grid_spec,
        out_shape=jax.ShapeDtypeStruct((rows, d), BF16),
        compiler_params=_params("arbitrary"),
        name="moe_group",
    )(tile_expert, n_used, xs, w13, w2)


def _combine_kernel(*refs, final):
    if final:
        lb_ref, gb_ref, nb_ref, ns_ref, x_ref, pcol_ref, mod_ref, fg_ref, y_ref, o_ref, ybuf, sems = refs
    else:
        lb_ref, gb_ref, nb_ref, ns_ref, x_ref, pcol_ref, mod_ref, y_ref, o_ref, ybuf, sems = refs
    seg = (lb_ref, gb_ref, nb_ref, ns_ref)
    i = pl.program_id(0)
    nt = pl.num_programs(0)
    slot = i % 2
    tg = x_ref.shape[0]

    @pl.when(i == 0)
    def _():
        ybuf[...] = jnp.zeros_like(ybuf)
        _segment_copies(i, seg, ybuf.at[slot], y_ref, sems.at[slot], False, "start")

    @pl.when(i + 1 < nt)
    def _():
        _segment_copies(i + 1, seg, ybuf.at[1 - slot], y_ref, sems.at[1 - slot], False, "start")

    s = lax.broadcasted_iota(jnp.int32, (tg, MOE_SLOTS), 1).astype(F32)
    pc = pcol_ref[...]
    scatter = jnp.where(s == pc[:, 0:1], pc[:, 2:3], jnp.where(s == pc[:, 1:2], pc[:, 3:4], 0.0)).astype(BF16)
    _segment_copies(i, seg, ybuf.at[slot], y_ref, sems.at[slot], False, "wait")
    y = jnp.dot(scatter, ybuf[slot], preferred_element_type=F32)
    xn = x_ref[...] + mod_ref[5:6, :] * y
    if final:
        xn = _rms(xn) * fg_ref[...]
    o_ref[...] = xn


def _combine(seg, x2, pcol, mod, mod_row, final_g, ys, tg):
    n, d = x2.shape
    final = final_g is not None
    in_specs = [
        pl.BlockSpec((tg, d), lambda i, *_: (i, 0)),
        pl.BlockSpec((tg, LANES), lambda i, *_: (i, 0)),
        pl.BlockSpec((None, 6, d), lambda i, *_: (mod_row(i), 0, 0)),
    ]
    args = [x2, pcol, mod]
    if final:
        in_specs.append(pl.BlockSpec((1, d), lambda i, *_: (0, 0)))
        args.append(final_g)
    in_specs.append(pl.BlockSpec(memory_space=pl.ANY))
    args.append(ys)
    grid_spec = pltpu.PrefetchScalarGridSpec(
        num_scalar_prefetch=len(seg),
        grid=(n // tg,),
        in_specs=in_specs,
        out_specs=pl.BlockSpec((tg, d), lambda i, *_: (i, 0)),
        scratch_shapes=[pltpu.VMEM((2, MOE_SLOTS, d), BF16), pltpu.SemaphoreType.DMA((2,))],
    )
    return pl.pallas_call(
        functools.partial(_combine_kernel, final=final),
        grid_spec=grid_spec,
        out_shape=jax.ShapeDtypeStruct((n, d), F32),
        compiler_params=_params("arbitrary"),
        name="moe_combine",
    )(*seg, *args)


def _routed_moe(x, mod, batch_row, w13, w2, h2, plan, final_g):
    b, l, d = x.shape
    n = b * l
    tg, tm = MOE_TG, MOE_TM
    nt = n // tg
    x2, h22 = x.reshape(n, d), h2.reshape(n, d)
    mod_row = lambda i: batch_row((i * tg) // l)
    prow, pcol, cnt = plan
    pcol = pcol.reshape(n, LANES)

    cnt = cnt[:, :, 0].astype(jnp.int32)
    cpad = (cnt + PIECE - 1) // PIECE * PIECE
    lbase = jnp.cumsum(cpad, axis=1) - cpad
    tot = (jnp.sum(cpad, axis=0) + tm - 1) // tm * tm
    ends = jnp.cumsum(tot)
    gbase = (ends - tot)[None, :] + jnp.cumsum(cpad, axis=0) - cpad
    rows = (2 * n + nt * N_EXPERTS * (PIECE - 1) + N_EXPERTS * (tm - 1) + tm - 1) // tm * tm
    n_used = (ends[-1] // tm).astype(jnp.int32)
    tile_start = jnp.arange(rows // tm, dtype=jnp.int32) * tm
    tile_expert = jnp.sum(jnp.minimum(tile_start, ends[-1] - 1)[:, None] >= ends[None, :], axis=1).astype(jnp.int32)
    seg = tuple(a.reshape(-1).astype(jnp.int32)
                for a in (lbase, gbase, cpad // BIG_PIECE, cpad % BIG_PIECE // PIECE))
    data_end = ends - tot + jnp.sum(cpad, axis=0)
    fill = jnp.concatenate([data_end, (ends - data_end) // PIECE, ends[-1:], (rows - ends[-1:]) // tm]).astype(jnp.int32)

    xs = _dispatch(seg, fill, h22, prow, rows, tg)
    ys = _group(tile_expert, n_used.reshape(1), xs, w13, w2, tm)
    out = _combine(seg, x2, pcol, mod, mod_row, final_g, ys, tg)
    return out.reshape(b, l, d)


def _rope_tables(l):
    rows = l // GRID_W
    row = jnp.repeat(jnp.arange(rows, dtype=F32), GRID_W)
    colp = jnp.tile(jnp.arange(GRID_W, dtype=F32), rows)
    inv = ROPE_THETA ** (-jnp.arange(ROPE_PAIRS, dtype=F32) * 2.0 / AXIS_DIM)
    ang_r, ang_c = row[:, None] * inv, colp[:, None] * inv
    zero = jnp.zeros_like(ang_r)
    cos64 = jnp.concatenate([jnp.cos(ang_r)] * 2 + [jnp.cos(ang_c)] * 2, axis=1)
    up64 = jnp.concatenate([-jnp.sin(ang_r), zero, -jnp.sin(ang_c), zero], axis=1)
    dn64 = jnp.concatenate([zero, jnp.sin(ang_r), zero, jnp.sin(ang_c)], axis=1)
    return tuple(jnp.tile(a, (1, LANES // HEAD_DIM)) for a in (cos64, up64, dn64))


def _lambda_init(layer):
    return 0.8 - 0.6 * math.exp(-0.3 * layer)


def _permute_in(w):
    off_q = POOL_WIDTH
    off_g = off_q + 3 * QKV_WIDTH
    return jnp.concatenate([w[:, off_g:], w[:, off_q:off_g], w[:, :off_q]], axis=1).astype(BF16)


def kernel(x, c, ctx, c_ctx, w_mod, b_mod, norm1_g, norm2_g, w_in, pool_w, pool_scale, lam_q1, lam_k1,
           lam_q2, lam_k2, subln_g, w_pool_proj, w_attn_proj, w_out, ffn_w1, ffn_w3, ffn_w2, router_w,
           moe_w1, moe_w3, moe_w2, final_g):
    b, l, d = x.shape
    n_ctx = ctx.shape[1]
    depth = w_mod.shape[0]
    assert d == D_MODEL and b + 1 <= MOD_ROWS and l % 512 == 0 and n_ctx % 256 == 0

    s_in = jnp.concatenate([c, c_ctx[None, :], jnp.zeros((MOD_ROWS - b - 1, d), F32)], axis=0)
    mod_all = _modulation(s_in, w_mod, b_mod)
    tables = _rope_tables(l)
    big_tile = 1024 if l % 1024 == 0 else 512
    lat_row = lambda bi: bi
    ctx_row = lambda bi: b
    gq, gk, gp = COL_Q // V_DIM, COL_K // V_DIM, COL_P // POOL_WIDTH
    full_kinds = ("g",) * 4 + ("q",) * 2 + ("k",) * 2 + ("v",) * 2 + ("p",)

    xc = ctx
    for layer in range(depth):
        last = layer == depth - 1
        lam0 = _lambda_init(layer)
        mod = mod_all[layer].reshape(MOD_ROWS, 6, d)
        g1 = norm1_g[layer][None, :]
        g2 = norm2_g[layer][None, :]
        w_in_p = _permute_in(w_in[layer])
        lam_p = jnp.stack([lam_q1[layer], lam_k1[layer], lam_q2[layer], lam_k2[layer]])
        g_col = subln_g[layer][:, None]
        pw = pool_w[layer].astype(BF16)
        ps = pool_scale[layer][None, :]
        wp = w_pool_proj[layer].astype(BF16)
        wa = w_attn_proj[layer].astype(BF16)
        wo = w_out[layer].astype(BF16)
        fg = final_g[None, :] if last else None
        moe_layer = layer % 2 == 1
        assert last or not moe_layer, "context tokens are only carried through dense layers"
        if moe_layer:
            rw = jnp.pad(router_w[layer // 2].T, ((0, ROUTER_ROWS - N_EXPERTS), (0, 0)))
            rw_hi = rw.astype(BF16)
            routers = (rw_hi, (rw - rw_hi.astype(F32)).astype(BF16))
            we13 = jnp.concatenate([moe_w1[layer // 2], moe_w3[layer // 2]], axis=2).astype(BF16)
            we2 = moe_w2[layer // 2].astype(BF16)
        else:
            routers = None
            wf1 = ffn_w1[layer // 2].astype(BF16)
            wf3 = ffn_w3[layer // 2].astype(BF16)
            wf2 = ffn_w2[layer // 2].astype(BF16)

        z, vt = _inproj(x, mod, lat_row, g1, w_in_p, full_kinds, tables, big_tile)
        if last:
            zc, vtc = _inproj(xc, mod, ctx_row, g1, w_in_p[:, COL_K:COL_V + QKV_WIDTH], ("k",) * 2 + ("v",) * 2,
                              None, n_ctx)
            ckb = 0
        else:
            zc, vtc = _inproj(xc, mod, ctx_row, g1, w_in_p, full_kinds, None, n_ctx)
            ckb = gk
        attn_y = _attention(lam_p, g_col, z, gq, zc, ckb, vtc, z, gk, vt, lam0, 512, 1)
        pool_y = _pool(z, gp, pw, ps)
        x, h2, *plan = _merge(x, mod, lat_row, g2, z, pool_y, attn_y, wp, wa, wo, routers,
                              MOE_TG if moe_layer else big_tile)

        if not last:
            attn_yc = _attention(lam_p, g_col, zc, COL_Q // QKV_WIDTH, zc, COL_K // QKV_WIDTH, vtc, None, 0, None,
                                 lam0, 512, N_HEADS)
            pool_yc = _pool(zc, gp, pw, ps)
            xc, h2c = _merge(xc, mod, ctx_row, g2, zc, pool_yc, attn_yc, wp, wa, wo, None, n_ctx)

        if moe_layer:
            x = _routed_moe(x, mod, lat_row, we13, we2, h2, plan, fg)
        else:
            x = _ffn(x, h2, mod, lat_row, wf1, wf3, wf2, fg, 512)
        if not last:
            xc = _ffn(xc, h2c, mod, ctx_row, wf1, wf3, wf2, fg, n_ctx)
    return x
```

```python
import functools
import math

import jax
import jax.numpy as jnp
from jax import lax
from jax.experimental import pallas as pl
from jax.experimental.pallas import tpu as pltpu

F32 = jnp.float32
BF16 = jnp.bfloat16

D_MODEL = 1024
EPS = 1e-6
GRID_W = 64
N_HEADS = 8
HEAD_DIM = 64
V_DIM = 2 * HEAD_DIM
ROPE_THETA = 10000.0
AXIS_DIM = HEAD_DIM // 2
ROPE_PAIRS = AXIS_DIM // 2
POOL_WINDOWS = (2, 4, 8, 16)
POOL_WIDTH = 512
POOL_GROUP_DIM = POOL_WIDTH // len(POOL_WINDOWS)
POOL_EDGE = max(POOL_WINDOWS) // 2
N_EXPERTS = 8
Q_SCALE = HEAD_DIM ** -0.5
LOG2E = math.log2(math.e)
MAX_UNSHIFTED_LOG2 = 80.0
SQ_NORM_SLACK = 1.05

QKV_WIDTH = N_HEADS * V_DIM
COL_G = 0
COL_Q = COL_G + 2 * D_MODEL
COL_K = COL_Q + QKV_WIDTH
COL_V = COL_K + QKV_WIDTH
COL_P = COL_V
IN_CHUNK = 512

TOKEN_TILE = 512
BIG_TILE = 1024
MOD_TILE = 1024

LANES = 128
MOD_ROWS = 40

VMEM_LIMIT = 56 * 1024 * 1024


def _resident(shape):
    nd = len(shape)
    return pl.BlockSpec(shape, lambda *_: (0,) * nd, pipeline_mode=pl.Buffered(1))


def _params(*sem):
    return pltpu.CompilerParams(dimension_semantics=sem, vmem_limit_bytes=VMEM_LIMIT)


def _sigmoid(v):
    return 1.0 / (1.0 + jnp.exp(-v))


def _rms(v):
    return v * lax.rsqrt(jnp.mean(v * v, axis=-1, keepdims=True) + EPS)


def _mod_kernel(s_ref, w_ref, b_ref, o_ref):
    s = s_ref[...]
    s = s * _sigmoid(s)
    w = w_ref[...]
    s_hi = s.astype(BF16)
    s_lo = (s - s_hi.astype(F32)).astype(BF16)
    w_hi = w.astype(BF16)
    w_lo = (w - w_hi.astype(F32)).astype(BF16)
    acc = jnp.dot(s_hi, w_hi, preferred_element_type=F32)
    acc += jnp.dot(s_hi, w_lo, preferred_element_type=F32)
    acc += jnp.dot(s_lo, w_hi, preferred_element_type=F32)
    o_ref[...] = acc + b_ref[...]


def _modulation(s_in, w_mod, b_mod):
    depth, d, n = w_mod.shape
    tn = MOD_TILE
    return pl.pallas_call(
        _mod_kernel,
        grid=(depth, n // tn),
        in_specs=[
            pl.BlockSpec((MOD_ROWS, d), lambda l, j: (0, 0)),
            pl.BlockSpec((None, d, tn), lambda l, j: (l, 0, j)),
            pl.BlockSpec((None, 1, tn), lambda l, j: (l, 0, j)),
        ],
        out_specs=pl.BlockSpec((None, MOD_ROWS, tn), lambda l, j: (l, 0, j)),
        out_shape=jax.ShapeDtypeStruct((depth, MOD_ROWS, n), F32),
        compiler_params=_params("parallel", "parallel"),
        name="modulation",
    )(s_in, w_mod, b_mod.reshape(depth, 1, n))


def _rope(z, c, s_up, s_dn):
    up = pltpu.roll(z, LANES - ROPE_PAIRS, 1)
    dn = pltpu.roll(z, ROPE_PAIRS, 1)
    return z * c + up * s_up + dn * s_dn


def _inproj_kernel(*refs, kinds, rope):
    if rope:
        x_ref, mod_ref, g_ref, w_ref, c_ref, su_ref, sd_ref, z_ref, vt_ref = refs
    else:
        x_ref, mod_ref, g_ref, w_ref, z_ref, vt_ref = refs
    y = _rms(x_ref[...]) * g_ref[...]
    h = (y * (1.0 + mod_ref[1:2, :]) + mod_ref[0:1, :]).astype(BF16)
    z_col, head = 0, 0
    for ci, kind in enumerate(kinds):
        lo = ci * IN_CHUNK
        z = jnp.dot(h, w_ref[:, lo:lo + IN_CHUNK], preferred_element_type=F32)
        if kind == "v":
            for j in range(0, IN_CHUNK, V_DIM):
                vt_ref[head * V_DIM:(head + 1) * V_DIM, :] = z[:, j:j + V_DIM].T.astype(BF16)
                head += 1
            continue
        if rope and kind in ("q", "k"):
            c, su, sd = c_ref[...], su_ref[...], sd_ref[...]
            z = jnp.concatenate(
                [_rope(z[:, j:j + LANES], c, su, sd) for j in range(0, IN_CHUNK, LANES)], axis=1)
        if kind == "q":
            z = z * (Q_SCALE * LOG2E)
        z_ref[:, z_col:z_col + IN_CHUNK] = z.astype(BF16)
        z_col += IN_CHUNK


def _inproj(x, mod, mod_row, g, w, kinds, tables, t):
    b, l, d = x.shape
    n_v = sum(k == "v" for k in kinds)
    wz = IN_CHUNK * (len(kinds) - n_v)
    rows_vt = n_v * IN_CHUNK
    rope = tables is not None
    in_specs = [
        pl.BlockSpec((None, t, d), lambda bi, i: (bi, i, 0)),
        pl.BlockSpec((None, 6, d), lambda bi, i: (mod_row(bi), 0, 0)),
        pl.BlockSpec((1, d), lambda bi, i: (0, 0)),
        _resident((d, IN_CHUNK * len(kinds))),
    ]
    args = [x, mod, g, w]
    if rope:
        in_specs += [pl.BlockSpec((t, LANES), lambda bi, i: (i, 0))] * 3
        args += list(tables)
    return pl.pallas_call(
        functools.partial(_inproj_kernel, kinds=kinds, rope=rope),
        grid=(b, l // t),
        in_specs=in_specs,
        out_specs=[pl.BlockSpec((None, t, wz), lambda bi, i: (bi, i, 0)),
                   pl.BlockSpec((None, rows_vt, t), lambda bi, i: (bi, 0, i))],
        out_shape=[jax.ShapeDtypeStruct((b, l, wz), BF16), jax.ShapeDtypeStruct((b, rows_vt, l), BF16)],
        compiler_params=_params("parallel", "parallel"),
        name="inproj",
    )(*args)


def _pool_kernel(u_ref, pw_ref, ps_ref, o_ref):
    l = u_ref.shape[0]
    e = POOL_EDGE
    t_head = lax.broadcasted_iota(jnp.int32, (e, POOL_GROUP_DIM), 0)
    t_tail = t_head + (l - e)

    def shifted(a, k):
        r = pltpu.roll(a, k % l, 0)
        if k > 0:
            return jnp.concatenate([jnp.where(t_head >= k, r[0:e], 0.0), r[e:]], axis=0)
        return jnp.concatenate([r[0:l - e], jnp.where(t_tail < l + k, r[l - e:], 0.0)], axis=0)

    def count(t, w):
        return (jnp.minimum(t + w // 2, l) - jnp.maximum(t - w // 2, 0)).astype(F32)

    for gi, w in enumerate(POOL_WINDOWS):
        lo = gi * POOL_GROUP_DIM
        u = u_ref[:, lo:lo + POOL_GROUP_DIM].astype(F32)
        back, fwd, span = u, u, 1
        while span < w // 2:
            back = back + shifted(back, span)
            fwd = fwd + shifted(fwd, -span)
            span *= 2
        win = shifted(back, 1) + fwd
        mean = jnp.concatenate([win[0:e] / count(t_head, w), win[e:l - e] * (1.0 / w),
                                win[l - e:] / count(t_tail, w)], axis=0)
        m = (mean - u).astype(BF16)
        y = jnp.dot(m, pw_ref[gi], preferred_element_type=F32)
        o_ref[:, lo:lo + POOL_GROUP_DIM] = (y * ps_ref[:, lo:lo + POOL_GROUP_DIM]).astype(BF16)


def _pool(z, col_block, pool_w, pool_scale):
    b, l, _ = z.shape
    return pl.pallas_call(
        _pool_kernel,
        grid=(b,),
        in_specs=[
            pl.BlockSpec((None, l, POOL_WIDTH), lambda bi: (bi, 0, col_block)),
            _resident(pool_w.shape),
            pl.BlockSpec((1, POOL_WIDTH), lambda bi: (0, 0)),
        ],
        out_specs=pl.BlockSpec((None, l, POOL_WIDTH), lambda bi: (bi, 0, 0)),
        out_shape=jax.ShapeDtypeStruct((b, l, POOL_WIDTH), BF16),
        compiler_params=_params("parallel"),
        name="pool",
    )(z, pool_w, pool_scale)


def _query_chunks(nq, tq):
    return [(c0, min(tq, nq - c0)) for c0 in range(0, nq, tq)]


def _attn_kernel(*refs, n_lat, tq, lam0, heads):
    if n_lat:
        lam_ref, g_ref, q_ref, kc_ref, vtc_ref, kl_ref, vtl_ref, o_ref, p_sc = refs
    else:
        lam_ref, g_ref, q_ref, kc_ref, vtc_ref, o_ref, p_sc = refs
    lp = lam_ref[...]
    lam = (jnp.exp(jnp.sum(lp[0:1] * lp[1:2], axis=1, keepdims=True))
           - jnp.exp(jnp.sum(lp[2:3] * lp[3:4], axis=1, keepdims=True)) + lam0)
    nq = q_ref.shape[0]
    half = lax.broadcasted_iota(jnp.int32, (8, V_DIM), 1) // HEAD_DIM
    sel = (half == lax.broadcasted_iota(jnp.int32, (8, V_DIM), 0)).astype(BF16)
    nt = (((1,), (1,)), ((), ()))

    def sq_norm_max(a):
        return jnp.max(lax.dot_general(sel, a * a, nt, preferred_element_type=F32), axis=1, keepdims=True)

    def head_groups(hh):
        cs = slice(hh * V_DIM, (hh + 1) * V_DIM)
        groups = [(kc_ref[:, cs], vtc_ref[cs, :])]
        if n_lat:
            groups.append((kl_ref[:, cs], vtl_ref[cs, :]))
        return cs, groups

    def scores(shift, groups, cs, c0, n, slot):
        q = q_ref[c0:c0 + n, cs]
        lane = lax.broadcasted_iota(jnp.int32, q.shape, 1)
        sums = []
        for mi, first in enumerate((True, False)):
            qm = jnp.where((lane < HEAD_DIM) if first else (lane >= HEAD_DIM), q, jnp.zeros_like(q))
            s_t = [lax.dot_general(k, qm, nt, preferred_element_type=F32) for k, _ in groups]
            if shift:
                m = jnp.max(s_t[0], axis=0, keepdims=True)
                for s_g in s_t[1:]:
                    m = jnp.maximum(m, jnp.max(s_g, axis=0, keepdims=True))
                s_t = [s_g - m for s_g in s_t]
            l_m, row = None, 0
            for s_g in s_t:
                p_g = jnp.exp2(s_g)
                l_g = jnp.sum(p_g, axis=0, keepdims=True)
                l_m = l_g if l_m is None else l_m + l_g
                p_sc[slot, mi, row:row + p_g.shape[0], 0:n] = p_g.astype(BF16)
                row += p_g.shape[0]
            sums.append(l_m)
        return sums

    def values(groups, cs, c0, n, slot, sums):
        r = (lam * sums[0] * (1.0 / sums[1])).astype(BF16)
        o_t, row = None, 0
        for k, vt in groups:
            rows = slice(row, row + k.shape[0])
            part = jnp.dot(vt, p_sc[slot, 0, rows, 0:n] - r * p_sc[slot, 1, rows, 0:n], preferred_element_type=F32)
            o_t = part if o_t is None else o_t + part
            row += k.shape[0]
        o_t = o_t * (1.0 / sums[0])
        o_t = o_t * lax.rsqrt(jnp.mean(o_t * o_t, axis=0, keepdims=True) + EPS)
        o_t = o_t * (g_ref[...] * (1.0 - lam0))
        o_ref[c0:c0 + n, cs] = o_t.T.astype(BF16)

    def attend(shift):
        work = [(hh, c0, n) for hh in range(heads) for c0, n in _query_chunks(nq, tq)]
        pending = None
        for i, (hh, c0, n) in enumerate(work):
            cs, groups = head_groups(hh)
            sums = scores(shift, groups, cs, c0, n, i % 2)
            if pending is not None:
                values(*pending)
            pending = (groups, cs, c0, n, i % 2, sums)
        values(*pending)

    attend(False)

    bound = None
    for hh in range(heads):
        cs, groups = head_groups(hh)
        k_sq = sq_norm_max(groups[0][0])
        for k, _ in groups[1:]:
            k_sq = jnp.maximum(k_sq, sq_norm_max(k))
        b_h = sq_norm_max(q_ref[:, cs]) * k_sq
        bound = b_h if bound is None else jnp.maximum(bound, b_h)
    small = jnp.max(bound) * SQ_NORM_SLACK < MAX_UNSHIFTED_LOG2 ** 2

    @pl.when(jnp.logical_not(small))
    def _():
        attend(True)


def _attention(lam_p, g_col, zq, q_blk, zc, kc_blk, vtc, zl, kl_blk, vtl, lam0, tq, heads):
    b, nq, _ = zq.shape
    n_ctx = zc.shape[1]
    n_lat = 0 if zl is None else zl.shape[1]
    w = heads * V_DIM

    def col(blk, n):
        return pl.BlockSpec((None, n, w), lambda bi, h: (bi, 0, blk + h))

    def vt_rows(n):
        return pl.BlockSpec((None, w, n), lambda bi, h: (bi, h, 0))

    in_specs = [
        pl.BlockSpec(lam_p.shape, lambda bi, h: (0, 0)),
        pl.BlockSpec(g_col.shape, lambda bi, h: (0, 0)),
        col(q_blk, nq), col(kc_blk, n_ctx), vt_rows(n_ctx),
    ]
    args = [lam_p, g_col, zq, zc, vtc]
    if n_lat:
        in_specs += [col(kl_blk, n_lat), vt_rows(n_lat)]
        args += [zl, vtl]
    tq = min(tq, nq)
    return pl.pallas_call(
        functools.partial(_attn_kernel, n_lat=n_lat, tq=tq, lam0=lam0, heads=heads),
        grid=(b, N_HEADS // heads),
        in_specs=in_specs,
        out_specs=pl.BlockSpec((None, nq, w), lambda bi, h: (bi, 0, h)),
        out_shape=jax.ShapeDtypeStruct((b, nq, QKV_WIDTH), BF16),
        scratch_shapes=[pltpu.VMEM((2, 2, n_ctx + n_lat, tq), BF16)],
        compiler_params=_params("parallel", "parallel"),
        name="attention",
    )(*args)


def _merge_kernel(*refs, plan):
    if plan:
        (x_ref, mod_ref, g2_ref, zg_ref, py_ref, ay_ref, wp_ref, wa_ref, wo_ref, rh_ref, rl_ref,
         xo_ref, h2_ref, prow_ref, pcol_ref, cnt_ref) = refs
    else:
        x_ref, mod_ref, g2_ref, zg_ref, py_ref, ay_ref, wp_ref, wa_ref, wo_ref, xo_ref, h2_ref = refs
    d = x_ref.shape[1]
    g_pool = _sigmoid(zg_ref[:, 0:d].astype(F32))
    g_attn = _sigmoid(zg_ref[:, d:2 * d].astype(F32))
    y = (g_pool * jnp.dot(py_ref[...], wp_ref[...], preferred_element_type=F32)
         + g_attn * jnp.dot(ay_ref[...], wa_ref[...], preferred_element_type=F32))
    o = jnp.dot(y.astype(BF16), wo_ref[...], preferred_element_type=F32)
    xn = x_ref[...] + mod_ref[2:3, :] * o
    xo_ref[...] = xn
    h2 = _rms(xn) * g2_ref[...] * (1.0 + mod_ref[4:5, :]) + mod_ref[3:4, :]
    h2_hi = h2.astype(BF16)
    h2_ref[...] = h2_hi
    if plan:
        h2_lo = (h2 - h2_hi.astype(F32)).astype(BF16)
        rows, cnt = _route_plan(h2_hi, h2_lo, rh_ref[...], rl_ref[...])
        prow_ref[...] = rows
        pcol_ref[...] = jnp.concatenate([rows, jnp.zeros((LANES - 8, rows.shape[1]), F32)], axis=0).T
        cnt_ref[...] = jnp.broadcast_to(cnt, (N_EXPERTS, LANES))


def _merge(x, mod, mod_row, g2, z, pool_y, attn_y, wp, wa, wo, router_w, t):
    b, l, d = x.shape
    plan = router_w is not None
    nt = l // t
    tile = lambda w: pl.BlockSpec((None, t, w), lambda bi, i: (bi, i, 0))
    in_specs = [
        tile(d),
        pl.BlockSpec((None, 6, d), lambda bi, i: (mod_row(bi), 0, 0)),
        pl.BlockSpec((1, d), lambda bi, i: (0, 0)),
        tile(2 * d),
        tile(POOL_WIDTH), tile(QKV_WIDTH),
        _resident(wp.shape), _resident(wa.shape), _resident(wo.shape),
    ]
    args = [x, mod, g2, z, pool_y, attn_y, wp, wa, wo]
    out_specs = [tile(d), tile(d)]
    out_shape = [jax.ShapeDtypeStruct((b, l, d), F32), jax.ShapeDtypeStruct((b, l, d), BF16)]
    if plan:
        assert t == MOE_TG
        in_specs += [_resident(router_w[0].shape)] * 2
        args += list(router_w)
        out_specs += [pl.BlockSpec((None, 8, t), lambda bi, i: (bi * nt + i, 0, 0)),
                      tile(LANES),
                      pl.BlockSpec((None, N_EXPERTS, LANES), lambda bi, i: (bi * nt + i, 0, 0))]
        out_shape += [jax.ShapeDtypeStruct((b * nt, 8, t), F32),
                      jax.ShapeDtypeStruct((b, l, LANES), F32),
                      jax.ShapeDtypeStruct((b * nt, N_EXPERTS, LANES), F32)]
    return pl.pallas_call(
        functools.partial(_merge_kernel, plan=plan),
        grid=(b, nt),
        in_specs=in_specs,
        out_specs=out_specs,
        out_shape=out_shape,
        compiler_params=_params("parallel", "parallel"),
        name="merge",
    )(*args)


def _swiglu(h, w1, w3, w2):
    a = jnp.dot(h, w1, preferred_element_type=F32)
    b = jnp.dot(h, w3, preferred_element_type=F32)
    return jnp.dot((a * _sigmoid(a) * b).astype(BF16), w2, preferred_element_type=F32)


def _ffn_kernel(*refs, final):
    if final:
        x_ref, h2_ref, mod_ref, w1_ref, w3_ref, w2_ref, fg_ref, o_ref = refs
    else:
        x_ref, h2_ref, mod_ref, w1_ref, w3_ref, w2_ref, o_ref = refs
    y = _swiglu(h2_ref[...], w1_ref[...], w3_ref[...], w2_ref[...])
    xn = x_ref[...] + mod_ref[5:6, :] * y
    if final:
        xn = _rms(xn) * fg_ref[...]
    o_ref[...] = xn


def _ffn(x, h2, mod, mod_row, w1, w3, w2, final_g, t):
    b, l, d = x.shape
    final = final_g is not None
    tile = pl.BlockSpec((None, t, d), lambda bi, i: (bi, i, 0))
    in_specs = [tile, tile, pl.BlockSpec((None, 6, d), lambda bi, i: (mod_row(bi), 0, 0)),
                _resident(w1.shape), _resident(w3.shape), _resident(w2.shape)]
    args = [x, h2, mod, w1, w3, w2]
    if final:
        in_specs.append(pl.BlockSpec((1, d), lambda bi, i: (0, 0)))
        args.append(final_g)
    return pl.pallas_call(
        functools.partial(_ffn_kernel, final=final),
        grid=(b, l // t),
        in_specs=in_specs,
        out_specs=tile,
        out_shape=jax.ShapeDtypeStruct((b, l, d), F32),
        compiler_params=_params("parallel", "parallel"),
        name="ffn",
    )(*args)


MOE_TG = TOKEN_TILE
MOE_TM = 1024
PIECE = 16
BIG_PIECE = 64
MOE_SLOTS = 2 * MOE_TG + N_EXPERTS * PIECE
ROUTER_ROWS = 16


def _route_plan(h2_hi, h2_lo, rh, rl):
    tg = h2_hi.shape[0]
    nt = (((1,), (1,)), ((), ()))
    both = lax.dot_general(jnp.concatenate([rh, rl], axis=0), h2_hi, nt, preferred_element_type=F32)
    logits = (both[0:ROUTER_ROWS] + both[ROUTER_ROWS:2 * ROUTER_ROWS]
              + lax.dot_general(rh, h2_lo, nt, preferred_element_type=F32))
    sub = lax.broadcasted_iota(jnp.int32, logits.shape, 0).astype(F32)
    neg = jnp.float32(-jnp.inf)
    logits = jnp.where(sub < N_EXPERTS, logits, neg)
    v1 = jnp.max(logits, axis=0, keepdims=True)
    i1 = jnp.min(jnp.where(logits == v1, sub, float(ROUTER_ROWS)), axis=0, keepdims=True)
    rest = jnp.where(sub == i1, neg, logits)
    v2 = jnp.max(rest, axis=0, keepdims=True)
    i2 = jnp.min(jnp.where(rest == v2, sub, float(ROUTER_ROWS)), axis=0, keepdims=True)
    e2 = jnp.exp(v2 - v1)
    w1 = 1.0 / (1.0 + e2)
    w2 = e2 * w1
    two = w2 != 0.0
    sel = ((sub == i1) | ((sub == i2) & two)).astype(F32)
    before = (lax.broadcasted_iota(jnp.int32, (tg, tg), 0)
              < lax.broadcasted_iota(jnp.int32, (tg, tg), 1)).astype(BF16)
    rank = jnp.dot(sel.astype(BF16), before, preferred_element_type=F32)
    cnt = jnp.sum(sel, axis=1, keepdims=True)
    cpad = jnp.ceil(cnt * (1.0 / PIECE)) * PIECE
    subc = sub[:, 0:1]
    lbase = jnp.zeros_like(cpad)
    for e in range(N_EXPERTS - 1):
        c_e = jnp.sum(jnp.where(subc == e, cpad, 0.0), axis=0, keepdims=True)
        lbase = lbase + jnp.where(subc > e, c_e, 0.0)
    slot = lbase + rank
    e_a = jnp.where(two, jnp.minimum(i1, i2), i1)
    e_b = jnp.where(two, jnp.maximum(i1, i2), i1)
    ls_a = jnp.sum(jnp.where(sub == e_a, slot, 0.0), axis=0, keepdims=True)
    ls_b = jnp.sum(jnp.where(sub == e_b, slot, 0.0), axis=0, keepdims=True)
    w_a = jnp.where(e_a == i1, w1, w2)
    w_b = jnp.where(two, jnp.where(e_b == i1, w1, w2), 0.0)
    rows = jnp.concatenate([ls_a, ls_b, w_a, w_b, jnp.zeros((4, tg), F32)], axis=0)
    return rows, cnt[0:N_EXPERTS]


def _segment_copies(i, seg, local_buf, sorted_hbm, sem, to_sorted, action):
    lb_ref, gb_ref, nb_ref, ns_ref = seg
    for e in range(N_EXPERTS):
        k = i * N_EXPERTS + e

        def copy(off, rows, k=k):
            lo = pl.multiple_of(lb_ref[k] + off, PIECE)
            go = pl.multiple_of(gb_ref[k] + off, PIECE)
            local = local_buf.at[pl.ds(lo, rows), :]
            remote = sorted_hbm.at[pl.ds(go, rows), :]
            c = pltpu.make_async_copy(local, remote, sem) if to_sorted else pltpu.make_async_copy(remote, local, sem)
            getattr(c, action)()

        def big(j, carry, copy=copy):
            copy(j * BIG_PIECE, BIG_PIECE)
            return carry

        def small(j, carry, copy=copy, k=k):
            copy(nb_ref[k] * BIG_PIECE + j * PIECE, PIECE)
            return carry

        lax.fori_loop(0, nb_ref[k], big, 0)
        lax.fori_loop(0, ns_ref[k], small, 0)


def _zero_fill(fill_ref, zbuf, out_ref, sem, action):
    def copy(go, rows):
        c = pltpu.make_async_copy(zbuf.at[pl.ds(0, rows), :], out_ref.at[pl.ds(go, rows), :], sem)
        getattr(c, action)()

    for e in range(N_EXPERTS):
        def tail(j, carry, e=e):
            copy(pl.multiple_of(fill_ref[e] + j * PIECE, PIECE), PIECE)
            return carry

        lax.fori_loop(0, fill_ref[N_EXPERTS + e], tail, 0)

    def rest(j, carry):
        copy(pl.multiple_of(fill_ref[2 * N_EXPERTS] + j * MOE_TM, MOE_TM), MOE_TM)
        return carry

    lax.fori_loop(0, fill_ref[2 * N_EXPERTS + 1], rest, 0)


def _dispatch_kernel(lb_ref, gb_ref, nb_ref, ns_ref, fill_ref, h2_ref, prow_ref, out_ref, cbuf, zbuf, sems):
    seg = (lb_ref, gb_ref, nb_ref, ns_ref)
    i = pl.program_id(0)
    nt = pl.num_programs(0)
    slot = i % 2
    tg = h2_ref.shape[0]

    @pl.when(i >= 2)
    def _():
        _segment_copies(i - 2, seg, cbuf.at[slot], out_ref, sems.at[slot], True, "wait")

    s = lax.broadcasted_iota(jnp.int32, (MOE_SLOTS, tg), 0).astype(F32)
    onehot = jnp.where(s == prow_ref[0:1, :], 1.0, jnp.where(s == prow_ref[1:2, :], 1.0, 0.0)).astype(BF16)
    cbuf[slot] = jnp.dot(onehot, h2_ref[...], preferred_element_type=F32).astype(BF16)
    _segment_copies(i, seg, cbuf.at[slot], out_ref, sems.at[slot], True, "start")

    @pl.when(i == nt - 1)
    def _():
        @pl.when(i >= 1)
        def _():
            _segment_copies(i - 1, seg, cbuf.at[1 - slot], out_ref, sems.at[1 - slot], True, "wait")

        _segment_copies(i, seg, cbuf.at[slot], out_ref, sems.at[slot], True, "wait")
        zbuf[...] = jnp.zeros_like(zbuf)
        _zero_fill(fill_ref, zbuf, out_ref, sems.at[2], "start")
        _zero_fill(fill_ref, zbuf, out_ref, sems.at[2], "wait")


def _dispatch(seg, fill, h2, prow, rows, tg):
    n, d = h2.shape
    grid_spec = pltpu.PrefetchScalarGridSpec(
        num_scalar_prefetch=len(seg) + 1,
        grid=(n // tg,),
        in_specs=[
            pl.BlockSpec((tg, d), lambda i, *_: (i, 0)),
            pl.BlockSpec((None, 8, tg), lambda i, *_: (i, 0, 0)),
        ],
        out_specs=pl.BlockSpec(memory_space=pl.ANY),
        scratch_shapes=[pltpu.VMEM((2, MOE_SLOTS, d), BF16), pltpu.VMEM((MOE_TM, d), BF16),
                        pltpu.SemaphoreType.DMA((3,))],
    )
    return pl.pallas_call(
        _dispatch_kernel,
        grid_spec=grid_spec,
        out_shape=jax.ShapeDtypeStruct((rows, d), BF16),
        compiler_params=_params("arbitrary"),
        name="moe_dispatch",
    )(*seg, fill, h2, prow)


def _group_kernel(te_ref, nu_ref, x_ref, w13_ref, w2_ref, o_ref):
    del te_ref
    live = pl.program_id(0) < nu_ref[0]

    @pl.when(live)
    def _():
        f = w2_ref.shape[0]
        ab = jnp.dot(x_ref[...], w13_ref[...], preferred_element_type=F32)
        a, b = ab[:, 0:f], ab[:, f:2 * f]
        o_ref[...] = jnp.dot((a * _sigmoid(a) * b).astype(BF16), w2_ref[...],
                             preferred_element_type=F32).astype(BF16)

    @pl.when(jnp.logical_not(live))
    def _():
        o_ref[...] = jnp.zeros_like(o_ref)


def _group(tile_expert, n_used, xs, w13, w2, tm):
    rows, d = xs.shape
    f = w2.shape[1]
    used = lambda r, te, nu: (jnp.minimum(r, nu[0] - 1), 0)
    grid_spec = pltpu.PrefetchScalarGridSpec(
        num_scalar_prefetch=2,
        grid=(rows // tm,),
        in_specs=[
            pl.BlockSpec((tm, d), used),
            pl.BlockSpec((None, d, 2 * f), lambda r, te, nu: (te[r], 0, 0)),
            pl.BlockSpec((None, f, d), lambda r, te, nu: (te[r], 0, 0)),
        ],
        out_specs=pl.BlockSpec((tm, d), lambda r, te, nu: (r, 0)),
    )
    return pl.pallas_call(
        _group_kernel,
        grid_spec=grid_spec,
        out_shape=jax.ShapeDtypeStruct((rows, d), BF16),
        compiler_params=_params("arbitrary"),
        name="moe_group",
    )(tile_expert, n_used, xs, w13, w2)


def _combine_kernel(*refs, final):
    if final:
        lb_ref, gb_ref, nb_ref, ns_ref, x_ref, pcol_ref, mod_ref, fg_ref, y_ref, o_ref, ybuf, sems = refs
    else:
        lb_ref, gb_ref, nb_ref, ns_ref, x_ref, pcol_ref, mod_ref, y_ref, o_ref, ybuf, sems = refs
    seg = (lb_ref, gb_ref, nb_ref, ns_ref)
    i = pl.program_id(0)
    nt = pl.num_programs(0)
    slot = i % 2
    tg = x_ref.shape[0]

    @pl.when(i == 0)
    def _():
        ybuf[...] = jnp.zeros_like(ybuf)
        _segment_copies(i, seg, ybuf.at[slot], y_ref, sems.at[slot], False, "start")

    @pl.when(i + 1 < nt)
    def _():
        _segment_copies(i + 1, seg, ybuf.at[1 - slot], y_ref, sems.at[1 - slot], False, "start")

    s = lax.broadcasted_iota(jnp.int32, (tg, MOE_SLOTS), 1).astype(F32)
    pc = pcol_ref[...]
    scatter = jnp.where(s == pc[:, 0:1], pc[:, 2:3], jnp.where(s == pc[:, 1:2], pc[:, 3:4], 0.0)).astype(BF16)
    _segment_copies(i, seg, ybuf.at[slot], y_ref, sems.at[slot], False, "wait")
    y = jnp.dot(scatter, ybuf[slot], preferred_element_type=F32)
    xn = x_ref[...] + mod_ref[5:6, :] * y
    if final:
        xn = _rms(xn) * fg_ref[...]
    o_ref[...] = xn


def _combine(seg, x2, pcol, mod, mod_row, final_g, ys, tg):
    n, d = x2.shape
    final = final_g is not None
    in_specs = [
        pl.BlockSpec((tg, d), lambda i, *_: (i, 0)),
        pl.BlockSpec((tg, LANES), lambda i, *_: (i, 0)),
        pl.BlockSpec((None, 6, d), lambda i, *_: (mod_row(i), 0, 0)),
    ]
    args = [x2, pcol, mod]
    if final:
        in_specs.append(pl.BlockSpec((1, d), lambda i, *_: (0, 0)))
        args.append(final_g)
    in_specs.append(pl.BlockSpec(memory_space=pl.ANY))
    args.append(ys)
    grid_spec = pltpu.PrefetchScalarGridSpec(
        num_scalar_prefetch=len(seg),
        grid=(n // tg,),
        in_specs=in_specs,
        out_specs=pl.BlockSpec((tg, d), lambda i, *_: (i, 0)),
        scratch_shapes=[pltpu.VMEM((2, MOE_SLOTS, d), BF16), pltpu.SemaphoreType.DMA((2,))],
    )
    return pl.pallas_call(
        functools.partial(_combine_kernel, final=final),
        grid_spec=grid_spec,
        out_shape=jax.ShapeDtypeStruct((n, d), F32),
        compiler_params=_params("arbitrary"),
        name="moe_combine",
    )(*seg, *args)


def _routed_moe(x, mod, batch_row, w13, w2, h2, plan, final_g):
    b, l, d = x.shape
    n = b * l
    tg, tm = MOE_TG, MOE_TM
    nt = n // tg
    x2, h22 = x.reshape(n, d), h2.reshape(n, d)
    mod_row = lambda i: batch_row((i * tg) // l)
    prow, pcol, cnt = plan
    pcol = pcol.reshape(n, LANES)

    cnt = cnt[:, :, 0].astype(jnp.int32)
    cpad = (cnt + PIECE - 1) // PIECE * PIECE
    lbase = jnp.cumsum(cpad, axis=1) - cpad
    tot = (jnp.sum(cpad, axis=0) + tm - 1) // tm * tm
    ends = jnp.cumsum(tot)
    gbase = (ends - tot)[None, :] + jnp.cumsum(cpad, axis=0) - cpad
    rows = (2 * n + nt * N_EXPERTS * (PIECE - 1) + N_EXPERTS * (tm - 1) + tm - 1) // tm * tm
    n_used = (ends[-1] // tm).astype(jnp.int32)
    tile_start = jnp.arange(rows // tm, dtype=jnp.int32) * tm
    tile_expert = jnp.sum(jnp.minimum(tile_start, ends[-1] - 1)[:, None] >= ends[None, :], axis=1).astype(jnp.int32)
    seg = tuple(a.reshape(-1).astype(jnp.int32)
                for a in (lbase, gbase, cpad // BIG_PIECE, cpad % BIG_PIECE // PIECE))
    data_end = ends - tot + jnp.sum(cpad, axis=0)
    fill = jnp.concatenate([data_end, (ends - data_end) // PIECE, ends[-1:], (rows - ends[-1:]) // tm]).astype(jnp.int32)

    xs = _dispatch(seg, fill, h22, prow, rows, tg)
    ys = _group(tile_expert, n_used.reshape(1), xs, w13, w2, tm)
    out = _combine(seg, x2, pcol, mod, mod_row, final_g, ys, tg)
    return out.reshape(b, l, d)


def _rope_tables(l):
    rows = l // GRID_W
    row = jnp.repeat(jnp.arange(rows, dtype=F32), GRID_W)
    colp = jnp.tile(jnp.arange(GRID_W, dtype=F32), rows)
    inv = ROPE_THETA ** (-jnp.arange(ROPE_PAIRS, dtype=F32) * 2.0 / AXIS_DIM)
    ang_r, ang_c = row[:, None] * inv, colp[:, None] * inv
    zero = jnp.zeros_like(ang_r)
    cos64 = jnp.concatenate([jnp.cos(ang_r)] * 2 + [jnp.cos(ang_c)] * 2, axis=1)
    up64 = jnp.concatenate([-jnp.sin(ang_r), zero, -jnp.sin(ang_c), zero], axis=1)
    dn64 = jnp.concatenate([zero, jnp.sin(ang_r), zero, jnp.sin(ang_c)], axis=1)
    return tuple(jnp.tile(a, (1, LANES // HEAD_DIM)) for a in (cos64, up64, dn64))


def _lambda_init(layer):
    return 0.8 - 0.6 * math.exp(-0.3 * layer)


def _permute_in(w):
    off_q = POOL_WIDTH
    off_g = off_q + 3 * QKV_WIDTH
    return jnp.concatenate([w[:, off_g:], w[:, off_q:off_g], w[:, :off_q]], axis=1).astype(BF16)


def kernel(x, c, ctx, c_ctx, w_mod, b_mod, norm1_g, norm2_g, w_in, pool_w, pool_scale, lam_q1, lam_k1,
           lam_q2, lam_k2, subln_g, w_pool_proj, w_attn_proj, w_out, ffn_w1, ffn_w3, ffn_w2, router_w,
           moe_w1, moe_w3, moe_w2, final_g):
    b, l, d = x.shape
    n_ctx = ctx.shape[1]
    depth = w_mod.shape[0]
    assert d == D_MODEL and b + 1 <= MOD_ROWS and l % TOKEN_TILE == 0 and n_ctx % 256 == 0

    s_in = jnp.concatenate([c, c_ctx[None, :], jnp.zeros((MOD_ROWS - b - 1, d), F32)], axis=0)
    mod_all = _modulation(s_in, w_mod, b_mod)
    tables = _rope_tables(l)
    big_tile = BIG_TILE if l % BIG_TILE == 0 else TOKEN_TILE
    lat_row = lambda bi: bi
    ctx_row = lambda bi: b
    gq, gk, gp = COL_Q // V_DIM, COL_K // V_DIM, COL_P // POOL_WIDTH
    full_kinds = ("g",) * 4 + ("q",) * 2 + ("k",) * 2 + ("v",) * 2 + ("p",)

    xc = ctx
    for layer in range(depth):
        last = layer == depth - 1
        lam0 = _lambda_init(layer)
        mod = mod_all[layer].reshape(MOD_ROWS, 6, d)
        g1 = norm1_g[layer][None, :]
        g2 = norm2_g[layer][None, :]
        w_in_p = _permute_in(w_in[layer])
        lam_p = jnp.stack([lam_q1[layer], lam_k1[layer], lam_q2[layer], lam_k2[layer]])
        g_col = subln_g[layer][:, None]
        pw = pool_w[layer].astype(BF16)
        ps = pool_scale[layer][None, :]
        wp = w_pool_proj[layer].astype(BF16)
        wa = w_attn_proj[layer].astype(BF16)
        wo = w_out[layer].astype(BF16)
        fg = final_g[None, :] if last else None
        moe_layer = layer % 2 == 1
        assert last or not moe_layer, "context tokens are only carried through dense layers"
        if moe_layer:
            rw = jnp.pad(router_w[layer // 2].T, ((0, ROUTER_ROWS - N_EXPERTS), (0, 0)))
            rw_hi = rw.astype(BF16)
            routers = (rw_hi, (rw - rw_hi.astype(F32)).astype(BF16))
            we13 = jnp.concatenate([moe_w1[layer // 2], moe_w3[layer // 2]], axis=2).astype(BF16)
            we2 = moe_w2[layer // 2].astype(BF16)
        else:
            routers = None
            wf1 = ffn_w1[layer // 2].astype(BF16)
            wf3 = ffn_w3[layer // 2].astype(BF16)
            wf2 = ffn_w2[layer // 2].astype(BF16)

        z, vt = _inproj(x, mod, lat_row, g1, w_in_p, full_kinds, tables, big_tile)
        if last:
            zc, vtc = _inproj(xc, mod, ctx_row, g1, w_in_p[:, COL_K:COL_V + QKV_WIDTH], ("k",) * 2 + ("v",) * 2,
                              None, n_ctx)
            ckb = 0
        else:
            zc, vtc = _inproj(xc, mod, ctx_row, g1, w_in_p, full_kinds, None, n_ctx)
            ckb = gk
        attn_y = _attention(lam_p, g_col, z, gq, zc, ckb, vtc, z, gk, vt, lam0, TOKEN_TILE, 1)
        pool_y = _pool(z, gp, pw, ps)
        x, h2, *plan = _merge(x, mod, lat_row, g2, z, pool_y, attn_y, wp, wa, wo, routers,
                              MOE_TG if moe_layer else big_tile)

        if not last:
            attn_yc = _attention(lam_p, g_col, zc, COL_Q // QKV_WIDTH, zc, COL_K // QKV_WIDTH, vtc, None, 0, None,
                                 lam0, TOKEN_TILE, N_HEADS)
            pool_yc = _pool(zc, gp, pw, ps)
            xc, h2c = _merge(xc, mod, ctx_row, g2, zc, pool_yc, attn_yc, wp, wa, wo, None, n_ctx)

        if moe_layer:
            x = _routed_moe(x, mod, lat_row, we13, we2, h2, plan, fg)
        else:
            x = _ffn(x, h2, mod, lat_row, wf1, wf3, wf2, fg, TOKEN_TILE)
        if not last:
            xc = _ffn(xc, h2c, mod, ctx_row, wf1, wf3, wf2, fg, n_ctx)
    return x
```

```python
import functools
import math

import jax
import jax.numpy as jnp
from jax import lax
from jax.experimental import pallas as pl
from jax.experimental.pallas import tpu as pltpu

F32 = jnp.float32
BF16 = jnp.bfloat16

D_MODEL = 1024
EPS = 1e-6
GRID_W = 64
N_HEADS = 8
HEAD_DIM = 64
V_DIM = 2 * HEAD_DIM
ROPE_THETA = 10000.0
AXIS_DIM = HEAD_DIM // 2
ROPE_PAIRS = AXIS_DIM // 2
POOL_WINDOWS = (2, 4, 8, 16)
POOL_WIDTH = 512
POOL_GROUP_DIM = POOL_WIDTH // len(POOL_WINDOWS)
POOL_EDGE = max(POOL_WINDOWS) // 2
N_EXPERTS = 8
Q_SCALE = HEAD_DIM ** -0.5
LOG2E = math.log2(math.e)
MAX_UNSHIFTED_LOG2 = 80.0
SQ_NORM_SLACK = 1.05

QKV_WIDTH = N_HEADS * V_DIM
COL_G = 0
COL_Q = COL_G + 2 * D_MODEL
COL_K = COL_Q + QKV_WIDTH
COL_V = COL_K + QKV_WIDTH
COL_P = COL_V
IN_CHUNK = 512

TOKEN_TILE = 512
BIG_TILE = 1024
MOD_TILE = 1024

LANES = 128
MOD_ROWS = 40

VMEM_LIMIT = 56 * 1024 * 1024


def _resident(shape):
    nd = len(shape)
    return pl.BlockSpec(shape, lambda *_: (0,) * nd, pipeline_mode=pl.Buffered(1))


def _params(*sem):
    return pltpu.CompilerParams(dimension_semantics=sem, vmem_limit_bytes=VMEM_LIMIT)


def _sigmoid(v):
    return 1.0 / (1.0 + jnp.exp(-v))


def _rms(v):
    return v * lax.rsqrt(jnp.mean(v * v, axis=-1, keepdims=True) + EPS)


def _mod_kernel(s_ref, w_ref, b_ref, o_ref):
    s = s_ref[...]
    s = s * _sigmoid(s)
    w = w_ref[...]
    s_hi = s.astype(BF16)
    s_lo = (s - s_hi.astype(F32)).astype(BF16)
    w_hi = w.astype(BF16)
    w_lo = (w - w_hi.astype(F32)).astype(BF16)
    acc = jnp.dot(s_hi, w_hi, preferred_element_type=F32)
    acc += jnp.dot(s_hi, w_lo, preferred_element_type=F32)
    acc += jnp.dot(s_lo, w_hi, preferred_element_type=F32)
    o_ref[...] = acc + b_ref[...]


def _modulation(s_in, w_mod, b_mod):
    depth, d, n = w_mod.shape
    tn = MOD_TILE
    return pl.pallas_call(
        _mod_kernel,
        grid=(depth, n // tn),
        in_specs=[
            pl.BlockSpec((MOD_ROWS, d), lambda l, j: (0, 0)),
            pl.BlockSpec((None, d, tn), lambda l, j: (l, 0, j)),
            pl.BlockSpec((None, 1, tn), lambda l, j: (l, 0, j)),
        ],
        out_specs=pl.BlockSpec((None, MOD_ROWS, tn), lambda l, j: (l, 0, j)),
        out_shape=jax.ShapeDtypeStruct((depth, MOD_ROWS, n), F32),
        compiler_params=_params("parallel", "parallel"),
        name="modulation",
    )(s_in, w_mod, b_mod.reshape(depth, 1, n))


def _rope(z, c, s_up, s_dn):
    up = pltpu.roll(z, LANES - ROPE_PAIRS, 1)
    dn = pltpu.roll(z, ROPE_PAIRS, 1)
    return z * c + up * s_up + dn * s_dn


def _inproj_kernel(*refs, kinds, rope):
    if rope:
        x_ref, mod_ref, g_ref, w_ref, c_ref, su_ref, sd_ref, z_ref, vt_ref = refs
    else:
        x_ref, mod_ref, g_ref, w_ref, z_ref, vt_ref = refs
    y = _rms(x_ref[...]) * g_ref[...]
    h = (y * (1.0 + mod_ref[1:2, :]) + mod_ref[0:1, :]).astype(BF16)
    z_col, head = 0, 0
    for ci, kind in enumerate(kinds):
        lo = ci * IN_CHUNK
        z = jnp.dot(h, w_ref[:, lo:lo + IN_CHUNK], preferred_element_type=F32)
        if kind == "v":
            for j in range(0, IN_CHUNK, V_DIM):
                vt_ref[head * V_DIM:(head + 1) * V_DIM, :] = z[:, j:j + V_DIM].T.astype(BF16)
                head += 1
            continue
        if rope and kind in ("q", "k"):
            c, su, sd = c_ref[...], su_ref[...], sd_ref[...]
            z = jnp.concatenate(
                [_rope(z[:, j:j + LANES], c, su, sd) for j in range(0, IN_CHUNK, LANES)], axis=1)
        if kind == "q":
            z = z * (Q_SCALE * LOG2E)
        z_ref[:, z_col:z_col + IN_CHUNK] = z.astype(BF16)
        z_col += IN_CHUNK


def _inproj(x, mod, mod_row, g, w, kinds, tables, t):
    b, l, d = x.shape
    n_v = sum(k == "v" for k in kinds)
    wz = IN_CHUNK * (len(kinds) - n_v)
    rows_vt = n_v * IN_CHUNK
    rope = tables is not None
    in_specs = [
        pl.BlockSpec((None, t, d), lambda bi, i: (bi, i, 0)),
        pl.BlockSpec((None, 6, d), lambda bi, i: (mod_row(bi), 0, 0)),
        pl.BlockSpec((1, d), lambda bi, i: (0, 0)),
        _resident((d, IN_CHUNK * len(kinds))),
    ]
    args = [x, mod, g, w]
    if rope:
        in_specs += [pl.BlockSpec((t, LANES), lambda bi, i: (i, 0))] * 3
        args += list(tables)
    return pl.pallas_call(
        functools.partial(_inproj_kernel, kinds=kinds, rope=rope),
        grid=(b, l // t),
        in_specs=in_specs,
        out_specs=[pl.BlockSpec((None, t, wz), lambda bi, i: (bi, i, 0)),
                   pl.BlockSpec((None, rows_vt, t), lambda bi, i: (bi, 0, i))],
        out_shape=[jax.ShapeDtypeStruct((b, l, wz), BF16), jax.ShapeDtypeStruct((b, rows_vt, l), BF16)],
        compiler_params=_params("parallel", "parallel"),
        name="inproj",
    )(*args)


def _pool_kernel(u_ref, pw_ref, ps_ref, o_ref):
    l = u_ref.shape[0]
    e = POOL_EDGE
    t_head = lax.broadcasted_iota(jnp.int32, (e, POOL_GROUP_DIM), 0)
    t_tail = t_head + (l - e)

    def shifted(a, k):
        r = pltpu.roll(a, k % l, 0)
        if k > 0:
            return jnp.concatenate([jnp.where(t_head >= k, r[0:e], 0.0), r[e:]], axis=0)
        return jnp.concatenate([r[0:l - e], jnp.where(t_tail < l + k, r[l - e:], 0.0)], axis=0)

    def count(t, w):
        return (jnp.minimum(t + w // 2, l) - jnp.maximum(t - w // 2, 0)).astype(F32)

    for gi, w in enumerate(POOL_WINDOWS):
        lo = gi * POOL_GROUP_DIM
        u = u_ref[:, lo:lo + POOL_GROUP_DIM].astype(F32)
        back, fwd, span = u, u, 1
        while span < w // 2:
            back = back + shifted(back, span)
            fwd = fwd + shifted(fwd, -span)
            span *= 2
        win = shifted(back, 1) + fwd
        mean = jnp.concatenate([win[0:e] / count(t_head, w), win[e:l - e] * (1.0 / w),
                                win[l - e:] / count(t_tail, w)], axis=0)
        m = (mean - u).astype(BF16)
        y = jnp.dot(m, pw_ref[gi], preferred_element_type=F32)
        o_ref[:, lo:lo + POOL_GROUP_DIM] = (y * ps_ref[:, lo:lo + POOL_GROUP_DIM]).astype(BF16)


def _pool(z, col_block, pool_w, pool_scale):
    b, l, _ = z.shape
    return pl.pallas_call(
        _pool_kernel,
        grid=(b,),
        in_specs=[
            pl.BlockSpec((None, l, POOL_WIDTH), lambda bi: (bi, 0, col_block)),
            _resident(pool_w.shape),
            pl.BlockSpec((1, POOL_WIDTH), lambda bi: (0, 0)),
        ],
        out_specs=pl.BlockSpec((None, l, POOL_WIDTH), lambda bi: (bi, 0, 0)),
        out_shape=jax.ShapeDtypeStruct((b, l, POOL_WIDTH), BF16),
        compiler_params=_params("parallel"),
        name="pool",
    )(z, pool_w, pool_scale)


def _query_chunks(nq, tq):
    return [(c0, min(tq, nq - c0)) for c0 in range(0, nq, tq)]


def _attn_kernel(*refs, n_lat, tq, lam0, heads):
    if n_lat:
        lam_ref, g_ref, q_ref, kc_ref, vtc_ref, kl_ref, vtl_ref, o_ref, p_sc = refs
    else:
        lam_ref, g_ref, q_ref, kc_ref, vtc_ref, o_ref, p_sc = refs
    lp = lam_ref[...]
    lam = (jnp.exp(jnp.sum(lp[0:1] * lp[1:2], axis=1, keepdims=True))
           - jnp.exp(jnp.sum(lp[2:3] * lp[3:4], axis=1, keepdims=True)) + lam0)
    nq = q_ref.shape[0]
    half = lax.broadcasted_iota(jnp.int32, (8, V_DIM), 1) // HEAD_DIM
    sel = (half == lax.broadcasted_iota(jnp.int32, (8, V_DIM), 0)).astype(BF16)
    nt = (((1,), (1,)), ((), ()))

    def sq_norm_max(a):
        return jnp.max(lax.dot_general(sel, a * a, nt, preferred_element_type=F32), axis=1, keepdims=True)

    def head_groups(hh):
        cs = slice(hh * V_DIM, (hh + 1) * V_DIM)
        groups = [(kc_ref[:, cs], vtc_ref[cs, :])]
        if n_lat:
            groups.append((kl_ref[:, cs], vtl_ref[cs, :]))
        return cs, groups

    def scores(shift, groups, cs, c0, n, slot):
        q = q_ref[c0:c0 + n, cs]
        lane = lax.broadcasted_iota(jnp.int32, q.shape, 1)
        sums = []
        for mi, first in enumerate((True, False)):
            qm = jnp.where((lane < HEAD_DIM) if first else (lane >= HEAD_DIM), q, jnp.zeros_like(q))
            s_t = [lax.dot_general(k, qm, nt, preferred_element_type=F32) for k, _ in groups]
            if shift:
                m = jnp.max(s_t[0], axis=0, keepdims=True)
                for s_g in s_t[1:]:
                    m = jnp.maximum(m, jnp.max(s_g, axis=0, keepdims=True))
                s_t = [s_g - m for s_g in s_t]
            l_m, row = None, 0
            for s_g in s_t:
                p_g = jnp.exp2(s_g)
                l_g = jnp.sum(p_g, axis=0, keepdims=True)
                l_m = l_g if l_m is None else l_m + l_g
                p_sc[slot, mi, row:row + p_g.shape[0], 0:n] = p_g.astype(BF16)
                row += p_g.shape[0]
            sums.append(l_m)
        return sums

    def values(groups, cs, c0, n, slot, sums):
        r = (lam * sums[0] * (1.0 / sums[1])).astype(BF16)
        o_t, row = None, 0
        for k, vt in groups:
            rows = slice(row, row + k.shape[0])
            part = jnp.dot(vt, p_sc[slot, 0, rows, 0:n] - r * p_sc[slot, 1, rows, 0:n], preferred_element_type=F32)
            o_t = part if o_t is None else o_t + part
            row += k.shape[0]
        o_t = o_t * (1.0 / sums[0])
        o_t = o_t * lax.rsqrt(jnp.mean(o_t * o_t, axis=0, keepdims=True) + EPS)
        o_t = o_t * (g_ref[...] * (1.0 - lam0))
        o_ref[c0:c0 + n, cs] = o_t.T.astype(BF16)

    def attend(shift):
        work = [(hh, c0, n) for hh in range(heads) for c0, n in _query_chunks(nq, tq)]
        pending = None
        for i, (hh, c0, n) in enumerate(work):
            cs, groups = head_groups(hh)
            sums = scores(shift, groups, cs, c0, n, i % 2)
            if pending is not None:
                values(*pending)
            pending = (groups, cs, c0, n, i % 2, sums)
        values(*pending)

    attend(False)

    bound = None
    for hh in range(heads):
        cs, groups = head_groups(hh)
        k_sq = sq_norm_max(groups[0][0])
        for k, _ in groups[1:]:
            k_sq = jnp.maximum(k_sq, sq_norm_max(k))
        b_h = sq_norm_max(q_ref[:, cs]) * k_sq
        bound = b_h if bound is None else jnp.maximum(bound, b_h)
    small = jnp.max(bound) * SQ_NORM_SLACK < MAX_UNSHIFTED_LOG2 ** 2

    @pl.when(jnp.logical_not(small))
    def _():
        attend(True)


def _attention(lam_p, g_col, zq, q_blk, zc, kc_blk, vtc, zl, kl_blk, vtl, lam0, tq, heads):
    b, nq, _ = zq.shape
    n_ctx = zc.shape[1]
    n_lat = 0 if zl is None else zl.shape[1]
    w = heads * V_DIM

    def col(blk, n):
        return pl.BlockSpec((None, n, w), lambda bi, h: (bi, 0, blk + h))

    def vt_rows(n):
        return pl.BlockSpec((None, w, n), lambda bi, h: (bi, h, 0))

    in_specs = [
        pl.BlockSpec(lam_p.shape, lambda bi, h: (0, 0)),
        pl.BlockSpec(g_col.shape, lambda bi, h: (0, 0)),
        col(q_blk, nq), col(kc_blk, n_ctx), vt_rows(n_ctx),
    ]
    args = [lam_p, g_col, zq, zc, vtc]
    if n_lat:
        in_specs += [col(kl_blk, n_lat), vt_rows(n_lat)]
        args += [zl, vtl]
    tq = min(tq, nq)
    return pl.pallas_call(
        functools.partial(_attn_kernel, n_lat=n_lat, tq=tq, lam0=lam0, heads=heads),
        grid=(b, N_HEADS // heads),
        in_specs=in_specs,
        out_specs=pl.BlockSpec((None, nq, w), lambda bi, h: (bi, 0, h)),
        out_shape=jax.ShapeDtypeStruct((b, nq, QKV_WIDTH), BF16),
        scratch_shapes=[pltpu.VMEM((2, 2, n_ctx + n_lat, tq), BF16)],
        compiler_params=_params("parallel", "parallel"),
        name="attention",
    )(*args)


def _merge_kernel(*refs, plan):
    if plan:
        (x_ref, mod_ref, g2_ref, zg_ref, py_ref, ay_ref, wp_ref, wa_ref, wo_ref, rh_ref, rl_ref,
         xo_ref, h2_ref, prow_ref, pcol_ref, cnt_ref) = refs
    else:
        x_ref, mod_ref, g2_ref, zg_ref, py_ref, ay_ref, wp_ref, wa_ref, wo_ref, xo_ref, h2_ref = refs
    d = x_ref.shape[1]
    g_pool = _sigmoid(zg_ref[:, 0:d].astype(F32))
    g_attn = _sigmoid(zg_ref[:, d:2 * d].astype(F32))
    y = (g_pool * jnp.dot(py_ref[...], wp_ref[...], preferred_element_type=F32)
         + g_attn * jnp.dot(ay_ref[...], wa_ref[...], preferred_element_type=F32))
    o = jnp.dot(y.astype(BF16), wo_ref[...], preferred_element_type=F32)
    xn = x_ref[...] + mod_ref[2:3, :] * o
    xo_ref[...] = xn
    h2 = _rms(xn) * g2_ref[...] * (1.0 + mod_ref[4:5, :]) + mod_ref[3:4, :]
    h2_hi = h2.astype(BF16)
    h2_ref[...] = h2_hi
    if plan:
        h2_lo = (h2 - h2_hi.astype(F32)).astype(BF16)
        for ti in range(prow_ref.shape[0]):
            rs = slice(ti * MOE_TG, (ti + 1) * MOE_TG)
            rows, cnt = _route_plan(h2_hi[rs], h2_lo[rs], rh_ref[...], rl_ref[...])
            prow_ref[ti] = rows
            pcol_ref[rs, :] = jnp.concatenate([rows, jnp.zeros((LANES - 8, MOE_TG), F32)], axis=0).T
            cnt_ref[ti] = jnp.broadcast_to(cnt, (N_EXPERTS, LANES))


def _merge(x, mod, mod_row, g2, z, pool_y, attn_y, wp, wa, wo, router_w, t):
    b, l, d = x.shape
    plan = router_w is not None
    nt = l // t
    tile = lambda w: pl.BlockSpec((None, t, w), lambda bi, i: (bi, i, 0))
    in_specs = [
        tile(d),
        pl.BlockSpec((None, 6, d), lambda bi, i: (mod_row(bi), 0, 0)),
        pl.BlockSpec((1, d), lambda bi, i: (0, 0)),
        tile(2 * d),
        tile(POOL_WIDTH), tile(QKV_WIDTH),
        _resident(wp.shape), _resident(wa.shape), _resident(wo.shape),
    ]
    args = [x, mod, g2, z, pool_y, attn_y, wp, wa, wo]
    out_specs = [tile(d), tile(d)]
    out_shape = [jax.ShapeDtypeStruct((b, l, d), F32), jax.ShapeDtypeStruct((b, l, d), BF16)]
    if plan:
        assert t % MOE_TG == 0
        per = t // MOE_TG
        in_specs += [_resident(router_w[0].shape)] * 2
        args += list(router_w)
        out_specs += [pl.BlockSpec((per, 8, MOE_TG), lambda bi, i: (bi * nt + i, 0, 0)),
                      tile(LANES),
                      pl.BlockSpec((per, N_EXPERTS, LANES), lambda bi, i: (bi * nt + i, 0, 0))]
        out_shape += [jax.ShapeDtypeStruct((b * nt * per, 8, MOE_TG), F32),
                      jax.ShapeDtypeStruct((b, l, LANES), F32),
                      jax.ShapeDtypeStruct((b * nt * per, N_EXPERTS, LANES), F32)]
    return pl.pallas_call(
        functools.partial(_merge_kernel, plan=plan),
        grid=(b, nt),
        in_specs=in_specs,
        out_specs=out_specs,
        out_shape=out_shape,
        compiler_params=_params("parallel", "parallel"),
        name="merge",
    )(*args)


def _swiglu(h, w1, w3, w2):
    a = jnp.dot(h, w1, preferred_element_type=F32)
    b = jnp.dot(h, w3, preferred_element_type=F32)
    return jnp.dot((a * _sigmoid(a) * b).astype(BF16), w2, preferred_element_type=F32)


def _ffn_kernel(*refs, final):
    if final:
        x_ref, h2_ref, mod_ref, w1_ref, w3_ref, w2_ref, fg_ref, o_ref = refs
    else:
        x_ref, h2_ref, mod_ref, w1_ref, w3_ref, w2_ref, o_ref = refs
    y = _swiglu(h2_ref[...], w1_ref[...], w3_ref[...], w2_ref[...])
    xn = x_ref[...] + mod_ref[5:6, :] * y
    if final:
        xn = _rms(xn) * fg_ref[...]
    o_ref[...] = xn


def _ffn(x, h2, mod, mod_row, w1, w3, w2, final_g, t):
    b, l, d = x.shape
    final = final_g is not None
    tile = pl.BlockSpec((None, t, d), lambda bi, i: (bi, i, 0))
    in_specs = [tile, tile, pl.BlockSpec((None, 6, d), lambda bi, i: (mod_row(bi), 0, 0)),
                _resident(w1.shape), _resident(w3.shape), _resident(w2.shape)]
    args = [x, h2, mod, w1, w3, w2]
    if final:
        in_specs.append(pl.BlockSpec((1, d), lambda bi, i: (0, 0)))
        args.append(final_g)
    return pl.pallas_call(
        functools.partial(_ffn_kernel, final=final),
        grid=(b, l // t),
        in_specs=in_specs,
        out_specs=tile,
        out_shape=jax.ShapeDtypeStruct((b, l, d), F32),
        compiler_params=_params("parallel", "parallel"),
        name="ffn",
    )(*args)


MOE_TG = TOKEN_TILE
MOE_TM = 1024
PIECE = 16
BIG_PIECE = 64
MOE_SLOTS = 2 * MOE_TG + N_EXPERTS * PIECE
ROUTER_ROWS = 16


def _route_plan(h2_hi, h2_lo, rh, rl):
    tg = h2_hi.shape[0]
    nt = (((1,), (1,)), ((), ()))
    both = lax.dot_general(jnp.concatenate([rh, rl], axis=0), h2_hi, nt, preferred_element_type=F32)
    logits = (both[0:ROUTER_ROWS] + both[ROUTER_ROWS:2 * ROUTER_ROWS]
              + lax.dot_general(rh, h2_lo, nt, preferred_element_type=F32))
    sub = lax.broadcasted_iota(jnp.int32, logits.shape, 0).astype(F32)
    neg = jnp.float32(-jnp.inf)
    logits = jnp.where(sub < N_EXPERTS, logits, neg)
    v1 = jnp.max(logits, axis=0, keepdims=True)
    i1 = jnp.min(jnp.where(logits == v1, sub, float(ROUTER_ROWS)), axis=0, keepdims=True)
    rest = jnp.where(sub == i1, neg, logits)
    v2 = jnp.max(rest, axis=0, keepdims=True)
    i2 = jnp.min(jnp.where(rest == v2, sub, float(ROUTER_ROWS)), axis=0, keepdims=True)
    e2 = jnp.exp(v2 - v1)
    w1 = 1.0 / (1.0 + e2)
    w2 = e2 * w1
    two = w2 != 0.0
    sel = ((sub == i1) | ((sub == i2) & two)).astype(F32)
    before = (lax.broadcasted_iota(jnp.int32, (tg, tg), 0)
              < lax.broadcasted_iota(jnp.int32, (tg, tg), 1)).astype(BF16)
    rank = jnp.dot(sel.astype(BF16), before, preferred_element_type=F32)
    cnt = jnp.sum(sel, axis=1, keepdims=True)
    cpad = jnp.ceil(cnt * (1.0 / PIECE)) * PIECE
    subc = sub[:, 0:1]
    lbase = jnp.zeros_like(cpad)
    for e in range(N_EXPERTS - 1):
        c_e = jnp.sum(jnp.where(subc == e, cpad, 0.0), axis=0, keepdims=True)
        lbase = lbase + jnp.where(subc > e, c_e, 0.0)
    slot = lbase + rank
    e_a = jnp.where(two, jnp.minimum(i1, i2), i1)
    e_b = jnp.where(two, jnp.maximum(i1, i2), i1)
    ls_a = jnp.sum(jnp.where(sub == e_a, slot, 0.0), axis=0, keepdims=True)
    ls_b = jnp.sum(jnp.where(sub == e_b, slot, 0.0), axis=0, keepdims=True)
    w_a = jnp.where(e_a == i1, w1, w2)
    w_b = jnp.where(two, jnp.where(e_b == i1, w1, w2), 0.0)
    rows = jnp.concatenate([ls_a, ls_b, w_a, w_b, jnp.zeros((4, tg), F32)], axis=0)
    return rows, cnt[0:N_EXPERTS]


def _segment_copies(i, seg, local_buf, sorted_hbm, sem, to_sorted, action):
    lb_ref, gb_ref, nb_ref, ns_ref = seg
    for e in range(N_EXPERTS):
        k = i * N_EXPERTS + e

        def copy(off, rows, k=k):
            lo = pl.multiple_of(lb_ref[k] + off, PIECE)
            go = pl.multiple_of(gb_ref[k] + off, PIECE)
            local = local_buf.at[pl.ds(lo, rows), :]
            remote = sorted_hbm.at[pl.ds(go, rows), :]
            c = pltpu.make_async_copy(local, remote, sem) if to_sorted else pltpu.make_async_copy(remote, local, sem)
            getattr(c, action)()

        def big(j, carry, copy=copy):
            copy(j * BIG_PIECE, BIG_PIECE)
            return carry

        def small(j, carry, copy=copy, k=k):
            copy(nb_ref[k] * BIG_PIECE + j * PIECE, PIECE)
            return carry

        lax.fori_loop(0, nb_ref[k], big, 0)
        lax.fori_loop(0, ns_ref[k], small, 0)


def _zero_fill(fill_ref, zbuf, out_ref, sem, action):
    def copy(go, rows):
        c = pltpu.make_async_copy(zbuf.at[pl.ds(0, rows), :], out_ref.at[pl.ds(go, rows), :], sem)
        getattr(c, action)()

    for e in range(N_EXPERTS):
        def tail(j, carry, e=e):
            copy(pl.multiple_of(fill_ref[e] + j * PIECE, PIECE), PIECE)
            return carry

        lax.fori_loop(0, fill_ref[N_EXPERTS + e], tail, 0)

    def rest(j, carry):
        copy(pl.multiple_of(fill_ref[2 * N_EXPERTS] + j * MOE_TM, MOE_TM), MOE_TM)
        return carry

    lax.fori_loop(0, fill_ref[2 * N_EXPERTS + 1], rest, 0)


def _dispatch_kernel(lb_ref, gb_ref, nb_ref, ns_ref, fill_ref, h2_ref, prow_ref, out_ref, cbuf, zbuf, sems):
    seg = (lb_ref, gb_ref, nb_ref, ns_ref)
    i = pl.program_id(0)
    nt = pl.num_programs(0)
    slot = i % 2
    tg = h2_ref.shape[0]

    @pl.when(i >= 2)
    def _():
        _segment_copies(i - 2, seg, cbuf.at[slot], out_ref, sems.at[slot], True, "wait")

    s = lax.broadcasted_iota(jnp.int32, (MOE_SLOTS, tg), 0).astype(F32)
    onehot = jnp.where(s == prow_ref[0:1, :], 1.0, jnp.where(s == prow_ref[1:2, :], 1.0, 0.0)).astype(BF16)
    cbuf[slot] = jnp.dot(onehot, h2_ref[...], preferred_element_type=F32).astype(BF16)
    _segment_copies(i, seg, cbuf.at[slot], out_ref, sems.at[slot], True, "start")

    @pl.when(i == nt - 1)
    def _():
        @pl.when(i >= 1)
        def _():
            _segment_copies(i - 1, seg, cbuf.at[1 - slot], out_ref, sems.at[1 - slot], True, "wait")

        _segment_copies(i, seg, cbuf.at[slot], out_ref, sems.at[slot], True, "wait")
        zbuf[...] = jnp.zeros_like(zbuf)
        _zero_fill(fill_ref, zbuf, out_ref, sems.at[2], "start")
        _zero_fill(fill_ref, zbuf, out_ref, sems.at[2], "wait")


def _dispatch(seg, fill, h2, prow, rows, tg):
    n, d = h2.shape
    grid_spec = pltpu.PrefetchScalarGridSpec(
        num_scalar_prefetch=len(seg) + 1,
        grid=(n // tg,),
        in_specs=[
            pl.BlockSpec((tg, d), lambda i, *_: (i, 0)),
            pl.BlockSpec((None, 8, tg), lambda i, *_: (i, 0, 0)),
        ],
        out_specs=pl.BlockSpec(memory_space=pl.ANY),
        scratch_shapes=[pltpu.VMEM((2, MOE_SLOTS, d), BF16), pltpu.VMEM((MOE_TM, d), BF16),
                        pltpu.SemaphoreType.DMA((3,))],
    )
    return pl.pallas_call(
        _dispatch_kernel,
        grid_spec=grid_spec,
        out_shape=jax.ShapeDtypeStruct((rows, d), BF16),
        compiler_params=_params("arbitrary"),
        name="moe_dispatch",
    )(*seg, fill, h2, prow)


def _group_kernel(te_ref, nu_ref, x_ref, w13_ref, w2_ref, o_ref):
    del te_ref
    live = pl.program_id(0) < nu_ref[0]

    @pl.when(live)
    def _():
        f = w2_ref.shape[0]
        ab = jnp.dot(x_ref[...], w13_ref[...], preferred_element_type=F32)
        a, b = ab[:, 0:f], ab[:, f:2 * f]
        o_ref[...] = jnp.dot((a * _sigmoid(a) * b).astype(BF16), w2_ref[...],
                             preferred_element_type=F32).astype(BF16)

    @pl.when(jnp.logical_not(live))
    def _():
        o_ref[...] = jnp.zeros_like(o_ref)


def _group(tile_expert, n_used, xs, w13, w2, tm):
    rows, d = xs.shape
    f = w2.shape[1]
    used = lambda r, te, nu: (jnp.minimum(r, nu[0] - 1), 0)
    grid_spec = pltpu.PrefetchScalarGridSpec(
        num_scalar_prefetch=2,
        grid=(rows // tm,),
        in_specs=[
            pl.BlockSpec((tm, d), used),
            pl.BlockSpec((None, d, 2 * f), lambda r, te, nu: (te[r], 0, 0)),
            pl.BlockSpec((None, f, d), lambda r, te, nu: (te[r], 0, 0)),
        ],
        out_specs=pl.BlockSpec((tm, d), lambda r, te, nu: (r, 0)),
    )
    return pl.pallas_call(
        _group_kernel,
        grid_spec=grid_spec,
        out_shape=jax.ShapeDtypeStruct((rows, d), BF16),
        compiler_params=_params("arbitrary"),
        name="moe_group",
    )(tile_expert, n_used, xs, w13, w2)


def _combine_kernel(*refs, final):
    if final:
        lb_ref, gb_ref, nb_ref, ns_ref, x_ref, pcol_ref, mod_ref, fg_ref, y_ref, o_ref, ybuf, sems = refs
    else:
        lb_ref, gb_ref, nb_ref, ns_ref, x_ref, pcol_ref, mod_ref, y_ref, o_ref, ybuf, sems = refs
    seg = (lb_ref, gb_ref, nb_ref, ns_ref)
    i = pl.program_id(0)
    nt = pl.num_programs(0)
    slot = i % 2
    tg = x_ref.shape[0]

    @pl.when(i == 0)
    def _():
        ybuf[...] = jnp.zeros_like(ybuf)
        _segment_copies(i, seg, ybuf.at[slot], y_ref, sems.at[slot], False, "start")

    @pl.when(i + 1 < nt)
    def _():
        _segment_copies(i + 1, seg, ybuf.at[1 - slot], y_ref, sems.at[1 - slot], False, "start")

    s = lax.broadcasted_iota(jnp.int32, (tg, MOE_SLOTS), 1).astype(F32)
    pc = pcol_ref[...]
    scatter = jnp.where(s == pc[:, 0:1], pc[:, 2:3], jnp.where(s == pc[:, 1:2], pc[:, 3:4], 0.0)).astype(BF16)
    _segment_copies(i, seg, ybuf.at[slot], y_ref, sems.at[slot], False, "wait")
    y = jnp.dot(scatter, ybuf[slot], preferred_element_type=F32)
    xn = x_ref[...] + mod_ref[5:6, :] * y
    if final:
        xn = _rms(xn) * fg_ref[...]
    o_ref[...] = xn


def _combine(seg, x2, pcol, mod, mod_row, final_g, ys, tg):
    n, d = x2.shape
    final = final_g is not None
    in_specs = [
        pl.BlockSpec((tg, d), lambda i, *_: (i, 0)),
        pl.BlockSpec((tg, LANES), lambda i, *_: (i, 0)),
        pl.BlockSpec((None, 6, d), lambda i, *_: (mod_row(i), 0, 0)),
    ]
    args = [x2, pcol, mod]
    if final:
        in_specs.append(pl.BlockSpec((1, d), lambda i, *_: (0, 0)))
        args.append(final_g)
    in_specs.append(pl.BlockSpec(memory_space=pl.ANY))
    args.append(ys)
    grid_spec = pltpu.PrefetchScalarGridSpec(
        num_scalar_prefetch=len(seg),
        grid=(n // tg,),
        in_specs=in_specs,
        out_specs=pl.BlockSpec((tg, d), lambda i, *_: (i, 0)),
        scratch_shapes=[pltpu.VMEM((2, MOE_SLOTS, d), BF16), pltpu.SemaphoreType.DMA((2,))],
    )
    return pl.pallas_call(
        functools.partial(_combine_kernel, final=final),
        grid_spec=grid_spec,
        out_shape=jax.ShapeDtypeStruct((n, d), F32),
        compiler_params=_params("arbitrary"),
        name="moe_combine",
    )(*seg, *args)


def _routed_moe(x, mod, batch_row, w13, w2, h2, plan, final_g):
    b, l, d = x.shape
    n = b * l
    tg, tm = MOE_TG, MOE_TM
    nt = n // tg
    x2, h22 = x.reshape(n, d), h2.reshape(n, d)
    mod_row = lambda i: batch_row((i * tg) // l)
    prow, pcol, cnt = plan
    pcol = pcol.reshape(n, LANES)

    cnt = cnt[:, :, 0].astype(jnp.int32)
    cpad = (cnt + PIECE - 1) // PIECE * PIECE
    lbase = jnp.cumsum(cpad, axis=1) - cpad
    tot = (jnp.sum(cpad, axis=0) + tm - 1) // tm * tm
    ends = jnp.cumsum(tot)
    gbase = (ends - tot)[None, :] + jnp.cumsum(cpad, axis=0) - cpad
    rows = (2 * n + nt * N_EXPERTS * (PIECE - 1) + N_EXPERTS * (tm - 1) + tm - 1) // tm * tm
    n_used = (ends[-1] // tm).astype(jnp.int32)
    tile_start = jnp.arange(rows // tm, dtype=jnp.int32) * tm
    tile_expert = jnp.sum(jnp.minimum(tile_start, ends[-1] - 1)[:, None] >= ends[None, :], axis=1).astype(jnp.int32)
    seg = tuple(a.reshape(-1).astype(jnp.int32)
                for a in (lbase, gbase, cpad // BIG_PIECE, cpad % BIG_PIECE // PIECE))
    data_end = ends - tot + jnp.sum(cpad, axis=0)
    fill = jnp.concatenate([data_end, (ends - data_end) // PIECE, ends[-1:], (rows - ends[-1:]) // tm]).astype(jnp.int32)

    xs = _dispatch(seg, fill, h22, prow, rows, tg)
    ys = _group(tile_expert, n_used.reshape(1), xs, w13, w2, tm)
    out = _combine(seg, x2, pcol, mod, mod_row, final_g, ys, tg)
    return out.reshape(b, l, d)


def _rope_tables(l):
    rows = l // GRID_W
    row = jnp.repeat(jnp.arange(rows, dtype=F32), GRID_W)
    colp = jnp.tile(jnp.arange(GRID_W, dtype=F32), rows)
    inv = ROPE_THETA ** (-jnp.arange(ROPE_PAIRS, dtype=F32) * 2.0 / AXIS_DIM)
    ang_r, ang_c = row[:, None] * inv, colp[:, None] * inv
    zero = jnp.zeros_like(ang_r)
    cos64 = jnp.concatenate([jnp.cos(ang_r)] * 2 + [jnp.cos(ang_c)] * 2, axis=1)
    up64 = jnp.concatenate([-jnp.sin(ang_r), zero, -jnp.sin(ang_c), zero], axis=1)
    dn64 = jnp.concatenate([zero, jnp.sin(ang_r), zero, jnp.sin(ang_c)], axis=1)
    return tuple(jnp.tile(a, (1, LANES // HEAD_DIM)) for a in (cos64, up64, dn64))


def _lambda_init(layer):
    return 0.8 - 0.6 * math.exp(-0.3 * layer)


def _permute_in(w):
    off_q = POOL_WIDTH
    off_g = off_q + 3 * QKV_WIDTH
    return jnp.concatenate([w[:, off_g:], w[:, off_q:off_g], w[:, :off_q]], axis=1).astype(BF16)


def kernel(x, c, ctx, c_ctx, w_mod, b_mod, norm1_g, norm2_g, w_in, pool_w, pool_scale, lam_q1, lam_k1,
           lam_q2, lam_k2, subln_g, w_pool_proj, w_attn_proj, w_out, ffn_w1, ffn_w3, ffn_w2, router_w,
           moe_w1, moe_w3, moe_w2, final_g):
    b, l, d = x.shape
    n_ctx = ctx.shape[1]
    depth = w_mod.shape[0]
    assert d == D_MODEL and b + 1 <= MOD_ROWS and l % TOKEN_TILE == 0 and n_ctx % 256 == 0

    s_in = jnp.concatenate([c, c_ctx[None, :], jnp.zeros((MOD_ROWS - b - 1, d), F32)], axis=0)
    mod_all = _modulation(s_in, w_mod, b_mod)
    tables = _rope_tables(l)
    big_tile = BIG_TILE if l % BIG_TILE == 0 else TOKEN_TILE
    lat_row = lambda bi: bi
    ctx_row = lambda bi: b
    gq, gk, gp = COL_Q // V_DIM, COL_K // V_DIM, COL_P // POOL_WIDTH
    full_kinds = ("g",) * 4 + ("q",) * 2 + ("k",) * 2 + ("v",) * 2 + ("p",)

    xc = ctx
    for layer in range(depth):
        last = layer == depth - 1
        lam0 = _lambda_init(layer)
        mod = mod_all[layer].reshape(MOD_ROWS, 6, d)
        g1 = norm1_g[layer][None, :]
        g2 = norm2_g[layer][None, :]
        w_in_p = _permute_in(w_in[layer])
        lam_p = jnp.stack([lam_q1[layer], lam_k1[layer], lam_q2[layer], lam_k2[layer]])
        g_col = subln_g[layer][:, None]
        pw = pool_w[layer].astype(BF16)
        ps = pool_scale[layer][None, :]
        wp = w_pool_proj[layer].astype(BF16)
        wa = w_attn_proj[layer].astype(BF16)
        wo = w_out[layer].astype(BF16)
        fg = final_g[None, :] if last else None
        moe_layer = layer % 2 == 1
        assert last or not moe_layer, "context tokens are only carried through dense layers"
        if moe_layer:
            rw = jnp.pad(router_w[layer // 2].T, ((0, ROUTER_ROWS - N_EXPERTS), (0, 0)))
            rw_hi = rw.astype(BF16)
            routers = (rw_hi, (rw - rw_hi.astype(F32)).astype(BF16))
            we13 = jnp.concatenate([moe_w1[layer // 2], moe_w3[layer // 2]], axis=2).astype(BF16)
            we2 = moe_w2[layer // 2].astype(BF16)
        else:
            routers = None
            wf1 = ffn_w1[layer // 2].astype(BF16)
            wf3 = ffn_w3[layer // 2].astype(BF16)
            wf2 = ffn_w2[layer // 2].astype(BF16)

        z, vt = _inproj(x, mod, lat_row, g1, w_in_p, full_kinds, tables, big_tile)
        if last:
            zc, vtc = _inproj(xc, mod, ctx_row, g1, w_in_p[:, COL_K:COL_V + QKV_WIDTH], ("k",) * 2 + ("v",) * 2,
                              None, n_ctx)
            ckb = 0
        else:
            zc, vtc = _inproj(xc, mod, ctx_row, g1, w_in_p, full_kinds, None, n_ctx)
            ckb = gk
        attn_y = _attention(lam_p, g_col, z, gq, zc, ckb, vtc, z, gk, vt, lam0, TOKEN_TILE, 1)
        pool_y = _pool(z, gp, pw, ps)
        x, h2, *plan = _merge(x, mod, lat_row, g2, z, pool_y, attn_y, wp, wa, wo, routers, big_tile)

        if not last:
            attn_yc = _attention(lam_p, g_col, zc, COL_Q // QKV_WIDTH, zc, COL_K // QKV_WIDTH, vtc, None, 0, None,
                                 lam0, TOKEN_TILE, N_HEADS)
            pool_yc = _pool(zc, gp, pw, ps)
            xc, h2c = _merge(xc, mod, ctx_row, g2, zc, pool_yc, attn_yc, wp, wa, wo, None, n_ctx)

        if moe_layer:
            x = _routed_moe(x, mod, lat_row, we13, we2, h2, plan, fg)
        else:
            x = _ffn(x, h2, mod, lat_row, wf1, wf3, wf2, fg, TOKEN_TILE)
        if not last:
            xc = _ffn(xc, h2c, mod, ctx_row, wf1, wf3, wf2, fg, n_ctx)
    return x
```

```python
import functools
import math

import jax
import jax.numpy as jnp
from jax import lax
from jax.experimental import pallas as pl
from jax.experimental.pallas import tpu as pltpu

F32 = jnp.float32
BF16 = jnp.bfloat16

D_MODEL = 1024
EPS = 1e-6
GRID_W = 64
N_HEADS = 8
HEAD_DIM = 64
V_DIM = 2 * HEAD_DIM
ROPE_THETA = 10000.0
AXIS_DIM = HEAD_DIM // 2
ROPE_PAIRS = AXIS_DIM // 2
POOL_WINDOWS = (2, 4, 8, 16)
POOL_WIDTH = 512
POOL_GROUP_DIM = POOL_WIDTH // len(POOL_WINDOWS)
POOL_EDGE = max(POOL_WINDOWS) // 2
N_EXPERTS = 8
Q_SCALE = HEAD_DIM ** -0.5
LOG2E = math.log2(math.e)
MAX_UNSHIFTED_LOG2 = 80.0
SQ_NORM_SLACK = 1.05

QKV_WIDTH = N_HEADS * V_DIM
COL_G = 0
COL_Q = COL_G + 2 * D_MODEL
COL_K = COL_Q + QKV_WIDTH
COL_V = COL_K + QKV_WIDTH
COL_P = COL_V
IN_CHUNK = 512

TOKEN_TILE = 512
BIG_TILE = 1024
MOD_TILE = 1024

LANES = 128
MOD_ROWS = 40

VMEM_LIMIT = 56 * 1024 * 1024


def _resident(shape):
    nd = len(shape)
    return pl.BlockSpec(shape, lambda *_: (0,) * nd, pipeline_mode=pl.Buffered(1))


def _params(*sem):
    return pltpu.CompilerParams(dimension_semantics=sem, vmem_limit_bytes=VMEM_LIMIT)


def _sigmoid(v):
    return 1.0 / (1.0 + jnp.exp(-v))


def _rms(v):
    return v * lax.rsqrt(jnp.mean(v * v, axis=-1, keepdims=True) + EPS)


def _mod_kernel(s_ref, w_ref, b_ref, o_ref):
    s = s_ref[...]
    s = s * _sigmoid(s)
    w = w_ref[...]
    s_hi = s.astype(BF16)
    s_lo = (s - s_hi.astype(F32)).astype(BF16)
    w_hi = w.astype(BF16)
    w_lo = (w - w_hi.astype(F32)).astype(BF16)
    acc = jnp.dot(s_hi, w_hi, preferred_element_type=F32)
    acc += jnp.dot(s_hi, w_lo, preferred_element_type=F32)
    acc += jnp.dot(s_lo, w_hi, preferred_element_type=F32)
    o_ref[...] = acc + b_ref[...]


def _modulation(s_in, w_mod, b_mod):
    depth, d, n = w_mod.shape
    tn = MOD_TILE
    return pl.pallas_call(
        _mod_kernel,
        grid=(depth, n // tn),
        in_specs=[
            pl.BlockSpec((MOD_ROWS, d), lambda l, j: (0, 0)),
            pl.BlockSpec((None, d, tn), lambda l, j: (l, 0, j)),
            pl.BlockSpec((None, 1, tn), lambda l, j: (l, 0, j)),
        ],
        out_specs=pl.BlockSpec((None, MOD_ROWS, tn), lambda l, j: (l, 0, j)),
        out_shape=jax.ShapeDtypeStruct((depth, MOD_ROWS, n), F32),
        compiler_params=_params("parallel", "parallel"),
        name="modulation",
    )(s_in, w_mod, b_mod.reshape(depth, 1, n))


def _rope(z, c, s_up, s_dn):
    up = pltpu.roll(z, LANES - ROPE_PAIRS, 1)
    dn = pltpu.roll(z, ROPE_PAIRS, 1)
    return z * c + up * s_up + dn * s_dn


def _inproj_kernel(*refs, kinds, rope):
    if rope:
        x_ref, mod_ref, g_ref, w_ref, c_ref, su_ref, sd_ref, z_ref, vt_ref = refs
    else:
        x_ref, mod_ref, g_ref, w_ref, z_ref, vt_ref = refs
    y = _rms(x_ref[...]) * g_ref[...]
    h = (y * (1.0 + mod_ref[1:2, :]) + mod_ref[0:1, :]).astype(BF16)
    z_col, head = 0, 0
    for ci, kind in enumerate(kinds):
        lo = ci * IN_CHUNK
        z = jnp.dot(h, w_ref[:, lo:lo + IN_CHUNK], preferred_element_type=F32)
        if kind == "v":
            for j in range(0, IN_CHUNK, V_DIM):
                vt_ref[head * V_DIM:(head + 1) * V_DIM, :] = z[:, j:j + V_DIM].T.astype(BF16)
                head += 1
            continue
        if rope and kind in ("q", "k"):
            c, su, sd = c_ref[...], su_ref[...], sd_ref[...]
            z = jnp.concatenate(
                [_rope(z[:, j:j + LANES], c, su, sd) for j in range(0, IN_CHUNK, LANES)], axis=1)
        if kind == "q":
            z = z * (Q_SCALE * LOG2E)
        z_ref[:, z_col:z_col + IN_CHUNK] = z.astype(BF16)
        z_col += IN_CHUNK


def _inproj(x, mod, mod_row, g, w, kinds, tables, t):
    b, l, d = x.shape
    n_v = sum(k == "v" for k in kinds)
    wz = IN_CHUNK * (len(kinds) - n_v)
    rows_vt = n_v * IN_CHUNK
    rope = tables is not None
    in_specs = [
        pl.BlockSpec((None, t, d), lambda bi, i: (bi, i, 0)),
        pl.BlockSpec((None, 6, d), lambda bi, i: (mod_row(bi), 0, 0)),
        pl.BlockSpec((1, d), lambda bi, i: (0, 0)),
        _resident((d, IN_CHUNK * len(kinds))),
    ]
    args = [x, mod, g, w]
    if rope:
        in_specs += [pl.BlockSpec((t, LANES), lambda bi, i: (i, 0))] * 3
        args += list(tables)
    return pl.pallas_call(
        functools.partial(_inproj_kernel, kinds=kinds, rope=rope),
        grid=(b, l // t),
        in_specs=in_specs,
        out_specs=[pl.BlockSpec((None, t, wz), lambda bi, i: (bi, i, 0)),
                   pl.BlockSpec((None, rows_vt, t), lambda bi, i: (bi, 0, i))],
        out_shape=[jax.ShapeDtypeStruct((b, l, wz), BF16), jax.ShapeDtypeStruct((b, rows_vt, l), BF16)],
        compiler_params=_params("parallel", "parallel"),
        name="inproj",
    )(*args)


def _pool_kernel(u_ref, pw_ref, ps_ref, o_ref):
    l = u_ref.shape[0]
    e = POOL_EDGE
    t_head = lax.broadcasted_iota(jnp.int32, (e, POOL_GROUP_DIM), 0)
    t_tail = t_head + (l - e)

    def shifted(a, k):
        r = pltpu.roll(a, k % l, 0)
        if k > 0:
            return jnp.concatenate([jnp.where(t_head >= k, r[0:e], 0.0), r[e:]], axis=0)
        return jnp.concatenate([r[0:l - e], jnp.where(t_tail < l + k, r[l - e:], 0.0)], axis=0)

    def count(t, w):
        return (jnp.minimum(t + w // 2, l) - jnp.maximum(t - w // 2, 0)).astype(F32)

    for gi, w in enumerate(POOL_WINDOWS):
        lo = gi * POOL_GROUP_DIM
        u = u_ref[:, lo:lo + POOL_GROUP_DIM].astype(F32)
        back, fwd, span = u, u, 1
        while span < w // 2:
            back = back + shifted(back, span)
            fwd = fwd + shifted(fwd, -span)
            span *= 2
        win = shifted(back, 1) + fwd
        mean = jnp.concatenate([win[0:e] / count(t_head, w), win[e:l - e] * (1.0 / w),
                                win[l - e:] / count(t_tail, w)], axis=0)
        m = (mean - u).astype(BF16)
        y = jnp.dot(m, pw_ref[gi], preferred_element_type=F32)
        o_ref[:, lo:lo + POOL_GROUP_DIM] = (y * ps_ref[:, lo:lo + POOL_GROUP_DIM]).astype(BF16)


def _pool(z, col_block, pool_w, pool_scale):
    b, l, _ = z.shape
    return pl.pallas_call(
        _pool_kernel,
        grid=(b,),
        in_specs=[
            pl.BlockSpec((None, l, POOL_WIDTH), lambda bi: (bi, 0, col_block)),
            _resident(pool_w.shape),
            pl.BlockSpec((1, POOL_WIDTH), lambda bi: (0, 0)),
        ],
        out_specs=pl.BlockSpec((None, l, POOL_WIDTH), lambda bi: (bi, 0, 0)),
        out_shape=jax.ShapeDtypeStruct((b, l, POOL_WIDTH), BF16),
        compiler_params=_params("parallel"),
        name="pool",
    )(z, pool_w, pool_scale)


def _query_chunks(nq, tq):
    return [(c0, min(tq, nq - c0)) for c0 in range(0, nq, tq)]


def _attn_kernel(*refs, n_lat, tq, lam0, heads):
    if n_lat:
        lam_ref, g_ref, q_ref, kc_ref, vtc_ref, kl_ref, vtl_ref, o_ref, p_sc = refs
    else:
        lam_ref, g_ref, q_ref, kc_ref, vtc_ref, o_ref, p_sc = refs
    lp = lam_ref[...]
    lam = (jnp.exp(jnp.sum(lp[0:1] * lp[1:2], axis=1, keepdims=True))
           - jnp.exp(jnp.sum(lp[2:3] * lp[3:4], axis=1, keepdims=True)) + lam0)
    nq = q_ref.shape[0]
    half = lax.broadcasted_iota(jnp.int32, (8, V_DIM), 1) // HEAD_DIM
    sel = (half == lax.broadcasted_iota(jnp.int32, (8, V_DIM), 0)).astype(BF16)
    nt = (((1,), (1,)), ((), ()))

    def sq_norm_max(a):
        return jnp.max(lax.dot_general(sel, a * a, nt, preferred_element_type=F32), axis=1, keepdims=True)

    def head_groups(hh):
        cs = slice(hh * V_DIM, (hh + 1) * V_DIM)
        groups = [(kc_ref[:, cs], vtc_ref[cs, :])]
        if n_lat:
            groups.append((kl_ref[:, cs], vtl_ref[cs, :]))
        return cs, groups

    def scores(shift, groups, cs, c0, n, slot):
        q = q_ref[c0:c0 + n, cs]
        lane = lax.broadcasted_iota(jnp.int32, q.shape, 1)
        sums = []
        for mi, first in enumerate((True, False)):
            qm = jnp.where((lane < HEAD_DIM) if first else (lane >= HEAD_DIM), q, jnp.zeros_like(q))
            s_t = [lax.dot_general(k, qm, nt, preferred_element_type=F32) for k, _ in groups]
            if shift:
                m = jnp.max(s_t[0], axis=0, keepdims=True)
                for s_g in s_t[1:]:
                    m = jnp.maximum(m, jnp.max(s_g, axis=0, keepdims=True))
                s_t = [s_g - m for s_g in s_t]
            l_m, row = None, 0
            for s_g in s_t:
                p_g = jnp.exp2(s_g)
                l_g = jnp.sum(p_g, axis=0, keepdims=True)
                l_m = l_g if l_m is None else l_m + l_g
                p_sc[slot, mi, row:row + p_g.shape[0], 0:n] = p_g.astype(BF16)
                row += p_g.shape[0]
            sums.append(l_m)
        return sums

    def values(groups, cs, c0, n, slot, sums):
        r = (lam * sums[0] * (1.0 / sums[1])).astype(BF16)
        o_t, row = None, 0
        for k, vt in groups:
            rows = slice(row, row + k.shape[0])
            part = jnp.dot(vt, p_sc[slot, 0, rows, 0:n] - r * p_sc[slot, 1, rows, 0:n], preferred_element_type=F32)
            o_t = part if o_t is None else o_t + part
            row += k.shape[0]
        o_t = o_t * (1.0 / sums[0])
        o_t = o_t * lax.rsqrt(jnp.mean(o_t * o_t, axis=0, keepdims=True) + EPS)
        o_t = o_t * (g_ref[...] * (1.0 - lam0))
        o_ref[c0:c0 + n, cs] = o_t.T.astype(BF16)

    def attend(shift):
        work = [(hh, c0, n) for hh in range(heads) for c0, n in _query_chunks(nq, tq)]
        pending = None
        for i, (hh, c0, n) in enumerate(work):
            cs, groups = head_groups(hh)
            sums = scores(shift, groups, cs, c0, n, i % 2)
            if pending is not None:
                values(*pending)
            pending = (groups, cs, c0, n, i % 2, sums)
        values(*pending)

    attend(False)

    bound = None
    for hh in range(heads):
        cs, groups = head_groups(hh)
        k_sq = sq_norm_max(groups[0][0])
        for k, _ in groups[1:]:
            k_sq = jnp.maximum(k_sq, sq_norm_max(k))
        b_h = sq_norm_max(q_ref[:, cs]) * k_sq
        bound = b_h if bound is None else jnp.maximum(bound, b_h)
    small = jnp.max(bound) * SQ_NORM_SLACK < MAX_UNSHIFTED_LOG2 ** 2

    @pl.when(jnp.logical_not(small))
    def _():
        attend(True)


def _attention(lam_p, g_col, zq, q_blk, zc, kc_blk, vtc, zl, kl_blk, vtl, lam0, tq, heads):
    b, nq, _ = zq.shape
    n_ctx = zc.shape[1]
    n_lat = 0 if zl is None else zl.shape[1]
    w = heads * V_DIM

    def col(blk, n):
        return pl.BlockSpec((None, n, w), lambda bi, h: (bi, 0, blk + h))

    def vt_rows(n):
        return pl.BlockSpec((None, w, n), lambda bi, h: (bi, h, 0))

    in_specs = [
        pl.BlockSpec(lam_p.shape, lambda bi, h: (0, 0)),
        pl.BlockSpec(g_col.shape, lambda bi, h: (0, 0)),
        col(q_blk, nq), col(kc_blk, n_ctx), vt_rows(n_ctx),
    ]
    args = [lam_p, g_col, zq, zc, vtc]
    if n_lat:
        in_specs += [col(kl_blk, n_lat), vt_rows(n_lat)]
        args += [zl, vtl]
    tq = min(tq, nq)
    return pl.pallas_call(
        functools.partial(_attn_kernel, n_lat=n_lat, tq=tq, lam0=lam0, heads=heads),
        grid=(b, N_HEADS // heads),
        in_specs=in_specs,
        out_specs=pl.BlockSpec((None, nq, w), lambda bi, h: (bi, 0, h)),
        out_shape=jax.ShapeDtypeStruct((b, nq, QKV_WIDTH), BF16),
        scratch_shapes=[pltpu.VMEM((2, 2, n_ctx + n_lat, tq), BF16)],
        compiler_params=_params("parallel", "parallel"),
        name="attention",
    )(*args)


def _merge_kernel(*refs, plan):
    if plan:
        (x_ref, mod_ref, g2_ref, zg_ref, py_ref, ay_ref, wp_ref, wa_ref, wo_ref, rh_ref, rl_ref,
         xo_ref, h2_ref, prow_ref, pcol_ref, cnt_ref) = refs
    else:
        x_ref, mod_ref, g2_ref, zg_ref, py_ref, ay_ref, wp_ref, wa_ref, wo_ref, xo_ref, h2_ref = refs
    d = x_ref.shape[1]
    g_pool = _sigmoid(zg_ref[:, 0:d].astype(F32))
    g_attn = _sigmoid(zg_ref[:, d:2 * d].astype(F32))
    y = (g_pool * jnp.dot(py_ref[...], wp_ref[...], preferred_element_type=F32)
         + g_attn * jnp.dot(ay_ref[...], wa_ref[...], preferred_element_type=F32))
    o = jnp.dot(y.astype(BF16), wo_ref[...], preferred_element_type=F32)
    xn = x_ref[...] + mod_ref[2:3, :] * o
    xo_ref[...] = xn
    h2 = _rms(xn) * g2_ref[...] * (1.0 + mod_ref[4:5, :]) + mod_ref[3:4, :]
    h2_hi = h2.astype(BF16)
    h2_ref[...] = h2_hi
    if plan:
        h2_lo = (h2 - h2_hi.astype(F32)).astype(BF16)
        for ti in range(prow_ref.shape[0]):
            rs = slice(ti * MOE_TG, (ti + 1) * MOE_TG)
            rows, cnt = _route_plan(h2_hi[rs], h2_lo[rs], rh_ref[...], rl_ref[...])
            prow_ref[ti] = rows
            pcol_ref[rs, :] = jnp.concatenate([rows, jnp.zeros((LANES - 8, MOE_TG), F32)], axis=0).T
            cnt_ref[ti] = jnp.broadcast_to(cnt, (N_EXPERTS, LANES))


def _merge(x, mod, mod_row, g2, z, pool_y, attn_y, wp, wa, wo, router_w, t):
    b, l, d = x.shape
    plan = router_w is not None
    nt = l // t
    tile = lambda w: pl.BlockSpec((None, t, w), lambda bi, i: (bi, i, 0))
    in_specs = [
        tile(d),
        pl.BlockSpec((None, 6, d), lambda bi, i: (mod_row(bi), 0, 0)),
        pl.BlockSpec((1, d), lambda bi, i: (0, 0)),
        tile(2 * d),
        tile(POOL_WIDTH), tile(QKV_WIDTH),
        _resident(wp.shape), _resident(wa.shape), _resident(wo.shape),
    ]
    args = [x, mod, g2, z, pool_y, attn_y, wp, wa, wo]
    out_specs = [tile(d), tile(d)]
    out_shape = [jax.ShapeDtypeStruct((b, l, d), F32), jax.ShapeDtypeStruct((b, l, d), BF16)]
    if plan:
        assert t % MOE_TG == 0
        per = t // MOE_TG
        in_specs += [_resident(router_w[0].shape)] * 2
        args += list(router_w)
        out_specs += [pl.BlockSpec((per, 8, MOE_TG), lambda bi, i: (bi * nt + i, 0, 0)),
                      tile(LANES),
                      pl.BlockSpec((per, N_EXPERTS, LANES), lambda bi, i: (bi * nt + i, 0, 0))]
        out_shape += [jax.ShapeDtypeStruct((b * nt * per, 8, MOE_TG), F32),
                      jax.ShapeDtypeStruct((b, l, LANES), F32),
                      jax.ShapeDtypeStruct((b * nt * per, N_EXPERTS, LANES), F32)]
    return pl.pallas_call(
        functools.partial(_merge_kernel, plan=plan),
        grid=(b, nt),
        in_specs=in_specs,
        out_specs=out_specs,
        out_shape=out_shape,
        compiler_params=_params("parallel", "parallel"),
        name="merge",
    )(*args)


def _swiglu(h, w1, w3, w2):
    a = jnp.dot(h, w1, preferred_element_type=F32)
    b = jnp.dot(h, w3, preferred_element_type=F32)
    return jnp.dot((a * _sigmoid(a) * b).astype(BF16), w2, preferred_element_type=F32)


def _ffn_kernel(*refs, final):
    if final:
        x_ref, h2_ref, mod_ref, w1_ref, w3_ref, w2_ref, fg_ref, o_ref = refs
    else:
        x_ref, h2_ref, mod_ref, w1_ref, w3_ref, w2_ref, o_ref = refs
    y = _swiglu(h2_ref[...], w1_ref[...], w3_ref[...], w2_ref[...])
    xn = x_ref[...] + mod_ref[5:6, :] * y
    if final:
        xn = _rms(xn) * fg_ref[...]
    o_ref[...] = xn


def _ffn(x, h2, mod, mod_row, w1, w3, w2, final_g, t):
    b, l, d = x.shape
    final = final_g is not None
    tile = pl.BlockSpec((None, t, d), lambda bi, i: (bi, i, 0))
    in_specs = [tile, tile, pl.BlockSpec((None, 6, d), lambda bi, i: (mod_row(bi), 0, 0)),
                _resident(w1.shape), _resident(w3.shape), _resident(w2.shape)]
    args = [x, h2, mod, w1, w3, w2]
    if final:
        in_specs.append(pl.BlockSpec((1, d), lambda bi, i: (0, 0)))
        args.append(final_g)
    return pl.pallas_call(
        functools.partial(_ffn_kernel, final=final),
        grid=(b, l // t),
        in_specs=in_specs,
        out_specs=tile,
        out_shape=jax.ShapeDtypeStruct((b, l, d), F32),
        compiler_params=_params("parallel", "parallel"),
        name="ffn",
    )(*args)


MOE_TG = TOKEN_TILE
MOE_TM = 1024
PIECE = 16
BIG_PIECE = 128
MOE_SLOTS = 2 * MOE_TG + N_EXPERTS * PIECE
ROUTER_ROWS = 16


def _route_plan(h2_hi, h2_lo, rh, rl):
    tg = h2_hi.shape[0]
    nt = (((1,), (1,)), ((), ()))
    both = lax.dot_general(jnp.concatenate([rh, rl], axis=0), h2_hi, nt, preferred_element_type=F32)
    logits = (both[0:ROUTER_ROWS] + both[ROUTER_ROWS:2 * ROUTER_ROWS]
              + lax.dot_general(rh, h2_lo, nt, preferred_element_type=F32))
    sub = lax.broadcasted_iota(jnp.int32, logits.shape, 0).astype(F32)
    neg = jnp.float32(-jnp.inf)
    logits = jnp.where(sub < N_EXPERTS, logits, neg)
    v1 = jnp.max(logits, axis=0, keepdims=True)
    i1 = jnp.min(jnp.where(logits == v1, sub, float(ROUTER_ROWS)), axis=0, keepdims=True)
    rest = jnp.where(sub == i1, neg, logits)
    v2 = jnp.max(rest, axis=0, keepdims=True)
    i2 = jnp.min(jnp.where(rest == v2, sub, float(ROUTER_ROWS)), axis=0, keepdims=True)
    e2 = jnp.exp(v2 - v1)
    w1 = 1.0 / (1.0 + e2)
    w2 = e2 * w1
    two = w2 != 0.0
    sel = ((sub == i1) | ((sub == i2) & two)).astype(F32)
    before = (lax.broadcasted_iota(jnp.int32, (tg, tg), 0)
              < lax.broadcasted_iota(jnp.int32, (tg, tg), 1)).astype(BF16)
    rank = jnp.dot(sel.astype(BF16), before, preferred_element_type=F32)
    cnt = jnp.sum(sel, axis=1, keepdims=True)
    cpad = jnp.ceil(cnt * (1.0 / PIECE)) * PIECE
    subc = sub[:, 0:1]
    lbase = jnp.zeros_like(cpad)
    for e in range(N_EXPERTS - 1):
        c_e = jnp.sum(jnp.where(subc == e, cpad, 0.0), axis=0, keepdims=True)
        lbase = lbase + jnp.where(subc > e, c_e, 0.0)
    slot = lbase + rank
    e_a = jnp.where(two, jnp.minimum(i1, i2), i1)
    e_b = jnp.where(two, jnp.maximum(i1, i2), i1)
    ls_a = jnp.sum(jnp.where(sub == e_a, slot, 0.0), axis=0, keepdims=True)
    ls_b = jnp.sum(jnp.where(sub == e_b, slot, 0.0), axis=0, keepdims=True)
    w_a = jnp.where(e_a == i1, w1, w2)
    w_b = jnp.where(two, jnp.where(e_b == i1, w1, w2), 0.0)
    rows = jnp.concatenate([ls_a, ls_b, w_a, w_b, jnp.zeros((4, tg), F32)], axis=0)
    return rows, cnt[0:N_EXPERTS]


def _segment_copies(i, seg, local_buf, sorted_hbm, sem, to_sorted, action):
    lb_ref, gb_ref, nb_ref, ns_ref = seg
    for e in range(N_EXPERTS):
        k = i * N_EXPERTS + e

        def copy(off, rows, k=k):
            lo = pl.multiple_of(lb_ref[k] + off, PIECE)
            go = pl.multiple_of(gb_ref[k] + off, PIECE)
            local = local_buf.at[pl.ds(lo, rows), :]
            remote = sorted_hbm.at[pl.ds(go, rows), :]
            c = pltpu.make_async_copy(local, remote, sem) if to_sorted else pltpu.make_async_copy(remote, local, sem)
            getattr(c, action)()

        def big(j, carry, copy=copy):
            copy(j * BIG_PIECE, BIG_PIECE)
            return carry

        def small(j, carry, copy=copy, k=k):
            copy(nb_ref[k] * BIG_PIECE + j * PIECE, PIECE)
            return carry

        lax.fori_loop(0, nb_ref[k], big, 0)
        lax.fori_loop(0, ns_ref[k], small, 0)


def _zero_fill(fill_ref, zbuf, out_ref, sem, action):
    def copy(go, rows):
        c = pltpu.make_async_copy(zbuf.at[pl.ds(0, rows), :], out_ref.at[pl.ds(go, rows), :], sem)
        getattr(c, action)()

    for e in range(N_EXPERTS):
        def tail(j, carry, e=e):
            copy(pl.multiple_of(fill_ref[e] + j * PIECE, PIECE), PIECE)
            return carry

        lax.fori_loop(0, fill_ref[N_EXPERTS + e], tail, 0)

    def rest(j, carry):
        copy(pl.multiple_of(fill_ref[2 * N_EXPERTS] + j * MOE_TM, MOE_TM), MOE_TM)
        return carry

    lax.fori_loop(0, fill_ref[2 * N_EXPERTS + 1], rest, 0)


def _dispatch_kernel(lb_ref, gb_ref, nb_ref, ns_ref, fill_ref, h2_ref, prow_ref, out_ref, cbuf, zbuf, sems):
    seg = (lb_ref, gb_ref, nb_ref, ns_ref)
    i = pl.program_id(0)
    nt = pl.num_programs(0)
    slot = i % 2
    tg = h2_ref.shape[0]

    @pl.when(i >= 2)
    def _():
        _segment_copies(i - 2, seg, cbuf.at[slot], out_ref, sems.at[slot], True, "wait")

    s = lax.broadcasted_iota(jnp.int32, (MOE_SLOTS, tg), 0).astype(F32)
    onehot = jnp.where(s == prow_ref[0:1, :], 1.0, jnp.where(s == prow_ref[1:2, :], 1.0, 0.0)).astype(BF16)
    cbuf[slot] = jnp.dot(onehot, h2_ref[...], preferred_element_type=F32).astype(BF16)
    _segment_copies(i, seg, cbuf.at[slot], out_ref, sems.at[slot], True, "start")

    @pl.when(i == nt - 1)
    def _():
        @pl.when(i >= 1)
        def _():
            _segment_copies(i - 1, seg, cbuf.at[1 - slot], out_ref, sems.at[1 - slot], True, "wait")

        _segment_copies(i, seg, cbuf.at[slot], out_ref, sems.at[slot], True, "wait")
        zbuf[...] = jnp.zeros_like(zbuf)
        _zero_fill(fill_ref, zbuf, out_ref, sems.at[2], "start")
        _zero_fill(fill_ref, zbuf, out_ref, sems.at[2], "wait")


def _dispatch(seg, fill, h2, prow, rows, tg):
    n, d = h2.shape
    grid_spec = pltpu.PrefetchScalarGridSpec(
        num_scalar_prefetch=len(seg) + 1,
        grid=(n // tg,),
        in_specs=[
            pl.BlockSpec((tg, d), lambda i, *_: (i, 0)),
            pl.BlockSpec((None, 8, tg), lambda i, *_: (i, 0, 0)),
        ],
        out_specs=pl.BlockSpec(memory_space=pl.ANY),
        scratch_shapes=[pltpu.VMEM((2, MOE_SLOTS, d), BF16), pltpu.VMEM((MOE_TM, d), BF16),
                        pltpu.SemaphoreType.DMA((3,))],
    )
    return pl.pallas_call(
        _dispatch_kernel,
        grid_spec=grid_spec,
        out_shape=jax.ShapeDtypeStruct((rows, d), BF16),
        compiler_params=_params("arbitrary"),
        name="moe_dispatch",
    )(*seg, fill, h2, prow)


def _group_kernel(te_ref, nu_ref, x_ref, w13_ref, w2_ref, o_ref):
    del te_ref
    live = pl.program_id(0) < nu_ref[0]

    @pl.when(live)
    def _():
        f = w2_ref.shape[0]
        ab = jnp.dot(x_ref[...], w13_ref[...], preferred_element_type=F32)
        a, b = ab[:, 0:f], ab[:, f:2 * f]
        o_ref[...] = jnp.dot((a * _sigmoid(a) * b).astype(BF16), w2_ref[...],
                             preferred_element_type=F32).astype(BF16)

    @pl.when(jnp.logical_not(live))
    def _():
        o_ref[...] = jnp.zeros_like(o_ref)


def _group(tile_expert, n_used, xs, w13, w2, tm):
    rows, d = xs.shape
    f = w2.shape[1]
    used = lambda r, te, nu: (jnp.minimum(r, nu[0] - 1), 0)
    grid_spec = pltpu.PrefetchScalarGridSpec(
        num_scalar_prefetch=2,
        grid=(rows // tm,),
        in_specs=[
            pl.BlockSpec((tm, d), used),
            pl.BlockSpec((None, d, 2 * f), lambda r, te, nu: (te[r], 0, 0)),
            pl.BlockSpec((None, f, d), lambda r, te, nu: (te[r], 0, 0)),
        ],
        out_specs=pl.BlockSpec((tm, d), lambda r, te, nu: (r, 0)),
    )
    return pl.pallas_call(
        _group_kernel,
        grid_spec=grid_spec,
        out_shape=jax.ShapeDtypeStruct((rows, d), BF16),
        compiler_params=_params("arbitrary"),
        name="moe_group",
    )(tile_expert, n_used, xs, w13, w2)


def _combine_kernel(*refs, final):
    if final:
        lb_ref, gb_ref, nb_ref, ns_ref, x_ref, pcol_ref, mod_ref, fg_ref, y_ref, o_ref, ybuf, sems = refs
    else:
        lb_ref, gb_ref, nb_ref, ns_ref, x_ref, pcol_ref, mod_ref, y_ref, o_ref, ybuf, sems = refs
    seg = (lb_ref, gb_ref, nb_ref, ns_ref)
    i = pl.program_id(0)
    nt = pl.num_programs(0)
    slot = i % 2
    tg = x_ref.shape[0]

    @pl.when(i == 0)
    def _():
        ybuf[...] = jnp.zeros_like(ybuf)
        _segment_copies(i, seg, ybuf.at[slot], y_ref, sems.at[slot], False, "start")

    @pl.when(i + 1 < nt)
    def _():
        _segment_copies(i + 1, seg, ybuf.at[1 - slot], y_ref, sems.at[1 - slot], False, "start")

    s = lax.broadcasted_iota(jnp.int32, (tg, MOE_SLOTS), 1).astype(F32)
    pc = pcol_ref[...]
    scatter = jnp.where(s == pc[:, 0:1], pc[:, 2:3], jnp.where(s == pc[:, 1:2], pc[:, 3:4], 0.0)).astype(BF16)
    _segment_copies(i, seg, ybuf.at[slot], y_ref, sems.at[slot], False, "wait")
    y = jnp.dot(scatter, ybuf[slot], preferred_element_type=F32)
    xn = x_ref[...] + mod_ref[5:6, :] * y
    if final:
        xn = _rms(xn) * fg_ref[...]
    o_ref[...] = xn


def _combine(seg, x2, pcol, mod, mod_row, final_g, ys, tg):
    n, d = x2.shape
    final = final_g is not None
    in_specs = [
        pl.BlockSpec((tg, d), lambda i, *_: (i, 0)),
        pl.BlockSpec((tg, LANES), lambda i, *_: (i, 0)),
        pl.BlockSpec((None, 6, d), lambda i, *_: (mod_row(i), 0, 0)),
    ]
    args = [x2, pcol, mod]
    if final:
        in_specs.append(pl.BlockSpec((1, d), lambda i, *_: (0, 0)))
        args.append(final_g)
    in_specs.append(pl.BlockSpec(memory_space=pl.ANY))
    args.append(ys)
    grid_spec = pltpu.PrefetchScalarGridSpec(
        num_scalar_prefetch=len(seg),
        grid=(n // tg,),
        in_specs=in_specs,
        out_specs=pl.BlockSpec((tg, d), lambda i, *_: (i, 0)),
        scratch_shapes=[pltpu.VMEM((2, MOE_SLOTS, d), BF16), pltpu.SemaphoreType.DMA((2,))],
    )
    return pl.pallas_call(
        functools.partial(_combine_kernel, final=final),
        grid_spec=grid_spec,
        out_shape=jax.ShapeDtypeStruct((n, d), F32),
        compiler_params=_params("arbitrary"),
        name="moe_combine",
    )(*seg, *args)


def _routed_moe(x, mod, batch_row, w13, w2, h2, plan, final_g):
    b, l, d = x.shape
    n = b * l
    tg, tm = MOE_TG, MOE_TM
    nt = n // tg
    x2, h22 = x.reshape(n, d), h2.reshape(n, d)
    mod_row = lambda i: batch_row((i * tg) // l)
    prow, pcol, cnt = plan
    pcol = pcol.reshape(n, LANES)

    cnt = cnt[:, :, 0].astype(jnp.int32)
    cpad = (cnt + PIECE - 1) // PIECE * PIECE
    lbase = jnp.cumsum(cpad, axis=1) - cpad
    tot = (jnp.sum(cpad, axis=0) + tm - 1) // tm * tm
    ends = jnp.cumsum(tot)
    gbase = (ends - tot)[None, :] + jnp.cumsum(cpad, axis=0) - cpad
    rows = (2 * n + nt * N_EXPERTS * (PIECE - 1) + N_EXPERTS * (tm - 1) + tm - 1) // tm * tm
    n_used = (ends[-1] // tm).astype(jnp.int32)
    tile_start = jnp.arange(rows // tm, dtype=jnp.int32) * tm
    tile_expert = jnp.sum(jnp.minimum(tile_start, ends[-1] - 1)[:, None] >= ends[None, :], axis=1).astype(jnp.int32)
    seg = tuple(a.reshape(-1).astype(jnp.int32)
                for a in (lbase, gbase, cpad // BIG_PIECE, cpad % BIG_PIECE // PIECE))
    data_end = ends - tot + jnp.sum(cpad, axis=0)
    fill = jnp.concatenate([data_end, (ends - data_end) // PIECE, ends[-1:], (rows - ends[-1:]) // tm]).astype(jnp.int32)

    xs = _dispatch(seg, fill, h22, prow, rows, tg)
    ys = _group(tile_expert, n_used.reshape(1), xs, w13, w2, tm)
    out = _combine(seg, x2, pcol, mod, mod_row, final_g, ys, tg)
    return out.reshape(b, l, d)


def _rope_tables(l):
    rows = l // GRID_W
    row = jnp.repeat(jnp.arange(rows, dtype=F32), GRID_W)
    colp = jnp.tile(jnp.arange(GRID_W, dtype=F32), rows)
    inv = ROPE_THETA ** (-jnp.arange(ROPE_PAIRS, dtype=F32) * 2.0 / AXIS_DIM)
    ang_r, ang_c = row[:, None] * inv, colp[:, None] * inv
    zero = jnp.zeros_like(ang_r)
    cos64 = jnp.concatenate([jnp.cos(ang_r)] * 2 + [jnp.cos(ang_c)] * 2, axis=1)
    up64 = jnp.concatenate([-jnp.sin(ang_r), zero, -jnp.sin(ang_c), zero], axis=1)
    dn64 = jnp.concatenate([zero, jnp.sin(ang_r), zero, jnp.sin(ang_c)], axis=1)
    return tuple(jnp.tile(a, (1, LANES // HEAD_DIM)) for a in (cos64, up64, dn64))


def _lambda_init(layer):
    return 0.8 - 0.6 * math.exp(-0.3 * layer)


def _permute_in(w):
    off_q = POOL_WIDTH
    off_g = off_q + 3 * QKV_WIDTH
    return jnp.concatenate([w[:, off_g:], w[:, off_q:off_g], w[:, :off_q]], axis=1).astype(BF16)


def kernel(x, c, ctx, c_ctx, w_mod, b_mod, norm1_g, norm2_g, w_in, pool_w, pool_scale, lam_q1, lam_k1,
           lam_q2, lam_k2, subln_g, w_pool_proj, w_attn_proj, w_out, ffn_w1, ffn_w3, ffn_w2, router_w,
           moe_w1, moe_w3, moe_w2, final_g):
    b, l, d = x.shape
    n_ctx = ctx.shape[1]
    depth = w_mod.shape[0]
    assert d == D_MODEL and b + 1 <= MOD_ROWS and l % TOKEN_TILE == 0 and n_ctx % 256 == 0

    s_in = jnp.concatenate([c, c_ctx[None, :], jnp.zeros((MOD_ROWS - b - 1, d), F32)], axis=0)
    mod_all = _modulation(s_in, w_mod, b_mod)
    tables = _rope_tables(l)
    big_tile = BIG_TILE if l % BIG_TILE == 0 else TOKEN_TILE
    lat_row = lambda bi: bi
    ctx_row = lambda bi: b
    gq, gk, gp = COL_Q // V_DIM, COL_K // V_DIM, COL_P // POOL_WIDTH
    full_kinds = ("g",) * 4 + ("q",) * 2 + ("k",) * 2 + ("v",) * 2 + ("p",)

    xc = ctx
    for layer in range(depth):
        last = layer == depth - 1
        lam0 = _lambda_init(layer)
        mod = mod_all[layer].reshape(MOD_ROWS, 6, d)
        g1 = norm1_g[layer][None, :]
        g2 = norm2_g[layer][None, :]
        w_in_p = _permute_in(w_in[layer])
        lam_p = jnp.stack([lam_q1[layer], lam_k1[layer], lam_q2[layer], lam_k2[layer]])
        g_col = subln_g[layer][:, None]
        pw = pool_w[layer].astype(BF16)
        ps = pool_scale[layer][None, :]
        wp = w_pool_proj[layer].astype(BF16)
        wa = w_attn_proj[layer].astype(BF16)
        wo = w_out[layer].astype(BF16)
        fg = final_g[None, :] if last else None
        moe_layer = layer % 2 == 1
        assert last or not moe_layer, "context tokens are only carried through dense layers"
        if moe_layer:
            rw = jnp.pad(router_w[layer // 2].T, ((0, ROUTER_ROWS - N_EXPERTS), (0, 0)))
            rw_hi = rw.astype(BF16)
            routers = (rw_hi, (rw - rw_hi.astype(F32)).astype(BF16))
            we13 = jnp.concatenate([moe_w1[layer // 2], moe_w3[layer // 2]], axis=2).astype(BF16)
            we2 = moe_w2[layer // 2].astype(BF16)
        else:
            routers = None
            wf1 = ffn_w1[layer // 2].astype(BF16)
            wf3 = ffn_w3[layer // 2].astype(BF16)
            wf2 = ffn_w2[layer // 2].astype(BF16)

        z, vt = _inproj(x, mod, lat_row, g1, w_in_p, full_kinds, tables, big_tile)
        if last:
            zc, vtc = _inproj(xc, mod, ctx_row, g1, w_in_p[:, COL_K:COL_V + QKV_WIDTH], ("k",) * 2 + ("v",) * 2,
                              None, n_ctx)
            ckb = 0
        else:
            zc, vtc = _inproj(xc, mod, ctx_row, g1, w_in_p, full_kinds, None, n_ctx)
            ckb = gk
        attn_y = _attention(lam_p, g_col, z, gq, zc, ckb, vtc, z, gk, vt, lam0, TOKEN_TILE, 1)
        pool_y = _pool(z, gp, pw, ps)
        x, h2, *plan = _merge(x, mod, lat_row, g2, z, pool_y, attn_y, wp, wa, wo, routers, big_tile)

        if not last:
            attn_yc = _attention(lam_p, g_col, zc, COL_Q // QKV_WIDTH, zc, COL_K // QKV_WIDTH, vtc, None, 0, None,
                                 lam0, TOKEN_TILE, N_HEADS)
            pool_yc = _pool(zc, gp, pw, ps)
            xc, h2c = _merge(xc, mod, ctx_row, g2, zc, pool_yc, attn_yc, wp, wa, wo, None, n_ctx)

        if moe_layer:
            x = _routed_moe(x, mod, lat_row, we13, we2, h2, plan, fg)
        else:
            x = _ffn(x, h2, mod, lat_row, wf1, wf3, wf2, fg, TOKEN_TILE)
        if not last:
            xc = _ffn(xc, h2c, mod, ctx_row, wf1, wf3, wf2, fg, n_ctx)
    return x
```
